```python
import math
import jax, jax.numpy as jnp
from jax import lax
import numpy as np

D_MODEL = 1024
BATCH = 16
SEQ = 4096
DEPTH = 2
DEC_BATCH = 16
DEC_SEQ = 32
PAST_LEN = 1024

CHUNK = 64
Q_BLOCK = 128
N_AB = (DEPTH + 1) // 2
N_C = DEPTH // 2
GLA_H = 4
GLA_DK = 64
GLA_DV = 128
GLA_LR = 16
GLA_TAU = 16.0
GDN_H = 4
GDN_DK = 128
GDN_DV = 128
CONV_W = 4
GDN_CONV_DIM = GDN_H * (2 * GDN_DK + GDN_DV)
AB_IN_WIDTHS = (GLA_H * GLA_DK, GLA_H * GLA_DK, GLA_H * GLA_DV, GLA_LR, GLA_H * GLA_DV,
                GDN_CONV_DIM, GDN_H, GDN_H, GDN_H * GDN_DV)
AB_IN = 3608
AB_OUT = GLA_H * GLA_DV + GDN_H * GDN_DV
MLA_H = 16
MLA_NOPE = 64
MLA_ROPE = 32
MLA_V = 64
MLA_Q_RANK = 384
MLA_KV_RANK = 256
MLA_IN = MLA_Q_RANK + MLA_KV_RANK + MLA_ROPE
ROPE_THETA = 10000.0
N_GROUPS = 4
EXPERTS_PER_GROUP = 8
N_EXPERTS = N_GROUPS * EXPERTS_PER_GROUP
TOP_K = 2
D_EXPERT = 512
MOE_BLOCK = 128
EPS = 1e-6

kernel_name = 'hybrid_stream_gla_gdn_mla_hmoe_step'


def rmsnorm(x, w):
    xf = x.astype(jnp.float32)
    y = xf * lax.rsqrt(jnp.mean(xf * xf, axis=-1, keepdims=True) + EPS)
    return (y * w.astype(jnp.float32)).astype(x.dtype)


def l2norm(x):
    return x * lax.rsqrt(jnp.sum(x * x, axis=-1, keepdims=True) + EPS)


def ada_modulation(c, w, b):
    m = (jax.nn.silu(c) @ w + b)[:, None, :]
    return jnp.split(m, 3, axis=-1)


def to_heads(x, n):
    b, l, w = x.shape
    return x.reshape(b, l, n, w // n).transpose(0, 2, 1, 3)


def from_heads(x):
    b, n, l, d = x.shape
    return x.transpose(0, 2, 1, 3).reshape(b, l, n * d)


def rope(x, pos):
    half = x.shape[-1] // 2
    inv = jnp.exp(-math.log(ROPE_THETA) * jnp.arange(half, dtype=jnp.float32) / half)
    ang = pos.astype(jnp.float32)[:, None] * inv[None, :]
    if x.ndim == 4:
        ang = ang[:, None, :]
    cos, sin = jnp.cos(ang), jnp.sin(ang)
    xf = x.astype(jnp.float32)
    x1, x2 = xf[..., :half], xf[..., half:]
    return jnp.concatenate([x1 * cos - x2 * sin, x1 * sin + x2 * cos], axis=-1).astype(x.dtype)


def chunk_scan(step, s0, xs):
    b, h, l = xs[0].shape[:3]
    nc = max(l // CHUNK, 1)
    lc = l // nc

    def split(a):
        return jnp.moveaxis(a.reshape(a.shape[:2] + (nc, lc) + a.shape[3:]), 2, 0)

    s, out = lax.scan(lambda s, x: step(s, *x), s0.astype(jnp.float32), tuple(split(a) for a in xs))
    out = jnp.moveaxis(out, 0, 2)
    return s, out.reshape((b, h, l) + out.shape[4:])


def gla_chunk(S, q, k, v, la):
    L = q.shape[2]
    b = jnp.cumsum(la, axis=2)
    incl = jnp.tril(jnp.ones((L, L), bool))[:, :, None]
    dec = jnp.exp(jnp.where(incl, b[:, :, :, None, :] - b[:, :, None, :, :], -jnp.inf))
    att = jnp.einsum('bhid,bhjd,bhijd->bhij', q, k, dec)
    o = jnp.einsum('bhij,bhjv->bhiv', att, v) + jnp.einsum('bhid,bhdv->bhiv', q * jnp.exp(b), S)
    b_last = b[:, :, -1:, :]
    S_new = jnp.exp(b_last[:, :, 0, :])[..., None] * S + jnp.einsum('bhjd,bhjv->bhdv', k * jnp.exp(b_last - b), v)
    return S_new, o


def gdn_chunk(S, q, k, v, g, beta):
    L = q.shape[2]
    gam = jnp.cumsum(g, axis=-1)
    incl = jnp.tril(jnp.ones((L, L), bool))
    strict = jnp.tril(jnp.ones((L, L), bool), -1)
    dec = jnp.exp(jnp.where(incl, gam[..., :, None] - gam[..., None, :], -jnp.inf))
    kk = jnp.einsum('bhid,bhjd->bhij', k, k)
    a_mat = jnp.eye(L, dtype=jnp.float32) + jnp.where(strict, beta[..., :, None] * kk * dec, 0.0)
    rhs = jnp.concatenate([v * beta[..., None], k * (beta * jnp.exp(gam))[..., None]], axis=-1)
    sol = lax.linalg.triangular_solve(a_mat, rhs, left_side=True, lower=True)
    u, w = sol[..., :GDN_DV], sol[..., GDN_DV:]
    delta = u - jnp.einsum('bhlk,bhkv->bhlv', w, S)
    qk = jnp.einsum('bhid,bhjd->bhij', q, k) * dec
    o = jnp.einsum('bhik,bhkv->bhiv', q * jnp.exp(gam)[..., None], S) + jnp.einsum('bhij,bhjv->bhiv', qk, delta)
    g_last = gam[..., -1:]
    S_new = jnp.exp(g_last)[..., None] * S + jnp.einsum('bhjk,bhjv->bhkv', k * jnp.exp(g_last - gam)[..., None], delta)
    return S_new, o


def ab_mixer(h, gla_s0, gdn_s0, conv_hist, w_in, w_alpha, b_alpha, gla_nw, conv_w, a_log, dt_bias, gdn_nw, w_out):
    B, L, _ = h.shape
    z = h @ w_in
    offs = np.cumsum(AB_IN_WIDTHS)[:-1].tolist()
    gq, gk, gv, glr, gr, qkv, ga, gb, gg = jnp.split(z, offs, axis=-1)
    q = to_heads(gq, GLA_H).astype(jnp.float32) * GLA_DK ** -0.5
    k = to_heads(gk, GLA_H).astype(jnp.float32)
    v = to_heads(gv, GLA_H).astype(jnp.float32)
    la = jax.nn.log_sigmoid((glr @ w_alpha + b_alpha).astype(jnp.float32)) / GLA_TAU
    la = to_heads(la, GLA_H)
    gla_s, o_gla = chunk_scan(gla_chunk, gla_s0, (q, k, v, la))
    o_gla = from_heads(rmsnorm(o_gla.astype(h.dtype), gla_nw)) * jax.nn.silu(gr)
    xc = jnp.concatenate([conv_hist.astype(qkv.dtype), qkv], axis=1)
    new_conv = xc[:, -(CONV_W - 1):, :]
    conv = jax.nn.silu(sum(xc[:, i:i + L, :] * conv_w[i] for i in range(CONV_W)))
    q2, k2, v2 = jnp.split(conv, [GDN_H * GDN_DK, 2 * GDN_H * GDN_DK], axis=-1)
    q2 = l2norm(to_heads(q2, GDN_H).astype(jnp.float32)) * GDN_DK ** -0.5
    k2 = l2norm(to_heads(k2, GDN_H).astype(jnp.float32))
    v2 = to_heads(v2, GDN_H).astype(jnp.float32)
    beta = jax.nn.sigmoid(gb.astype(jnp.float32)).transpose(0, 2, 1)
    g = (-jnp.exp(a_log.astype(jnp.float32)) *
         jax.nn.softplus(ga.astype(jnp.float32) + dt_bias.astype(jnp.float32))).transpose(0, 2, 1)
    gdn_s, o_gdn = chunk_scan(gdn_chunk, gdn_s0, (q2, k2, v2, g, beta))
    o_gdn = from_heads(rmsnorm(o_gdn.astype(h.dtype), gdn_nw)) * jax.nn.silu(gg)
    out = jnp.concatenate([o_gla, o_gdn], axis=-1) @ w_out
    return out, gla_s, gdn_s, new_conv


def mla_attend(q_nope, q_rope, k_nope, k_rope, v, q_pos, k_pos):
    scale = (MLA_NOPE + MLA_ROPE) ** -0.5
    k_chunk = k_pos // CHUNK

    def block(args):
        qn, qr, qp = args
        s = (jnp.einsum('bqhd,bkhd->bhqk', qn, k_nope) + jnp.einsum('bqhd,bkd->bhqk', qr, k_rope)).astype(jnp.float32)
        mask = (qp // CHUNK)[:, None] >= k_chunk[None, :]
        p = jax.nn.softmax(jnp.where(mask, s * scale, -jnp.inf), axis=-1).astype(v.dtype)
        return jnp.einsum('bhqk,bkhd->bqhd', p, v)

    B, L = q_nope.shape[:2]
    if L <= Q_BLOCK:
        return block((q_nope, q_rope, q_pos))
    nb = L // Q_BLOCK

    def to_blocks(a):
        return jnp.moveaxis(a.reshape((B, nb, Q_BLOCK) + a.shape[2:]), 1, 0)

    out = lax.map(block, (to_blocks(q_nope), to_blocks(q_rope), q_pos.reshape(nb, Q_BLOCK)))
    return jnp.moveaxis(out, 0, 1).reshape((B, L) + out.shape[3:])


def mla_mixer(h, q_pos, past_ckv, past_kr, w_in, q_nw, w_uq, kv_nw, w_ukv, w_out):
    B, L, _ = h.shape
    cq, ckv, kr = jnp.split(h @ w_in, [MLA_Q_RANK, MLA_Q_RANK + MLA_KV_RANK], axis=-1)
    q = (rmsnorm(cq, q_nw) @ w_uq).reshape(B, L, MLA_H, MLA_NOPE + MLA_ROPE)
    q_nope, q_rope = q[..., :MLA_NOPE], rope(q[..., MLA_NOPE:], q_pos)
    ckv = rmsnorm(ckv, kv_nw)
    kr = rope(kr, q_pos)
    if past_ckv is None:
        ckv_all, kr_all, k_pos = ckv, kr, q_pos
    else:
        ckv_all = jnp.concatenate([past_ckv.astype(ckv.dtype), ckv], axis=1)
        kr_all = jnp.concatenate([past_kr.astype(kr.dtype), kr], axis=1)
        k_pos = jnp.arange(ckv_all.shape[1], dtype=jnp.int32)
    kv = (ckv_all @ w_ukv).reshape(B, ckv_all.shape[1], MLA_H, MLA_NOPE + MLA_V)
    o = mla_attend(q_nope, q_rope, kv[..., :MLA_NOPE], kr_all, kv[..., MLA_NOPE:], q_pos, k_pos)
    return o.reshape(B, L, MLA_H * MLA_V) @ w_out, ckv, kr


def hier_moe(h, w_group, b_group, w_expert, b_expert, w1, w3, w2):
    T, D = h.shape
    tok = jnp.arange(T, dtype=jnp.int32)
    gl = (h @ w_group).astype(jnp.float32) + b_group.astype(jnp.float32)
    g_sel = jnp.argmax(gl, axis=-1).astype(jnp.int32)
    p_g = jax.nn.softmax(gl, axis=-1)[tok, g_sel]
    el = ((h @ w_expert).astype(jnp.float32) + b_expert.astype(jnp.float32)).reshape(T, N_GROUPS, EXPERTS_PER_GROUP)
    el = el[tok, g_sel]
    top_v, top_i = lax.top_k(el, TOP_K)
    w_tok = p_g[:, None] * jax.nn.softmax(top_v, axis=-1)
    e_id = g_sel[:, None] * EXPERTS_PER_GROUP + top_i.astype(jnp.int32)
    A = T * TOP_K
    e_flat = e_id.reshape(A)
    t_flat = jnp.repeat(tok, TOP_K)
    w_flat = w_tok.reshape(A)
    order = jnp.argsort(e_flat)
    e_sorted = e_flat[order]
    counts = jnp.bincount(e_flat, length=N_EXPERTS)
    padded = (counts + MOE_BLOCK - 1) // MOE_BLOCK * MOE_BLOCK
    start = jnp.cumsum(counts) - counts
    pend = jnp.cumsum(padded)
    pstart = pend - padded
    dest = pstart[e_sorted] + jnp.arange(A, dtype=jnp.int32) - start[e_sorted]
    nb = -(-(A + N_EXPERTS * (MOE_BLOCK - 1)) // MOE_BLOCK)
    P = nb * MOE_BLOCK
    row_tok = jnp.zeros((P,), jnp.int32).at[dest].set(t_flat[order])
    row_w = jnp.zeros((P,), h.dtype).at[dest].set(w_flat[order].astype(h.dtype))
    blk_e = jnp.clip(jnp.searchsorted(pend, jnp.arange(nb, dtype=jnp.int32) * MOE_BLOCK, side='right'),
                     0, N_EXPERTS - 1)
    xg = h[row_tok].reshape(nb, MOE_BLOCK, D)

    def expert_block(args):
        xb, e = args
        return (jax.nn.silu(xb @ w1[e]) * (xb @ w3[e])) @ w2[e]

    yb = lax.map(expert_block, (xg, blk_e)).reshape(P, D)
    return jnp.zeros((T, D), h.dtype).at[row_tok].add(yb * row_w[:, None])


def trunk(x, c, pos0, gla_s, gdn_s, conv_s, past_ckv, past_kr, p):
    B, L, D = x.shape
    q_pos = pos0 + jnp.arange(L, dtype=jnp.int32)
    new_gla, new_gdn, new_conv, new_ckv, new_kr = [], [], [], [], []
    for l in range(DEPTH):
        i = l // 2
        shift, scale, gate = ada_modulation(c, p['ada_w'][l, 0], p['ada_b'][l, 0])
        h = rmsnorm(x, p['norm_w'][l, 0]) * (1 + scale) + shift
        if l % 2 == 0:
            out, s1, s2, cv = ab_mixer(h, gla_s[i], gdn_s[i], conv_s[i], p['ab_w_in'][i], p['gla_w_alpha'][i],
                                       p['gla_b_alpha'][i], p['gla_norm_w'][i], p['gdn_conv_w'][i],
                                       p['gdn_a_log'][i], p['gdn_dt_bias'][i], p['gdn_norm_w'][i], p['ab_w_out'][i])
            new_gla.append(s1)
            new_gdn.append(s2)
            new_conv.append(cv)
        else:
            out, ckv, kr = mla_mixer(h, q_pos, None if past_ckv is None else past_ckv[i],
                                     None if past_kr is None else past_kr[i], p['mla_w_in'][i],
                                     p['mla_q_norm_w'][i], p['mla_w_uq'][i], p['mla_kv_norm_w'][i],
                                     p['mla_w_ukv'][i], p['mla_w_out'][i])
            new_ckv.append(ckv)
            new_kr.append(kr)
        x = x + gate * out
        shift, scale, gate = ada_modulation(c, p['ada_w'][l, 1], p['ada_b'][l, 1])
        h = rmsnorm(x, p['norm_w'][l, 1]) * (1 + scale) + shift
        y = hier_moe(h.reshape(B * L, D), p['moe_w_group'][l], p['moe_b_group'][l], p['moe_w_expert'][l],
                     p['moe_b_expert'][l], p['moe_w1'][l], p['moe_w3'][l], p['moe_w2'][l])
        x = x + gate * y.reshape(B, L, D)
    y = rmsnorm(x, p['final_norm_w'])
    return (y, jnp.stack(new_gla), jnp.stack(new_gdn), jnp.stack(new_conv), jnp.stack(new_ckv), jnp.stack(new_kr))


def setup_inputs(seed: int = 0) -> dict:
    key = jax.random.key(seed)
    ks = iter(jax.random.split(key, 40))

    def nrm(shape, scale):
        return jax.random.normal(next(ks), shape, jnp.float32) * scale

    def gain(shape):
        return 1.0 + nrm(shape, 0.05)

    D = D_MODEL
    dt = jnp.exp(jax.random.uniform(next(ks), (N_AB, GDN_H), jnp.float32, math.log(1e-3), math.log(1e-1)))
    return {
        'x_prompt': nrm((BATCH, SEQ, D), 1.0),
        'x_sample': nrm((DEC_BATCH, DEC_SEQ, D), 1.0),
        'state_gla': nrm((N_AB, DEC_BATCH, GLA_H, GLA_DK, GLA_DV), 0.1),
        'state_gdn': nrm((N_AB, DEC_BATCH, GDN_H, GDN_DK, GDN_DV), 0.1),
        'state_gdn_conv': nrm((N_AB, DEC_BATCH, CONV_W - 1, GDN_CONV_DIM), 1.0),
        'cache_mla_ckv': nrm((N_C, DEC_BATCH, PAST_LEN, MLA_KV_RANK), 1.0),
        'cache_mla_krope': nrm((N_C, DEC_BATCH, PAST_LEN, MLA_ROPE), 1.0),
        'c_prompt': nrm((BATCH, D), 1.0),
        'c_sample': nrm((DEC_BATCH, D), 1.0),
        'ada_w': nrm((DEPTH, 2, D, 3 * D), 0.5 * D ** -0.5),
        'ada_b': nrm((DEPTH, 2, 3 * D), 0.02),
        'norm_w': gain((DEPTH, 2, D)),
        'final_norm_w': gain((D,)),
        'ab_w_in': nrm((N_AB, D, AB_IN), D ** -0.5),
        'gla_w_alpha': nrm((N_AB, GLA_LR, GLA_H * GLA_DK), GLA_LR ** -0.5),
        'gla_b_alpha': nrm((N_AB, GLA_H * GLA_DK), 0.1),
        'gla_norm_w': gain((N_AB, GLA_DV)),
        'gdn_conv_w': nrm((N_AB, CONV_W, GDN_CONV_DIM), CONV_W ** -0.5),
        'gdn_a_log': jnp.log(jax.random.uniform(next(ks), (N_AB, GDN_H), jnp.float32, 1.0, 16.0)),
        'gdn_dt_bias': dt + jnp.log(-jnp.expm1(-dt)),
        'gdn_norm_w': gain((N_AB, GDN_DV)),
        'ab_w_out': nrm((N_AB, AB_OUT, D), AB_OUT ** -0.5),
        'mla_w_in': nrm((N_C, D, MLA_IN), D ** -0.5),
        'mla_q_norm_w': gain((N_C, MLA_Q_RANK)),
        'mla_w_uq': nrm((N_C, MLA_Q_RANK, MLA_H * (MLA_NOPE + MLA_ROPE)), MLA_Q_RANK ** -0.5),
        'mla_kv_norm_w': gain((N_C, MLA_KV_RANK)),
        'mla_w_ukv': nrm((N_C, MLA_KV_RANK, MLA_H * (MLA_NOPE + MLA_V)), MLA_KV_RANK ** -0.5),
        'mla_w_out': nrm((N_C, MLA_H * MLA_V, D), (MLA_H * MLA_V) ** -0.5),
        'moe_w_group': nrm((DEPTH, D, N_GROUPS), D ** -0.5),
        'moe_b_group': nrm((DEPTH, N_GROUPS), 0.01),
        'moe_w_expert': nrm((DEPTH, D, N_EXPERTS), D ** -0.5),
        'moe_b_expert': nrm((DEPTH, N_EXPERTS), 0.01),
        'moe_w1': nrm((DEPTH, N_EXPERTS, D, D_EXPERT), D ** -0.5),
        'moe_w3': nrm((DEPTH, N_EXPERTS, D, D_EXPERT), D ** -0.5),
        'moe_w2': nrm((DEPTH, N_EXPERTS, D_EXPERT, D), D_EXPERT ** -0.5),
    }


def reference(x_prompt, x_sample, state_gla, state_gdn, state_gdn_conv, cache_mla_ckv, cache_mla_krope,
              c_prompt, c_sample, ada_w, ada_b, norm_w, final_norm_w, ab_w_in, gla_w_alpha, gla_b_alpha,
              gla_norm_w, gdn_conv_w, gdn_a_log, gdn_dt_bias, gdn_norm_w, ab_w_out, mla_w_in, mla_q_norm_w,
              mla_w_uq, mla_kv_norm_w, mla_w_ukv, mla_w_out, moe_w_group, moe_b_group, moe_w_expert,
              moe_b_expert, moe_w1, moe_w3, moe_w2):
    p = dict(ada_w=ada_w, ada_b=ada_b, norm_w=norm_w, final_norm_w=final_norm_w, ab_w_in=ab_w_in,
             gla_w_alpha=gla_w_alpha, gla_b_alpha=gla_b_alpha, gla_norm_w=gla_norm_w, gdn_conv_w=gdn_conv_w,
             gdn_a_log=gdn_a_log, gdn_dt_bias=gdn_dt_bias, gdn_norm_w=gdn_norm_w, ab_w_out=ab_w_out,
             mla_w_in=mla_w_in, mla_q_norm_w=mla_q_norm_w, mla_w_uq=mla_w_uq, mla_kv_norm_w=mla_kv_norm_w,
             mla_w_ukv=mla_w_ukv, mla_w_out=mla_w_out, moe_w_group=moe_w_group, moe_b_group=moe_b_group,
             moe_w_expert=moe_w_expert, moe_b_expert=moe_b_expert, moe_w1=moe_w1, moe_w3=moe_w3, moe_w2=moe_w2)
    bp = x_prompt.shape[0]
    y_prompt, gla_p, gdn_p, conv_p, ckv_p, kr_p = trunk(
        x_prompt, c_prompt, 0,
        jnp.zeros((N_AB, bp, GLA_H, GLA_DK, GLA_DV), jnp.float32),
        jnp.zeros((N_AB, bp, GDN_H, GDN_DK, GDN_DV), jnp.float32),
        jnp.zeros((N_AB, bp, CONV_W - 1, GDN_CONV_DIM), x_prompt.dtype),
        None, None, p)
    y_sample, gla_s, gdn_s, conv_s, ckv_s, kr_s = trunk(
        x_sample, c_sample, cache_mla_ckv.shape[2], state_gla, state_gdn, state_gdn_conv,
        cache_mla_ckv, cache_mla_krope, p)
    return (y_prompt, y_sample, gla_p, gdn_p, conv_p, ckv_p, kr_p, gla_s, gdn_s, conv_s, ckv_s, kr_s)
```

```python
import functools
import math

import jax
import jax.numpy as jnp
from jax import lax
from jax.experimental import pallas as pl
from jax.experimental.pallas import tpu as pltpu

F32 = jnp.float32
BF16 = jnp.bfloat16
I32 = jnp.int32
HI = lax.Precision.HIGHEST

D_MODEL = 1024
CHUNK = 64
GLA_H, GLA_DK, GLA_DV, GLA_LR, GLA_TAU = 4, 64, 128, 16, 16.0
GDN_H, GDN_DK, GDN_DV, CONV_W = 4, 128, 128, 4
GDN_CONV_DIM = GDN_H * (2 * GDN_DK + GDN_DV)
AB_IN_WIDTHS = (GLA_H * GLA_DK, GLA_H * GLA_DK, GLA_H * GLA_DV, GLA_LR, GLA_H * GLA_DV,
                GDN_CONV_DIM, GDN_H, GDN_H, GDN_H * GDN_DV)
AB_MAIN = 3584
MLA_H, MLA_NOPE, MLA_ROPE, MLA_V = 16, 64, 32, 64
MLA_Q_RANK, MLA_KV_RANK = 384, 256
MLA_IN_R = MLA_Q_RANK + MLA_KV_RANK + 256
ROPE_THETA = 10000.0
N_GROUPS, EXPERTS_PER_GROUP, N_EXPERTS, D_EXPERT = 4, 8, 32, 512
GROUP_SHIFT = int(math.log2(EXPERTS_PER_GROUP))
CHUNK_SHIFT = int(math.log2(CHUNK))
EPS = 1e-6

LANES = 128
GLA_SUB = 16
MOE_ROWS = 256
NEG = -3.0e38
VMEM_LIMIT = 56 * 1024 * 1024

LANE_GA = GLA_LR
LANE_GB = GLA_LR + GDN_H

NT = (((1,), (1,)), ((), ()))
TN = (((0,), (0,)), ((), ()))


def _cp(sem, vmem=None):
    return pltpu.CompilerParams(dimension_semantics=sem, vmem_limit_bytes=vmem or VMEM_LIMIT)


def _sds(shape, dtype):
    return jax.ShapeDtypeStruct(shape, dtype)


def _sigmoid(x):
    return 1.0 / (1.0 + jnp.exp(-x))


def _softplus(x):
    return jnp.maximum(x, 0.0) + jnp.log(1.0 + jnp.exp(-jnp.abs(x)))


def _rms(x):
    return x * lax.rsqrt(jnp.mean(x * x, axis=-1, keepdims=True) + EPS)


def _norm_mod(x, nw, shift, scale):
    return (_rms(x) * nw) * (1.0 + scale) + shift


def _dot_bf(a, b):
    return jnp.dot(a.astype(BF16), b.astype(BF16), preferred_element_type=F32)


def _dot_hi(a, b):
    return jnp.dot(a, b, precision=HI, preferred_element_type=F32)


def _ada_kernel(c_ref, w_ref, b_ref, o_ref):
    c = c_ref[...]
    o_ref[0] = _dot_hi(c * _sigmoid(c), w_ref[0]) + b_ref[0]


def _ada_call(c_all, ada_w, ada_b):
    n = ada_w.shape[0]
    nb = c_all.shape[0]
    return pl.pallas_call(
        _ada_kernel, grid=(n, 3),
        in_specs=[pl.BlockSpec((nb, D_MODEL), lambda i, j: (0, 0)),
                  pl.BlockSpec((1, D_MODEL, D_MODEL), lambda i, j: (i, 0, j)),
                  pl.BlockSpec((1, 1, D_MODEL), lambda i, j: (i, 0, j))],
        out_specs=pl.BlockSpec((1, nb, D_MODEL), lambda i, j: (i, 0, j)),
        out_shape=_sds((n, nb, 3 * D_MODEL), F32),
        compiler_params=_cp(("arbitrary", "arbitrary")), name="ada",
    )(c_all, ada_w, ada_b)


def _in0_kernel(x_ref, mod_ref, nw_ref, wm_ref, ws_ref, zm_ref, zs_ref):
    h = _norm_mod(x_ref[...], nw_ref[...], mod_ref[0, 0], mod_ref[0, 1])
    zm_ref[...] = _dot_bf(h, wm_ref[...]).astype(BF16)
    zs_ref[...] = _dot_hi(h, ws_ref[...])


def _in0_call(x, mod, nw, w_main, w_small, L, tm):
    T = x.shape[0]
    per_b = L // tm
    return pl.pallas_call(
        _in0_kernel, grid=(T // tm,),
        in_specs=[pl.BlockSpec((tm, D_MODEL), lambda i: (i, 0)),
                  pl.BlockSpec((1, 3, 1, D_MODEL), lambda i: (i // per_b, 0, 0, 0)),
                  pl.BlockSpec((1, D_MODEL), lambda i: (0, 0)),
                  pl.BlockSpec((D_MODEL, AB_MAIN), lambda i: (0, 0)),
                  pl.BlockSpec((D_MODEL, LANES), lambda i: (0, 0))],
        out_specs=[pl.BlockSpec((tm, AB_MAIN), lambda i: (i, 0)),
                   pl.BlockSpec((tm, LANES), lambda i: (i, 0))],
        out_shape=[_sds((T, AB_MAIN), BF16), _sds((T, LANES), F32)],
        compiler_params=_cp(("arbitrary",)), name="in0",
    )(x, mod, nw, w_main, w_small)


def _gated_head_out(o, nw, gate):
    gate = gate.astype(F32)
    return (_rms(o) * nw * (gate * _sigmoid(gate))).astype(BF16)


def _gla_kernel(q_ref, k_ref, v_ref, gr_ref, zs_ref, wa_ref, ba_ref, nw_ref, s0_ref,
                o_ref, st_ref, st_sc, *, CL, nc):
    c = pl.program_id(1)

    @pl.when(c == 0)
    def _():
        st_sc[...] = s0_ref[0]

    pre = _dot_hi(zs_ref[...], wa_ref[...]) + ba_ref[...]
    la = -_softplus(-pre) * (1.0 / GLA_TAU)
    r = lax.broadcasted_iota(I32, (CL, CL), 0)
    cc = lax.broadcasted_iota(I32, (CL, CL), 1)
    b_all = _dot_hi((r >= cc).astype(F32), la)
    row = lax.broadcasted_iota(I32, (GLA_SUB, 1), 0)
    nw = nw_ref[...]
    for h in range(GLA_H):
        q = q_ref[:, h * GLA_DK:(h + 1) * GLA_DK].astype(F32) * GLA_DK ** -0.5
        k = k_ref[:, h * GLA_DK:(h + 1) * GLA_DK].astype(F32)
        v = v_ref[:, h * GLA_DV:(h + 1) * GLA_DV].astype(F32)
        b = b_all[:, h * GLA_DK:(h + 1) * GLA_DK]
        st = st_sc[h]
        bprev = jnp.zeros((1, GLA_DK), F32)
        outs = []
        for blk in range(CL // GLA_SUB):
            sl = slice(blk * GLA_SUB, (blk + 1) * GLA_SUB)
            qi, ki, vi = q[sl], k[sl], v[sl]
            brel = b[sl] - bprev
            bend = brel[GLA_SUB - 1:GLA_SUB]
            oi = lax.dot_general((qi * jnp.exp(brel)).astype(BF16), st.astype(BF16), NT,
                                 preferred_element_type=F32)
            for j in range(GLA_SUB):
                d = brel - brel[j:j + 1]
                e = jnp.where(row >= j, jnp.exp(jnp.minimum(d, 0.0)), 0.0)
                a = jnp.sum(qi * ki[j:j + 1] * e, axis=-1, keepdims=True)
                oi = oi + a * vi[j:j + 1]
            outs.append(oi)
            khat = ki * jnp.exp(bend - brel)
            st = st * jnp.exp(bend) + lax.dot_general(vi.astype(BF16), khat.astype(BF16), TN,
                                                      preferred_element_type=F32)
            bprev = b[(blk + 1) * GLA_SUB - 1:(blk + 1) * GLA_SUB]
        st_sc[h] = st
        o = jnp.concatenate(outs, axis=0)
        o_ref[:, h * GLA_DV:(h + 1) * GLA_DV] = _gated_head_out(
            o, nw, gr_ref[:, h * GLA_DV:(h + 1) * GLA_DV])

    @pl.when(c == nc - 1)
    def _():
        st_ref[0] = st_sc[...]


def _gla_call(zm, zs, wa_pad, b_alpha, nw, s0t, B, L):
    CL = min(CHUNK, L)
    nc = L // CL
    T = B * L
    qk_w = GLA_H * GLA_DK
    v_w = GLA_H * GLA_DV
    row = lambda b, c: b * nc + c
    return pl.pallas_call(
        functools.partial(_gla_kernel, CL=CL, nc=nc), grid=(B, nc),
        in_specs=[pl.BlockSpec((CL, qk_w), lambda b, c: (row(b, c), 0)),
                  pl.BlockSpec((CL, qk_w), lambda b, c: (row(b, c), 1)),
                  pl.BlockSpec((CL, v_w), lambda b, c: (row(b, c), 1)),
                  pl.BlockSpec((CL, v_w), lambda b, c: (row(b, c), 2)),
                  pl.BlockSpec((CL, LANES), lambda b, c: (row(b, c), 0)),
                  pl.BlockSpec((LANES, qk_w), lambda b, c: (0, 0)),
                  pl.BlockSpec((1, qk_w), lambda b, c: (0, 0)),
                  pl.BlockSpec((1, GLA_DV), lambda b, c: (0, 0)),
                  pl.BlockSpec((1, GLA_H, GLA_DV, GLA_DK), lambda b, c: (b, 0, 0, 0))],
        out_specs=[pl.BlockSpec((CL, v_w), lambda b, c: (row(b, c), 0)),
                   pl.BlockSpec((1, GLA_H, GLA_DV, GLA_DK), lambda b, c: (b, 0, 0, 0))],
        out_shape=[_sds((T, v_w), BF16), _sds((B, GLA_H, GLA_DV, GLA_DK), F32)],
        scratch_shapes=[pltpu.VMEM((GLA_H, GLA_DV, GLA_DK), F32)],
        compiler_params=_cp(("arbitrary", "arbitrary")), name="gla",
    )(zm, zm, zm, zm, zs, wa_pad, b_alpha, nw, s0t)


def _gdn_kernel(qkv_ref, zs_ref, gg_ref, cw_ref, alog_ref, dtb_ref, nw_ref, hist_ref, s0_ref,
                o_ref, s_ref, cv_sc, st_sc, *, CL, nc):
    c = pl.program_id(1)

    @pl.when(c == 0)
    def _():
        cv_sc[0:8, :] = hist_ref[0]
        st_sc[...] = s0_ref[0]

    x = qkv_ref[...].astype(F32)
    cv_sc[8:8 + CL, :] = x
    cw = cw_ref[...]
    conv = (cv_sc[5:5 + CL, :] * cw[0:1] + cv_sc[6:6 + CL, :] * cw[1:2]
            + cv_sc[7:7 + CL, :] * cw[2:3] + x * cw[3:4])
    cv_sc[0:8, :] = x[CL - 8:CL]
    conv = conv * _sigmoid(conv)

    zs = zs_ref[...]
    g_all = -jnp.exp(alog_ref[...]) * _softplus(zs + dtb_ref[...])
    beta_all = _sigmoid(zs)
    r = lax.broadcasted_iota(I32, (CL, CL), 0)
    cc = lax.broadcasted_iota(I32, (CL, CL), 1)
    gam_all = _dot_hi((r >= cc).astype(F32), g_all)
    gam_t = gam_all.T
    eye = (r == cc).astype(F32)
    nw = nw_ref[...]
    qk_w = GDN_H * GDN_DK
    for h in range(GDN_H):
        q = conv[:, h * GDN_DK:(h + 1) * GDN_DK]
        k = conv[:, qk_w + h * GDN_DK:qk_w + (h + 1) * GDN_DK]
        v = conv[:, 2 * qk_w + h * GDN_DV:2 * qk_w + (h + 1) * GDN_DV]
        q = q * lax.rsqrt(jnp.sum(q * q, axis=-1, keepdims=True) + EPS) * GDN_DK ** -0.5
        k = k * lax.rsqrt(jnp.sum(k * k, axis=-1, keepdims=True) + EPS)
        gcol = gam_all[:, LANE_GA + h:LANE_GA + h + 1]
        grow = gam_t[LANE_GA + h:LANE_GA + h + 1, :]
        bcol = beta_all[:, LANE_GB + h:LANE_GB + h + 1]
        dec = jnp.where(r >= cc, jnp.exp(jnp.minimum(gcol - grow, 0.0)), 0.0)
        kb = k.astype(BF16)
        kk = lax.dot_general(kb, kb, NT, preferred_element_type=F32)
        a_mat = jnp.where(r > cc, bcol * kk * dec, 0.0)
        t_inv = eye - a_mat
        pw = a_mat
        for _ in range(int(math.log2(CL)) - 1):
            pw = _dot_hi(pw, pw)
            t_inv = t_inv + _dot_hi(t_inv, pw)
        eg = jnp.exp(gcol)
        u = _dot_bf(t_inv, v * bcol)
        w = _dot_bf(t_inv, k * (bcol * eg))
        st = st_sc[h]
        stb = st.astype(BF16)
        delta = u - _dot_bf(w, stb)
        qk = lax.dot_general(q.astype(BF16), kb, NT, preferred_element_type=F32) * dec
        o = _dot_bf(q * eg, stb) + _dot_bf(qk, delta)
        glast = gcol[CL - 1:CL]
        kdec = k * jnp.exp(glast - gcol)
        st_sc[h] = st * jnp.exp(glast) + lax.dot_general(
            kdec.astype(BF16), delta.astype(BF16), TN, preferred_element_type=F32)
        o_ref[:, h * GDN_DV:(h + 1) * GDN_DV] = _gated_head_out(
            o, nw, gg_ref[:, h * GDN_DV:(h + 1) * GDN_DV])

    @pl.when(c == nc - 1)
    def _():
        s_ref[0] = st_sc[...]


def _gdn_call(zm, zs, conv_w, alog_v, dtb_v, nw, hist8, s0, B, L):
    CL = min(CHUNK, L)
    nc = L // CL
    T = B * L
    v_w = GDN_H * GDN_DV
    row = lambda b, c: b * nc + c
    return pl.pallas_call(
        functools.partial(_gdn_kernel, CL=CL, nc=nc), grid=(B, nc),
        in_specs=[pl.BlockSpec((CL, GDN_CONV_DIM), lambda b, c: (row(b, c), 1)),
                  pl.BlockSpec((CL, LANES), lambda b, c: (row(b, c), 0)),
                  pl.BlockSpec((CL, v_w), lambda b, c: (row(b, c), 6)),
                  pl.BlockSpec((CONV_W, GDN_CONV_DIM), lambda b, c: (0, 0)),
                  pl.BlockSpec((1, LANES), lambda b, c: (0, 0)),
                  pl.BlockSpec((1, LANES), lambda b, c: (0, 0)),
                  pl.BlockSpec((1, GDN_DV), lambda b, c: (0, 0)),
                  pl.BlockSpec((1, 8, GDN_CONV_DIM), lambda b, c: (b, 0, 0)),
                  pl.BlockSpec((1, GDN_H, GDN_DK, GDN_DV), lambda b, c: (b, 0, 0, 0))],
        out_specs=[pl.BlockSpec((CL, v_w), lambda b, c: (row(b, c), 0)),
                   pl.BlockSpec((1, GDN_H, GDN_DK, GDN_DV), lambda b, c: (b, 0, 0, 0))],
        out_shape=[_sds((T, v_w), BF16), _sds((B, GDN_H, GDN_DK, GDN_DV), F32)],
        scratch_shapes=[pltpu.VMEM((8 + CL, GDN_CONV_DIM), F32),
                        pltpu.VMEM((GDN_H, GDN_DK, GDN_DV), F32)],
        compiler_params=_cp(("arbitrary", "arbitrary")), name="gdn",
    )(zm, zs, zm, conv_w, alog_v, dtb_v, nw, hist8, s0)


def _post_kernel(*refs, n_in):
    a_refs = refs[:n_in]
    w_refs = refs[n_in:2 * n_in]
    x_ref, mod1_ref, mod2_ref, nw2_ref, wr_ref, br_ref, xn_ref, h2_ref, rt_ref = refs[2 * n_in:]
    acc = jnp.dot(a_refs[0][...], w_refs[0][...], preferred_element_type=F32)
    for a_ref, w_ref in zip(a_refs[1:], w_refs[1:]):
        acc = acc + jnp.dot(a_ref[...], w_ref[...], preferred_element_type=F32)
    xn = x_ref[...] + mod1_ref[0, 2] * acc
    xn_ref[...] = xn
    h2 = _norm_mod(xn, nw2_ref[...], mod2_ref[0, 0], mod2_ref[0, 1])
    h2_ref[...] = h2
    logits = _dot_hi(h2, wr_ref[...]) + br_ref[...]
    lane = lax.broadcasted_iota(I32, logits.shape, 1)
    is_g = lane < N_GROUPS
    gl = jnp.where(is_g, logits, NEG)
    gmax = jnp.max(gl, axis=-1, keepdims=True)
    g_sel = jnp.min(jnp.where(gl == gmax, lane, LANES), axis=-1, keepdims=True)
    p_g = 1.0 / jnp.sum(jnp.where(is_g, jnp.exp(logits - gmax), 0.0), axis=-1, keepdims=True)
    in_grp = jnp.logical_and(
        jnp.logical_and(lane >= N_GROUPS, lane < N_GROUPS + N_EXPERTS),
        jnp.right_shift(lane - N_GROUPS, GROUP_SHIFT) == g_sel)
    el = jnp.where(in_grp, logits, NEG)
    v1 = jnp.max(el, axis=-1, keepdims=True)
    i1 = jnp.min(jnp.where(jnp.logical_and(in_grp, el == v1), lane, LANES), axis=-1, keepdims=True)
    rest = jnp.logical_and(in_grp, lane != i1)
    el2 = jnp.where(rest, logits, NEG)
    v2 = jnp.max(el2, axis=-1, keepdims=True)
    i2 = jnp.min(jnp.where(jnp.logical_and(rest, el2 == v2), lane, LANES), axis=-1, keepdims=True)
    ex = jnp.exp(v2 - v1)
    w1 = 1.0 / (1.0 + ex)
    w2 = ex * w1
    e1 = (i1 - N_GROUPS).astype(F32)
    e2 = (i2 - N_GROUPS).astype(F32)
    rt_ref[...] = jnp.where(lane == 0, e1, jnp.where(lane == 1, e2, jnp.where(
        lane == 2, p_g * w1, jnp.where(lane == 3, p_g * w2, 0.0))))


def _post_call(a_list, w_list, x, mod1, mod2, nw2, w_route, b_route, L, tm):
    T = x.shape[0]
    per_b = L // tm
    n_in = len(a_list)
    tok = lambda i: (i, 0)
    const = lambda i: (0, 0)
    modmap = lambda i: (i // per_b, 0, 0, 0)
    in_specs = ([pl.BlockSpec((tm, a.shape[1]), tok) for a in a_list]
                + [pl.BlockSpec(w.shape, const) for w in w_list]
                + [pl.BlockSpec((tm, D_MODEL), tok),
                   pl.BlockSpec((1, 3, 1, D_MODEL), modmap),
                   pl.BlockSpec((1, 3, 1, D_MODEL), modmap),
                   pl.BlockSpec((1, D_MODEL), const),
                   pl.BlockSpec((D_MODEL, LANES), const),
                   pl.BlockSpec((1, LANES), const)])
    return pl.pallas_call(
        functools.partial(_post_kernel, n_in=n_in), grid=(T // tm,),
        in_specs=in_specs,
        out_specs=[pl.BlockSpec((tm, D_MODEL), tok), pl.BlockSpec((tm, D_MODEL), tok),
                   pl.BlockSpec((tm, LANES), tok)],
        out_shape=[_sds((T, D_MODEL), F32), _sds((T, D_MODEL), F32), _sds((T, LANES), F32)],
        compiler_params=_cp(("arbitrary",)), name="post",
    )(*a_list, *w_list, x, mod1, mod2, nw2, w_route, b_route)


def _rank_kernel(rt_ref, rank_ref, cnt_ref, run_sc):
    @pl.when(pl.program_id(0) == 0)
    def _():
        run_sc[...] = jnp.zeros_like(run_sc)

    rt = rt_ref[...]
    tr = rt.shape[0]
    lane = lax.broadcasted_iota(I32, rt.shape, 1)
    o1 = lane == rt[:, 0:1].astype(I32)
    o2 = lane == rt[:, 1:2].astype(I32)
    onehot = jnp.where(o1, 1.0, 0.0) + jnp.where(o2, 1.0, 0.0)
    r = lax.broadcasted_iota(I32, (tr, tr), 0)
    cc = lax.broadcasted_iota(I32, (tr, tr), 1)
    before = jnp.dot(jnp.where(r > cc, 1.0, 0.0).astype(BF16), onehot.astype(BF16),
                     preferred_element_type=F32) + run_sc[...]
    rank1 = jnp.sum(jnp.where(o1, before, 0.0), axis=-1, keepdims=True)
    rank2 = jnp.sum(jnp.where(o2, before, 0.0), axis=-1, keepdims=True)
    rank_ref[...] = jnp.where(lane == 0, rank1, jnp.where(lane == 1, rank2, 0.0))
    run_sc[...] = run_sc[...] + jnp.sum(onehot, axis=0, keepdims=True)
    cnt_ref[...] = run_sc[...]


def _rank_call(route, tr):
    T = route.shape[0]
    return pl.pallas_call(
        _rank_kernel, grid=(T // tr,),
        in_specs=[pl.BlockSpec((tr, LANES), lambda i: (i, 0))],
        out_specs=[pl.BlockSpec((tr, LANES), lambda i: (i, 0)),
                   pl.BlockSpec((1, LANES), lambda i: (0, 0))],
        out_shape=[_sds((T, LANES), F32), _sds((1, LANES), F32)],
        scratch_shapes=[pltpu.VMEM((1, LANES), F32)],
        compiler_params=_cp(("arbitrary",)), name="moe_rank",
    )(route)


def _dest_kernel(rt_ref, rank_ref, ps_ref, d_ref):
    rt = rt_ref[...]
    rank = rank_ref[...]
    ps = ps_ref[...]
    lane = lax.broadcasted_iota(I32, rt.shape, 1)
    d1 = jnp.sum(jnp.where(lane == rt[:, 0:1].astype(I32), ps, 0.0), axis=-1, keepdims=True) + rank[:, 0:1]
    d2 = jnp.sum(jnp.where(lane == rt[:, 1:2].astype(I32), ps, 0.0), axis=-1, keepdims=True) + rank[:, 1:2]
    dd = jnp.where(lane == 0, d1, jnp.where(lane == 1, d2, 0.0))
    d_ref[0] = dd.T[0:8, :].astype(I32)


def _dest_call(route, rank, pstart, tr):
    T = route.shape[0]
    return pl.pallas_call(
        _dest_kernel, grid=(T // tr,),
        in_specs=[pl.BlockSpec((tr, LANES), lambda i: (i, 0)),
                  pl.BlockSpec((tr, LANES), lambda i: (i, 0)),
                  pl.BlockSpec((1, LANES), lambda i: (0, 0))],
        out_specs=pl.BlockSpec((1, 8, tr), lambda i: (i, 0, 0)),
        out_shape=_sds((T // tr, 8, tr), I32),
        compiler_params=_cp(("arbitrary",)), name="moe_dest",
    )(route, rank, pstart)


def _row_copy(src, s, dst, d, sem):
    return pltpu.make_async_copy(src.at[pl.ds(s, 1), :], dst.at[pl.ds(d, 1), :], sem)


def _disp_kernel(h_ref, dest_hbm, xg_in, xg_hbm, d_sm, sem_i, sem_o, *, tm):
    del xg_in
    i = pl.program_id(0)
    idx_cp = pltpu.make_async_copy(dest_hbm.at[i], d_sm, sem_i)
    idx_cp.start()
    idx_cp.wait()

    def issue(t, carry):
        for kk in range(2):
            _row_copy(h_ref, t, xg_hbm, d_sm[kk, t], sem_o).start()
        return carry

    lax.fori_loop(0, tm, issue, 0)

    def drain(t, carry):
        for kk in range(2):
            _row_copy(h_ref, t, xg_hbm, d_sm[kk, t], sem_o).wait()
        return carry

    lax.fori_loop(0, tm, drain, 0)


def _disp_call(h2, dest, xg0, tm):
    T = h2.shape[0]
    return pl.pallas_call(
        functools.partial(_disp_kernel, tm=tm), grid=(T // tm,),
        in_specs=[pl.BlockSpec((tm, D_MODEL), lambda i: (i, 0)),
                  pl.BlockSpec(memory_space=pl.ANY),
                  pl.BlockSpec(memory_space=pl.ANY)],
        out_specs=pl.BlockSpec(memory_space=pl.ANY),
        out_shape=_sds(xg0.shape, F32),
        scratch_shapes=[pltpu.SMEM((8, tm), I32), pltpu.SemaphoreType.DMA, pltpu.SemaphoreType.DMA],
        input_output_aliases={2: 0},
        compiler_params=_cp(("arbitrary",)), name="moe_dispatch",
    )(h2, dest, xg0)


def _expert_kernel(be_ref, nu_ref, x_ref, w1_ref, w3_ref, w2_ref, y_ref, w1b, w3b, w2b):
    i = pl.program_id(0)
    used = i < nu_ref[0]
    fresh = jnp.logical_or(i == 0, be_ref[i] != be_ref[jnp.maximum(i - 1, 0)])

    @pl.when(jnp.logical_and(used, fresh))
    def _():
        w1b[...] = w1_ref[0, 0].astype(BF16)
        w3b[...] = w3_ref[0, 0].astype(BF16)
        w2b[...] = w2_ref[0, 0].astype(BF16)

    @pl.when(used)
    def _():
        x = x_ref[...].astype(BF16)
        a = jnp.dot(x, w1b[...], preferred_element_type=F32)
        g = jnp.dot(x, w3b[...], preferred_element_type=F32)
        hm = (a * _sigmoid(a) * g).astype(BF16)
        y_ref[...] = jnp.dot(hm, w2b[...], preferred_element_type=F32)


def _expert_call(blk_e, n_used, xg, w1, w3, w2, layer):
    P = xg.shape[0]
    nb = P // MOE_ROWS
    rowmap = lambda i, be, nu: (jnp.minimum(i, nu[0] - 1), 0)
    wmap = lambda i, be, nu: (layer, be[i], 0, 0)
    return pl.pallas_call(
        _expert_kernel,
        grid_spec=pltpu.PrefetchScalarGridSpec(
            num_scalar_prefetch=2, grid=(nb,),
            in_specs=[pl.BlockSpec((MOE_ROWS, D_MODEL), rowmap),
                      pl.BlockSpec((1, 1, D_MODEL, D_EXPERT), wmap),
                      pl.BlockSpec((1, 1, D_MODEL, D_EXPERT), wmap),
                      pl.BlockSpec((1, 1, D_EXPERT, D_MODEL), wmap)],
            out_specs=pl.BlockSpec((MOE_ROWS, D_MODEL), rowmap),
            scratch_shapes=[pltpu.VMEM((D_MODEL, D_EXPERT), BF16),
                            pltpu.VMEM((D_MODEL, D_EXPERT), BF16),
                            pltpu.VMEM((D_EXPERT, D_MODEL), BF16)]),
        out_shape=_sds((P, D_MODEL), F32),
        compiler_params=_cp(("arbitrary",)), name="moe_experts",
    )(blk_e, n_used, xg, w1, w3, w2)


def _comb_kernel(x_ref, rt_ref, mod_ref, fnw_ref, dest_hbm, y_hbm, o_ref, d_sm, buf, sem_i, sem_g,
                 *, tm, final):
    i = pl.program_id(0)
    idx_cp = pltpu.make_async_copy(dest_hbm.at[i], d_sm, sem_i)
    idx_cp.start()
    idx_cp.wait()

    def issue(t, carry):
        for kk in range(2):
            _row_copy(y_hbm, d_sm[kk, t], buf.at[kk], t, sem_g).start()
        return carry

    lax.fori_loop(0, tm, issue, 0)

    def drain(t, carry):
        for kk in range(2):
            _row_copy(y_hbm, d_sm[kk, t], buf.at[kk], t, sem_g).wait()
        return carry

    lax.fori_loop(0, tm, drain, 0)
    rt = rt_ref[...]
    y = rt[:, 2:3] * buf[0] + rt[:, 3:4] * buf[1]
    out = x_ref[...] + mod_ref[0, 2] * y
    if final:
        out = _rms(out) * fnw_ref[...]
    o_ref[...] = out


def _comb_call(xn, route, mod, fnw, dest, yb, L, tm, final):
    T = xn.shape[0]
    per_b = L // tm
    return pl.pallas_call(
        functools.partial(_comb_kernel, tm=tm, final=final), grid=(T // tm,),
        in_specs=[pl.BlockSpec((tm, D_MODEL), lambda i: (i, 0)),
                  pl.BlockSpec((tm, LANES), lambda i: (i, 0)),
                  pl.BlockSpec((1, 3, 1, D_MODEL), lambda i: (i // per_b, 0, 0, 0)),
                  pl.BlockSpec((1, D_MODEL), lambda i: (0, 0)),
                  pl.BlockSpec(memory_space=pl.ANY),
                  pl.BlockSpec(memory_space=pl.ANY)],
        out_specs=pl.BlockSpec((tm, D_MODEL), lambda i: (i, 0)),
        out_shape=_sds((T, D_MODEL), F32),
        scratch_shapes=[pltpu.SMEM((8, tm), I32), pltpu.VMEM((2, tm, D_MODEL), F32),
                        pltpu.SemaphoreType.DMA, pltpu.SemaphoreType.DMA],
        compiler_params=_cp(("arbitrary",)), name="moe_combine",
    )(xn, route, mod, fnw, dest, yb)


def _moe(xn, h2, route, mod, fnw, w1, w3, w2, layer, L, final):
    T = xn.shape[0]
    tr = min(512, T)
    rank, counts = _rank_call(route, tr)
    cnt = counts[0, :N_EXPERTS].astype(I32)
    padded = (cnt + MOE_ROWS - 1) // MOE_ROWS * MOE_ROWS
    pend = jnp.cumsum(padded)
    pstart = jnp.zeros((1, LANES), F32).at[0, :N_EXPERTS].set((pend - padded).astype(F32))
    nb = -(-(2 * T + N_EXPERTS * (MOE_ROWS - 1)) // MOE_ROWS)
    blk_e = jnp.clip(jnp.searchsorted(pend, jnp.arange(nb, dtype=I32) * MOE_ROWS, side='right'),
                     0, N_EXPERTS - 1).astype(I32)
    n_used = (pend[N_EXPERTS - 1:] // MOE_ROWS).astype(I32)
    td = min(256, T)
    tc = min(td, L)
    dest_d = _dest_call(route, rank, pstart, td)
    dest_c = dest_d if tc == td else _dest_call(route, rank, pstart, tc)
    xg = _disp_call(h2, dest_d, jnp.zeros((nb * MOE_ROWS, D_MODEL), F32), td)
    yb = _expert_call(blk_e, n_used, xg, w1, w3, w2, layer)
    return _comb_call(xn, route, mod, fnw, dest_c, yb, L, tc, final)


def _in1_kernel(x_ref, mod_ref, nw_ref, win_ref, qnw_ref, wuq_ref, kvnw_ref, tq_ref, tk_ref,
                q_ref, ckv_ref, kr_ref):
    h = _norm_mod(x_ref[...], nw_ref[...], mod_ref[0, 0], mod_ref[0, 1])
    zz = _dot_bf(h, win_ref[...])
    cqn = _rms(zz[:, :MLA_Q_RANK]) * qnw_ref[...]
    q = _dot_bf(cqn, wuq_ref[...])
    tq = tq_ref[...]
    for hh in range(MLA_H):
        q_ref[:, hh * LANES:(hh + 1) * LANES] = (q[:, hh * LANES:(hh + 1) * LANES] * tq).astype(BF16)
    c0 = MLA_Q_RANK
    ckv_ref[...] = _rms(zz[:, c0:c0 + MLA_KV_RANK]) * kvnw_ref[...]
    c1 = c0 + MLA_KV_RANK
    tk = tk_ref[...]
    kr_ref[...] = (zz[:, c1:c1 + MLA_ROPE] * tk[:, :MLA_ROPE]
                   + zz[:, c1 + LANES:c1 + LANES + MLA_ROPE] * tk[:, MLA_ROPE:])


def _in1_call(x, mod, nw, w_in_r, q_nw, w_uq_r, kv_nw, tab_q, tab_k, L, tm):
    T = x.shape[0]
    per_b = L // tm
    tok = lambda i: (i, 0)
    const = lambda i: (0, 0)
    pos = lambda i: (i % per_b, 0)
    return pl.pallas_call(
        _in1_kernel, grid=(T // tm,),
        in_specs=[pl.BlockSpec((tm, D_MODEL), tok),
                  pl.BlockSpec((1, 3, 1, D_MODEL), lambda i: (i // per_b, 0, 0, 0)),
                  pl.BlockSpec((1, D_MODEL), const),
                  pl.BlockSpec((D_MODEL, MLA_IN_R), const),
                  pl.BlockSpec((1, MLA_Q_RANK), const),
                  pl.BlockSpec((MLA_Q_RANK, MLA_H * LANES), const),
                  pl.BlockSpec((1, MLA_KV_RANK), const),
                  pl.BlockSpec((tm, LANES), pos),
                  pl.BlockSpec((tm, 2 * MLA_ROPE), pos)],
        out_specs=[pl.BlockSpec((tm, MLA_H * LANES), tok),
                   pl.BlockSpec((tm, MLA_KV_RANK), tok),
                   pl.BlockSpec((tm, MLA_ROPE), tok)],
        out_shape=[_sds((T, MLA_H * LANES), BF16), _sds((T, MLA_KV_RANK), F32),
                   _sds((T, MLA_ROPE), F32)],
        compiler_params=_cp(("arbitrary",)), name="in1",
    )(x, mod, nw, w_in_r, q_nw, w_uq_r, kv_nw, tab_q, tab_k)


def _kv_kernel(ckv_ref, kr_ref, wk_ref, pe_ref, wv_ref, k_ref, v_ref):
    c = ckv_ref[...].astype(BF16)
    k_ref[...] = (jnp.dot(c, wk_ref[...], preferred_element_type=F32)
                  + jnp.dot(kr_ref[...].astype(BF16), pe_ref[...], preferred_element_type=F32)).astype(BF16)
    v_ref[...] = jnp.dot(c, wv_ref[...], preferred_element_type=F32).astype(BF16)


def _kv_call(ckv, kr, wk_r, place, wv_r, tm):
    T = ckv.shape[0]
    tok = lambda i: (i, 0)
    const = lambda i: (0, 0)
    return pl.pallas_call(
        _kv_kernel, grid=(T // tm,),
        in_specs=[pl.BlockSpec((tm, MLA_KV_RANK), tok),
                  pl.BlockSpec((tm, MLA_ROPE), tok),
                  pl.BlockSpec((MLA_KV_RANK, MLA_H * LANES), const),
                  pl.BlockSpec((MLA_ROPE, MLA_H * LANES), const),
                  pl.BlockSpec((MLA_KV_RANK, MLA_H * MLA_V), const)],
        out_specs=[pl.BlockSpec((tm, MLA_H * LANES), tok), pl.BlockSpec((tm, MLA_H * MLA_V), tok)],
        out_shape=[_sds((T, MLA_H * LANES), BF16), _sds((T, MLA_H * MLA_V), BF16)],
        compiler_params=_cp(("arbitrary",)), name="mla_kv",
    )(ckv, kr, wk_r, place, wv_r)


def _attn_kernel(q_ref, k_ref, v_ref, o_ref, *, tq, tk, Lk, pos0):
    q_lo = pos0 + pl.program_id(2) * tq
    k_hi = jnp.minimum(((q_lo + tq - 1) // CHUNK + 1) * CHUNK, Lk)
    n_blk = (k_hi + tk - 1) // tk
    n_full = jnp.minimum(((q_lo // CHUNK + 1) * CHUNK) // tk, n_blk)
    q_chunk = jnp.right_shift(q_lo + lax.broadcasted_iota(I32, (tq, tk), 0), CHUNK_SHIFT)
    k_iota = lax.broadcasted_iota(I32, (tq, tk), 1)
    qs = [q_ref[:, hh * LANES:(hh + 1) * LANES] for hh in range(2)]

    def step(j, carry, masked):
        k0 = pl.multiple_of(j * tk, tk)
        vb = v_ref[pl.ds(k0, tk), :]
        new = []
        for hh in range(2):
            m, l, acc = carry[hh]
            kb = k_ref[pl.ds(k0, tk), hh * LANES:(hh + 1) * LANES]
            s = lax.dot_general(qs[hh], kb, NT, preferred_element_type=F32)
            if masked:
                s = jnp.where(q_chunk >= jnp.right_shift(k0 + k_iota, CHUNK_SHIFT), s, NEG)
            m_new = jnp.maximum(m, jnp.max(s, axis=-1, keepdims=True))
            alpha = jnp.exp(m - m_new)
            p = jnp.exp(s - m_new)
            l = alpha * l + jnp.sum(p, axis=-1, keepdims=True)
            acc = alpha * acc + jnp.dot(p.astype(BF16), vb, preferred_element_type=F32)
            new.append((m_new, l, acc))
        return tuple(new)

    init = tuple((jnp.full((tq, 1), NEG, F32), jnp.zeros((tq, 1), F32), jnp.zeros((tq, LANES), F32))
                 for _ in range(2))
    carry = lax.fori_loop(0, n_full, functools.partial(step, masked=False), init)
    carry = lax.fori_loop(n_full, n_blk, functools.partial(step, masked=True), carry)
    lane = lax.broadcasted_iota(I32, (tq, LANES), 1)
    o0 = carry[0][2] / carry[0][1]
    o1 = carry[1][2] / carry[1][1]
    o_ref[...] = jnp.where(lane < MLA_V, o0, o1).astype(BF16)


def _attn_call(q, k, v, B, Lq, Lk, pos0, tq, tk):
    nq = Lq // tq
    return pl.pallas_call(
        functools.partial(_attn_kernel, tq=tq, tk=tk, Lk=Lk, pos0=pos0), grid=(B, MLA_H // 2, nq),
        in_specs=[pl.BlockSpec((tq, 2 * LANES), lambda b, hp, i: (b * nq + i, hp)),
                  pl.BlockSpec((Lk, 2 * LANES), lambda b, hp, i: (b, hp)),
                  pl.BlockSpec((Lk, 2 * MLA_V), lambda b, hp, i: (b, hp))],
        out_specs=pl.BlockSpec((tq, 2 * MLA_V), lambda b, hp, i: (b * nq + i, hp)),
        out_shape=_sds((B * Lq, MLA_H * MLA_V), BF16),
        compiler_params=_cp(("arbitrary", "arbitrary", "arbitrary")), name="mla_attn",
    )(q, k, v)


def _split_cols(w, widths):
    offs = [0]
    for n in widths:
        offs.append(offs[-1] + n)
    return [w[:, offs[i]:offs[i + 1]] for i in range(len(widths))]


def _prep_even(ab_w_in, gla_w_alpha, gdn_a_log, gdn_dt_bias):
    gq, gk, gv, glr, gr, qkv, ga, gb, gg = _split_cols(ab_w_in, AB_IN_WIDTHS)
    w_main = jnp.concatenate([gq, gk, gv, gr, qkv, gg], axis=1).astype(BF16)
    small = jnp.concatenate([glr, ga, gb], axis=1)
    w_small = jnp.pad(small, ((0, 0), (0, LANES - small.shape[1])))
    wa_pad = jnp.pad(gla_w_alpha, ((0, LANES - GLA_LR), (0, 0)))
    alog_v = jnp.zeros((1, LANES), F32).at[0, LANE_GA:LANE_GA + GDN_H].set(gdn_a_log)
    dtb_v = jnp.zeros((1, LANES), F32).at[0, LANE_GA:LANE_GA + GDN_H].set(gdn_dt_bias)
    return w_main, w_small, wa_pad, alog_v, dtb_v


def _swap_halves(w):
    half = w.shape[-1] // 2
    return jnp.concatenate([w[..., half:], w[..., :half]], axis=-1)


def _prep_odd(mla_w_in, mla_w_uq, mla_w_ukv):
    cq, ckv, kr = _split_cols(mla_w_in, (MLA_Q_RANK, MLA_KV_RANK, MLA_ROPE))
    pad = lambda w: jnp.pad(w, ((0, 0), (0, LANES - w.shape[1])))
    w_in_r = jnp.concatenate([cq, ckv, pad(kr), pad(_swap_halves(kr))], axis=1).astype(BF16)
    uq = mla_w_uq.reshape(MLA_Q_RANK, MLA_H, MLA_NOPE + MLA_ROPE)
    uq_rope = uq[..., MLA_NOPE:]
    w_uq_r = jnp.concatenate([uq, _swap_halves(uq_rope)], axis=-1).reshape(
        MLA_Q_RANK, MLA_H * LANES).astype(BF16)
    ukv = mla_w_ukv.reshape(MLA_KV_RANK, MLA_H, MLA_NOPE + MLA_V)
    wk_r = jnp.pad(ukv[..., :MLA_NOPE], ((0, 0), (0, 0), (0, LANES - MLA_NOPE))).reshape(
        MLA_KV_RANK, MLA_H * LANES).astype(BF16)
    wv_r = ukv[..., MLA_NOPE:].reshape(MLA_KV_RANK, MLA_H * MLA_V).astype(BF16)
    eye = jnp.eye(MLA_ROPE, dtype=F32)
    place = jnp.concatenate([jnp.zeros((MLA_ROPE, MLA_NOPE), F32), eye, eye], axis=1)
    place = jnp.tile(place, (1, MLA_H)).astype(BF16)
    return w_in_r, w_uq_r, wk_r, wv_r, place


def _rope_tables(pos0, L):
    half = MLA_ROPE // 2
    inv = jnp.exp(-math.log(ROPE_THETA) * jnp.arange(half, dtype=F32) / half)
    ang = (pos0 + jnp.arange(L, dtype=I32)).astype(F32)[:, None] * inv[None, :]
    cos, sin = jnp.cos(ang), jnp.sin(ang)
    tab_k = jnp.concatenate([cos, cos, -sin, sin], axis=1)
    scale = (MLA_NOPE + MLA_ROPE) ** -0.5
    tab_q = jnp.concatenate([jnp.ones((L, MLA_NOPE), F32), tab_k], axis=1) * scale
    return tab_q, tab_k


def _mod4(mods, lo, hi):
    return [mods[i, lo:hi].reshape(hi - lo, 3, 1, D_MODEL) for i in range(mods.shape[0])]


def _trunk(x3, mods, pos0, gla_s, gdn_s, conv_s, past_ckv, past_kr, p):
    B, L, _ = x3.shape
    T = B * L
    tm = min(512, L)
    x = x3.reshape(T, D_MODEL)
    row = lambda a: a.reshape(1, -1)

    w_main, w_small, wa_pad, alog_v, dtb_v = p['even']
    zm, zs = _in0_call(x, mods[0], row(p['norm_w'][0, 0]), w_main, w_small, L, tm)
    o_gla, gla_t = _gla_call(zm, zs, wa_pad, row(p['gla_b_alpha'][0]), row(p['gla_norm_w'][0]),
                             jnp.swapaxes(gla_s, -1, -2), B, L)
    hist8 = jnp.pad(conv_s, ((0, 0), (8 - (CONV_W - 1), 0), (0, 0)))
    o_gdn, gdn_new = _gdn_call(zm, zs, p['gdn_conv_w'][0], alog_v, dtb_v, row(p['gdn_norm_w'][0]),
                               hist8, gdn_s, B, L)
    qkv0 = 3 * GLA_H * GLA_DV
    conv_new = zm.reshape(B, L, AB_MAIN)[:, L - (CONV_W - 1):, qkv0:qkv0 + GDN_CONV_DIM].astype(F32)
    w_out = p['ab_w_out_bf']
    half = GLA_H * GLA_DV
    xn, h2, route = _post_call([o_gla, o_gdn], [w_out[:half], w_out[half:]], x, mods[0], mods[1],
                               row(p['norm_w'][0, 1]), p['w_route'][0], p['b_route'][0], L, tm)
    x = _moe(xn, h2, route, mods[1], row(p['final_norm_w']), p['moe_w1'], p['moe_w3'], p['moe_w2'],
             0, L, final=False)

    w_in_r, w_uq_r, wk_r, wv_r, place = p['odd']
    tab_q, tab_k = _rope_tables(pos0, L)
    q, ckv, kr = _in1_call(x, mods[2], row(p['norm_w'][1, 0]), w_in_r, row(p['mla_q_norm_w'][0]),
                           w_uq_r, row(p['mla_kv_norm_w'][0]), tab_q, tab_k, L, tm)
    if past_ckv is None:
        ckv_all, kr_all, Lk = ckv, kr, L
    else:
        Lk = past_ckv.shape[1] + L
        ckv_all = jnp.concatenate([past_ckv, ckv.reshape(B, L, -1)], axis=1).reshape(B * Lk, -1)
        kr_all = jnp.concatenate([past_kr, kr.reshape(B, L, -1)], axis=1).reshape(B * Lk, -1)
    tkv = 512 if (B * Lk) % 512 == 0 else Lk
    k_all, v_all = _kv_call(ckv_all, kr_all, wk_r, place, wv_r, tkv)
    tq = min(256, L)
    tk = 256 if Lk % 256 == 0 else Lk
    att = _attn_call(q, k_all, v_all, B, L, Lk, pos0, tq, tk)
    xn, h2, route = _post_call([att], [p['mla_w_out_bf']], x, mods[2], mods[3],
                               row(p['norm_w'][1, 1]), p['w_route'][1], p['b_route'][1], L, tm)
    y = _moe(xn, h2, route, mods[3], row(p['final_norm_w']), p['moe_w1'], p['moe_w3'], p['moe_w2'],
             1, L, final=True)
    return (y.reshape(B, L, D_MODEL), jnp.swapaxes(gla_t, -1, -2)[None], gdn_new[None], conv_new[None],
            ckv.reshape(1, B, L, MLA_KV_RANK), kr.reshape(1, B, L, MLA_ROPE))


def kernel(x_prompt, x_sample, state_gla, state_gdn, state_gdn_conv, cache_mla_ckv, cache_mla_krope,
           c_prompt, c_sample, ada_w, ada_b, norm_w, final_norm_w, ab_w_in, gla_w_alpha, gla_b_alpha,
           gla_norm_w, gdn_conv_w, gdn_a_log, gdn_dt_bias, gdn_norm_w, ab_w_out, mla_w_in, mla_q_norm_w,
           mla_w_uq, mla_kv_norm_w, mla_w_ukv, mla_w_out, moe_w_group, moe_b_group, moe_w_expert,
           moe_b_expert, moe_w1, moe_w3, moe_w2):
    depth = ada_w.shape[0]
    bp, bs = x_prompt.shape[0], x_sample.shape[0]
    mods = _ada_call(jnp.concatenate([c_prompt, c_sample], axis=0),
                     ada_w.reshape(2 * depth, D_MODEL, 3 * D_MODEL), ada_b.reshape(2 * depth, 1, 3 * D_MODEL))
    w_route = jnp.pad(jnp.concatenate([moe_w_group, moe_w_expert], axis=-1),
                      ((0, 0), (0, 0), (0, LANES - N_GROUPS - N_EXPERTS)))
    b_route = jnp.pad(jnp.concatenate([moe_b_group, moe_b_expert], axis=-1),
                      ((0, 0), (0, LANES - N_GROUPS - N_EXPERTS)))[:, None, :]
    p = dict(norm_w=norm_w, final_norm_w=final_norm_w, gla_b_alpha=gla_b_alpha, gla_norm_w=gla_norm_w,
             gdn_conv_w=gdn_conv_w, gdn_norm_w=gdn_norm_w, mla_q_norm_w=mla_q_norm_w,
             mla_kv_norm_w=mla_kv_norm_w, moe_w1=moe_w1, moe_w3=moe_w3, moe_w2=moe_w2,
             w_route=w_route, b_route=b_route,
             even=_prep_even(ab_w_in[0], gla_w_alpha[0], gdn_a_log[0], gdn_dt_bias[0]),
             odd=_prep_odd(mla_w_in[0], mla_w_uq[0], mla_w_ukv[0]),
             ab_w_out_bf=ab_w_out[0].astype(BF16), mla_w_out_bf=mla_w_out[0].astype(BF16))
    y_p, gla_p, gdn_p, conv_p, ckv_p, kr_p = _trunk(
        x_prompt, _mod4(mods, 0, bp), 0,
        jnp.zeros((bp, GLA_H, GLA_DK, GLA_DV), F32), jnp.zeros((bp, GDN_H, GDN_DK, GDN_DV), F32),
        jnp.zeros((bp, CONV_W - 1, GDN_CONV_DIM), F32), None, None, p)
    y_s, gla_s, gdn_s, conv_s, ckv_s, kr_s = _trunk(
        x_sample, _mod4(mods, bp, bp + bs), cache_mla_ckv.shape[2],
        state_gla[0], state_gdn[0], state_gdn_conv[0], cache_mla_ckv[0], cache_mla_krope[0], p)
    return (y_p, y_s, gla_p, gdn_p, conv_p, ckv_p, kr_p, gla_s, gdn_s, conv_s, ckv_s, kr_s)
```

```python
import functools
import math

import jax
import jax.numpy as jnp
from jax import lax
from jax.experimental import pallas as pl
from jax.experimental.pallas import tpu as pltpu

F32 = jnp.float32
BF16 = jnp.bfloat16
I32 = jnp.int32
HI = lax.Precision.HIGHEST

D_MODEL = 1024
CHUNK = 64
GLA_H, GLA_DK, GLA_DV, GLA_LR, GLA_TAU = 4, 64, 128, 16, 16.0
GDN_H, GDN_DK, GDN_DV, CONV_W = 4, 128, 128, 4
GDN_CONV_DIM = GDN_H * (2 * GDN_DK + GDN_DV)
AB_IN_WIDTHS = (GLA_H * GLA_DK, GLA_H * GLA_DK, GLA_H * GLA_DV, GLA_LR, GLA_H * GLA_DV,
                GDN_CONV_DIM, GDN_H, GDN_H, GDN_H * GDN_DV)
AB_MAIN = 3584
MLA_H, MLA_NOPE, MLA_ROPE, MLA_V = 16, 64, 32, 64
MLA_Q_RANK, MLA_KV_RANK = 384, 256
MLA_IN_R = MLA_Q_RANK + MLA_KV_RANK + 256
ROPE_THETA = 10000.0
N_GROUPS, EXPERTS_PER_GROUP, N_EXPERTS, D_EXPERT = 4, 8, 32, 512
GROUP_SHIFT = int(math.log2(EXPERTS_PER_GROUP))
CHUNK_SHIFT = int(math.log2(CHUNK))
EPS = 1e-6

LANES = 128
GLA_SUB = 16
SCAN_CHUNKS = 4
GLA_SAFE_LOG_DECAY = -60.0
MOE_ROWS = 256
ATTN_TILE = 512
NEG = -3.0e38
VMEM_LIMIT = 56 * 1024 * 1024

LANE_GA = GLA_LR
LANE_GB = GLA_LR + GDN_H

NT = (((1,), (1,)), ((), ()))
TN = (((0,), (0,)), ((), ()))


def _cp(sem, vmem=None):
    return pltpu.CompilerParams(dimension_semantics=sem, vmem_limit_bytes=vmem or VMEM_LIMIT)


def _sds(shape, dtype):
    return jax.ShapeDtypeStruct(shape, dtype)


def _sigmoid(x):
    return 1.0 / (1.0 + jnp.exp(-x))


def _softplus(x):
    return jnp.maximum(x, 0.0) + jnp.log(1.0 + jnp.exp(-jnp.abs(x)))


def _rms(x):
    return x * lax.rsqrt(jnp.mean(x * x, axis=-1, keepdims=True) + EPS)


def _norm_mod(x, nw, shift, scale):
    return (_rms(x) * nw) * (1.0 + scale) + shift


def _dot_bf(a, b):
    return jnp.dot(a.astype(BF16), b.astype(BF16), preferred_element_type=F32)


def _dot_hi(a, b):
    return jnp.dot(a, b, precision=HI, preferred_element_type=F32)


def _ada_kernel(c_ref, w_ref, b_ref, o_ref):
    c = c_ref[...]
    o_ref[0] = _dot_hi(c * _sigmoid(c), w_ref[0]) + b_ref[0]


def _ada_call(c_all, ada_w, ada_b):
    n = ada_w.shape[0]
    nb = c_all.shape[0]
    return pl.pallas_call(
        _ada_kernel, grid=(n, 3),
        in_specs=[pl.BlockSpec((nb, D_MODEL), lambda i, j: (0, 0)),
                  pl.BlockSpec((1, D_MODEL, D_MODEL), lambda i, j: (i, 0, j)),
                  pl.BlockSpec((1, 1, D_MODEL), lambda i, j: (i, 0, j))],
        out_specs=pl.BlockSpec((1, nb, D_MODEL), lambda i, j: (i, 0, j)),
        out_shape=_sds((n, nb, 3 * D_MODEL), F32),
        compiler_params=_cp(("arbitrary", "arbitrary")), name="ada",
    )(c_all, ada_w, ada_b)


def _in0_kernel(x_ref, mod_ref, nw_ref, wm_ref, ws_ref, zm_ref, zs_ref):
    h = _norm_mod(x_ref[...], nw_ref[...], mod_ref[0, 0], mod_ref[0, 1])
    zm_ref[...] = _dot_bf(h, wm_ref[...]).astype(BF16)
    zs_ref[...] = _dot_hi(h, ws_ref[...])


def _in0_call(x, mod, nw, w_main, w_small, L, tm):
    T = x.shape[0]
    per_b = L // tm
    return pl.pallas_call(
        _in0_kernel, grid=(T // tm,),
        in_specs=[pl.BlockSpec((tm, D_MODEL), lambda i: (i, 0)),
                  pl.BlockSpec((1, 3, 1, D_MODEL), lambda i: (i // per_b, 0, 0, 0)),
                  pl.BlockSpec((1, D_MODEL), lambda i: (0, 0)),
                  pl.BlockSpec((D_MODEL, AB_MAIN), lambda i: (0, 0)),
                  pl.BlockSpec((D_MODEL, LANES), lambda i: (0, 0))],
        out_specs=[pl.BlockSpec((tm, AB_MAIN), lambda i: (i, 0)),
                   pl.BlockSpec((tm, LANES), lambda i: (i, 0))],
        out_shape=[_sds((T, AB_MAIN), BF16), _sds((T, LANES), F32)],
        compiler_params=_cp(("arbitrary",)), name="in0",
    )(x, mod, nw, w_main, w_small)


def _gated_head_out(o, nw, gate):
    gate = gate.astype(F32)
    return (_rms(o) * nw * (gate * _sigmoid(gate))).astype(BF16)


def _gla_chunk_exact(q, k, v, b, st):
    CL = q.shape[0]
    row = lax.broadcasted_iota(I32, (GLA_SUB, 1), 0)
    bprev = jnp.zeros((1, GLA_DK), F32)
    outs = []
    for blk in range(CL // GLA_SUB):
        sl = slice(blk * GLA_SUB, (blk + 1) * GLA_SUB)
        qi, ki, vi = q[sl], k[sl], v[sl]
        brel = b[sl] - bprev
        bend = brel[GLA_SUB - 1:GLA_SUB]
        oi = lax.dot_general((qi * jnp.exp(brel)).astype(BF16), st.astype(BF16), NT,
                             preferred_element_type=F32)
        for j in range(GLA_SUB):
            d = brel - brel[j:j + 1]
            e = jnp.where(row >= j, jnp.exp(jnp.minimum(d, 0.0)), 0.0)
            a = jnp.sum(qi * ki[j:j + 1] * e, axis=-1, keepdims=True)
            oi = oi + a * vi[j:j + 1]
        outs.append(oi)
        khat = ki * jnp.exp(bend - brel)
        st = st * jnp.exp(bend) + lax.dot_general(vi.astype(BF16), khat.astype(BF16), TN,
                                                  preferred_element_type=F32)
        bprev = b[(blk + 1) * GLA_SUB - 1:(blk + 1) * GLA_SUB]
    return jnp.concatenate(outs, axis=0), st


def _gla_chunk_fast(q, k, v, b, st, causal):
    CL = q.shape[0]
    qh = (q * jnp.exp(b)).astype(BF16)
    kh = (k * jnp.exp(-b)).astype(BF16)
    vb = v.astype(BF16)
    att = jnp.where(causal, lax.dot_general(qh, kh, NT, preferred_element_type=F32), 0.0)
    o = (lax.dot_general(qh, st.astype(BF16), NT, preferred_element_type=F32)
         + jnp.dot(att.astype(BF16), vb, preferred_element_type=F32))
    blast = b[CL - 1:CL]
    kdec = (k * jnp.exp(blast - b)).astype(BF16)
    st = st * jnp.exp(blast) + lax.dot_general(vb, kdec, TN, preferred_element_type=F32)
    return o, st


def _gla_kernel(q_ref, k_ref, v_ref, gr_ref, zs_ref, wa_ref, ba_ref, nw_ref, s0_ref,
                o_ref, st_ref, st_sc, b_sc, *, CL, nch, nsteps):
    step = pl.program_id(1)

    @pl.when(step == 0)
    def _():
        st_sc[...] = s0_ref[0]

    r = lax.broadcasted_iota(I32, (CL, CL), 0)
    cc = lax.broadcasted_iota(I32, (CL, CL), 1)
    causal = r >= cc
    tri = causal.astype(F32)
    b_min = None
    for ch in range(nch):
        rows = slice(ch * CL, (ch + 1) * CL)
        pre = _dot_hi(zs_ref[rows, :], wa_ref[...]) + ba_ref[...]
        la = -_softplus(-pre) * (1.0 / GLA_TAU)
        b_ch = _dot_hi(tri, la)
        b_sc[rows, :] = b_ch
        lo = jnp.min(b_ch[CL - 1:CL, :])
        b_min = lo if b_min is None else jnp.minimum(b_min, lo)
    safe = b_min > GLA_SAFE_LOG_DECAY

    def run(chunk_fn):
        nw = nw_ref[...]
        for h in range(GLA_H):
            kc = slice(h * GLA_DK, (h + 1) * GLA_DK)
            vc = slice(h * GLA_DV, (h + 1) * GLA_DV)
            st = st_sc[h]
            for ch in range(nch):
                rows = slice(ch * CL, (ch + 1) * CL)
                q = q_ref[rows, kc].astype(F32) * GLA_DK ** -0.5
                o, st = chunk_fn(q, k_ref[rows, kc].astype(F32), v_ref[rows, vc].astype(F32),
                                 b_sc[rows, kc], st)
                o_ref[rows, vc] = _gated_head_out(o, nw, gr_ref[rows, vc])
            st_sc[h] = st

    @pl.when(safe)
    def _():
        run(functools.partial(_gla_chunk_fast, causal=causal))

    @pl.when(jnp.logical_not(safe))
    def _():
        run(_gla_chunk_exact)

    @pl.when(step == nsteps - 1)
    def _():
        st_ref[0] = st_sc[...]


def _gla_call(zm, zs, wa_pad, b_alpha, nw, s0t, B, L):
    CL = min(CHUNK, L)
    nch = SCAN_CHUNKS if L % (CL * SCAN_CHUNKS) == 0 else 1
    rows = CL * nch
    nsteps = L // rows
    T = B * L
    qk_w = GLA_H * GLA_DK
    v_w = GLA_H * GLA_DV
    row = lambda b, c: b * nsteps + c
    return pl.pallas_call(
        functools.partial(_gla_kernel, CL=CL, nch=nch, nsteps=nsteps), grid=(B, nsteps),
        in_specs=[pl.BlockSpec((rows, qk_w), lambda b, c: (row(b, c), 0)),
                  pl.BlockSpec((rows, qk_w), lambda b, c: (row(b, c), 1)),
                  pl.BlockSpec((rows, v_w), lambda b, c: (row(b, c), 1)),
                  pl.BlockSpec((rows, v_w), lambda b, c: (row(b, c), 2)),
                  pl.BlockSpec((rows, LANES), lambda b, c: (row(b, c), 0)),
                  pl.BlockSpec((LANES, qk_w), lambda b, c: (0, 0)),
                  pl.BlockSpec((1, qk_w), lambda b, c: (0, 0)),
                  pl.BlockSpec((1, GLA_DV), lambda b, c: (0, 0)),
                  pl.BlockSpec((1, GLA_H, GLA_DV, GLA_DK), lambda b, c: (b, 0, 0, 0))],
        out_specs=[pl.BlockSpec((rows, v_w), lambda b, c: (row(b, c), 0)),
                   pl.BlockSpec((1, GLA_H, GLA_DV, GLA_DK), lambda b, c: (b, 0, 0, 0))],
        out_shape=[_sds((T, v_w), BF16), _sds((B, GLA_H, GLA_DV, GLA_DK), F32)],
        scratch_shapes=[pltpu.VMEM((GLA_H, GLA_DV, GLA_DK), F32), pltpu.VMEM((rows, qk_w), F32)],
        compiler_params=_cp(("arbitrary", "arbitrary")), name="gla",
    )(zm, zm, zm, zm, zs, wa_pad, b_alpha, nw, s0t)


def _split_bf(a):
    hi = a.astype(BF16)
    return hi, (a - hi.astype(F32)).astype(BF16)


def _dot3(a_hl, b_hl):
    (ah, al), (bh, bl) = a_hl, b_hl
    d = functools.partial(jnp.dot, preferred_element_type=F32)
    return d(ah, bh) + d(ah, bl) + d(al, bh)


def _gdn_kernel(qkv_ref, zs_ref, gg_ref, cw_ref, alog_ref, dtb_ref, nw_ref, hist_ref, s0_ref,
                o_ref, s_ref, cv_sc, st_sc, *, CL, nch, nsteps):
    step = pl.program_id(1)
    R = CL * nch

    @pl.when(step == 0)
    def _():
        cv_sc[0:8, :] = hist_ref[0]
        st_sc[...] = s0_ref[0]

    x = qkv_ref[...].astype(F32)
    cv_sc[8:8 + R, :] = x
    cw = cw_ref[...]
    conv = (cv_sc[5:5 + R, :] * cw[0:1] + cv_sc[6:6 + R, :] * cw[1:2]
            + cv_sc[7:7 + R, :] * cw[2:3] + x * cw[3:4])
    cv_sc[0:8, :] = x[R - 8:R]
    conv = conv * _sigmoid(conv)

    zs = zs_ref[...]
    g_all = -jnp.exp(alog_ref[...]) * _softplus(zs + dtb_ref[...])
    beta_all = _sigmoid(zs)
    r = lax.broadcasted_iota(I32, (CL, CL), 0)
    cc = lax.broadcasted_iota(I32, (CL, CL), 1)
    tri = (r >= cc).astype(F32)
    eye = (r == cc).astype(F32)
    nw = nw_ref[...]
    qk_w = GDN_H * GDN_DK

    units = [(ch, h) for ch in range(nch) for h in range(GDN_H)]
    stage = {}
    for ch in range(nch):
        rows = slice(ch * CL, (ch + 1) * CL)
        gam = _dot_hi(tri, g_all[rows])
        gam_t = gam.T
        for h in range(GDN_H):
            q = conv[rows, h * GDN_DK:(h + 1) * GDN_DK]
            k = conv[rows, qk_w + h * GDN_DK:qk_w + (h + 1) * GDN_DK]
            v = conv[rows, 2 * qk_w + h * GDN_DV:2 * qk_w + (h + 1) * GDN_DV]
            q = q * lax.rsqrt(jnp.sum(q * q, axis=-1, keepdims=True) + EPS) * GDN_DK ** -0.5
            k = k * lax.rsqrt(jnp.sum(k * k, axis=-1, keepdims=True) + EPS)
            gcol = gam[:, LANE_GA + h:LANE_GA + h + 1]
            grow = gam_t[LANE_GA + h:LANE_GA + h + 1, :]
            bcol = beta_all[rows, LANE_GB + h:LANE_GB + h + 1]
            dec = jnp.where(r >= cc, jnp.exp(jnp.minimum(gcol - grow, 0.0)), 0.0)
            eg = jnp.exp(gcol)
            glast = gcol[CL - 1:CL]
            stage[ch, h] = dict(
                kb=k.astype(BF16), qb=q.astype(BF16), dec=dec, bcol=bcol,
                ub=(v * bcol).astype(BF16), wb=(k * (bcol * eg)).astype(BF16),
                qe=(q * eg).astype(BF16), kdec=(k * jnp.exp(glast - gcol)).astype(BF16),
                elast=jnp.exp(glast))
    for un in units:
        d = stage[un]
        kk = lax.dot_general(d['kb'], d['kb'], NT, preferred_element_type=F32)
        d['pw'] = jnp.where(r > cc, d['bcol'] * kk * d['dec'], 0.0)
        d['t'] = eye - d['pw']
    for _ in range(int(math.log2(CL)) - 1):
        for un in units:
            d = stage[un]
            hl = _split_bf(d['pw'])
            d['pw'] = _dot3(hl, hl)
        for un in units:
            d = stage[un]
            d['t'] = d['t'] + _dot3(_split_bf(d['t']), _split_bf(d['pw']))
    pre = {}
    for un in units:
        d = stage[un]
        tb = d['t'].astype(BF16)
        u = jnp.dot(tb, d['ub'], preferred_element_type=F32)
        w = jnp.dot(tb, d['wb'], preferred_element_type=F32).astype(BF16)
        qk = (lax.dot_general(d['qb'], d['kb'], NT, preferred_element_type=F32) * d['dec']).astype(BF16)
        pre[un] = (u, w, qk, d['qe'], d['kdec'], d['elast'])

    for h in range(GDN_H):
        vc = slice(h * GDN_DV, (h + 1) * GDN_DV)
        st = st_sc[h]
        for ch in range(nch):
            rows = slice(ch * CL, (ch + 1) * CL)
            u, w, qk, qe, kdec, elast = pre[ch, h]
            stb = st.astype(BF16)
            delta = u - jnp.dot(w, stb, preferred_element_type=F32)
            db = delta.astype(BF16)
            o = (jnp.dot(qe, stb, preferred_element_type=F32)
                 + jnp.dot(qk, db, preferred_element_type=F32))
            st = st * elast + lax.dot_general(kdec, db, TN, preferred_element_type=F32)
            o_ref[rows, vc] = _gated_head_out(o, nw, gg_ref[rows, vc])
        st_sc[h] = st

    @pl.when(step == nsteps - 1)
    def _():
        s_ref[0] = st_sc[...]


def _gdn_call(zm, zs, conv_w, alog_v, dtb_v, nw, hist8, s0, B, L):
    CL = min(CHUNK, L)
    nch = SCAN_CHUNKS if L % (CL * SCAN_CHUNKS) == 0 else 1
    rows = CL * nch
    nsteps = L // rows
    T = B * L
    v_w = GDN_H * GDN_DV
    row = lambda b, c: b * nsteps + c
    return pl.pallas_call(
        functools.partial(_gdn_kernel, CL=CL, nch=nch, nsteps=nsteps), grid=(B, nsteps),
        in_specs=[pl.BlockSpec((rows, GDN_CONV_DIM), lambda b, c: (row(b, c), 1)),
                  pl.BlockSpec((rows, LANES), lambda b, c: (row(b, c), 0)),
                  pl.BlockSpec((rows, v_w), lambda b, c: (row(b, c), 6)),
                  pl.BlockSpec((CONV_W, GDN_CONV_DIM), lambda b, c: (0, 0)),
                  pl.BlockSpec((1, LANES), lambda b, c: (0, 0)),
                  pl.BlockSpec((1, LANES), lambda b, c: (0, 0)),
                  pl.BlockSpec((1, GDN_DV), lambda b, c: (0, 0)),
                  pl.BlockSpec((1, 8, GDN_CONV_DIM), lambda b, c: (b, 0, 0)),
                  pl.BlockSpec((1, GDN_H, GDN_DK, GDN_DV), lambda b, c: (b, 0, 0, 0))],
        out_specs=[pl.BlockSpec((rows, v_w), lambda b, c: (row(b, c), 0)),
                   pl.BlockSpec((1, GDN_H, GDN_DK, GDN_DV), lambda b, c: (b, 0, 0, 0))],
        out_shape=[_sds((T, v_w), BF16), _sds((B, GDN_H, GDN_DK, GDN_DV), F32)],
        scratch_shapes=[pltpu.VMEM((8 + rows, GDN_CONV_DIM), F32),
                        pltpu.VMEM((GDN_H, GDN_DK, GDN_DV), F32)],
        compiler_params=_cp(("arbitrary", "arbitrary")), name="gdn",
    )(zm, zs, zm, conv_w, alog_v, dtb_v, nw, hist8, s0)


def _post_kernel(*refs, n_in):
    a_refs = refs[:n_in]
    w_refs = refs[n_in:2 * n_in]
    x_ref, mod1_ref, mod2_ref, nw2_ref, wr_ref, br_ref, xn_ref, h2_ref, rt_ref = refs[2 * n_in:]
    acc = jnp.dot(a_refs[0][...], w_refs[0][...], preferred_element_type=F32)
    for a_ref, w_ref in zip(a_refs[1:], w_refs[1:]):
        acc = acc + jnp.dot(a_ref[...], w_ref[...], preferred_element_type=F32)
    xn = x_ref[...] + mod1_ref[0, 2] * acc
    xn_ref[...] = xn
    h2 = _norm_mod(xn, nw2_ref[...], mod2_ref[0, 0], mod2_ref[0, 1])
    h2_ref[...] = h2
    logits = _dot_hi(h2, wr_ref[...]) + br_ref[...]
    lane = lax.broadcasted_iota(I32, logits.shape, 1)
    is_g = lane < N_GROUPS
    gl = jnp.where(is_g, logits, NEG)
    gmax = jnp.max(gl, axis=-1, keepdims=True)
    g_sel = jnp.min(jnp.where(gl == gmax, lane, LANES), axis=-1, keepdims=True)
    p_g = 1.0 / jnp.sum(jnp.where(is_g, jnp.exp(logits - gmax), 0.0), axis=-1, keepdims=True)
    in_grp = jnp.logical_and(
        jnp.logical_and(lane >= N_GROUPS, lane < N_GROUPS + N_EXPERTS),
        jnp.right_shift(lane - N_GROUPS, GROUP_SHIFT) == g_sel)
    el = jnp.where(in_grp, logits, NEG)
    v1 = jnp.max(el, axis=-1, keepdims=True)
    i1 = jnp.min(jnp.where(jnp.logical_and(in_grp, el == v1), lane, LANES), axis=-1, keepdims=True)
    rest = jnp.logical_and(in_grp, lane != i1)
    el2 = jnp.where(rest, logits, NEG)
    v2 = jnp.max(el2, axis=-1, keepdims=True)
    i2 = jnp.min(jnp.where(jnp.logical_and(rest, el2 == v2), lane, LANES), axis=-1, keepdims=True)
    ex = jnp.exp(v2 - v1)
    w1 = 1.0 / (1.0 + ex)
    w2 = ex * w1
    e1 = (i1 - N_GROUPS).astype(F32)
    e2 = (i2 - N_GROUPS).astype(F32)
    rt_ref[...] = jnp.where(lane == 0, e1, jnp.where(lane == 1, e2, jnp.where(
        lane == 2, p_g * w1, jnp.where(lane == 3, p_g * w2, 0.0))))


def _post_call(a_list, w_list, x, mod1, mod2, nw2, w_route, b_route, L, tm):
    T = x.shape[0]
    per_b = L // tm
    n_in = len(a_list)
    tok = lambda i: (i, 0)
    const = lambda i: (0, 0)
    modmap = lambda i: (i // per_b, 0, 0, 0)
    in_specs = ([pl.BlockSpec((tm, a.shape[1]), tok) for a in a_list]
                + [pl.BlockSpec(w.shape, const) for w in w_list]
                + [pl.BlockSpec((tm, D_MODEL), tok),
                   pl.BlockSpec((1, 3, 1, D_MODEL), modmap),
                   pl.BlockSpec((1, 3, 1, D_MODEL), modmap),
                   pl.BlockSpec((1, D_MODEL), const),
                   pl.BlockSpec((D_MODEL, LANES), const),
                   pl.BlockSpec((1, LANES), const)])
    return pl.pallas_call(
        functools.partial(_post_kernel, n_in=n_in), grid=(T // tm,),
        in_specs=in_specs,
        out_specs=[pl.BlockSpec((tm, D_MODEL), tok), pl.BlockSpec((tm, D_MODEL), tok),
                   pl.BlockSpec((tm, LANES), tok)],
        out_shape=[_sds((T, D_MODEL), F32), _sds((T, D_MODEL), F32), _sds((T, LANES), F32)],
        compiler_params=_cp(("arbitrary",)), name="post",
    )(*a_list, *w_list, x, mod1, mod2, nw2, w_route, b_route)


def _rank_kernel(rt_ref, rank_ref, cnt_ref, run_sc):
    @pl.when(pl.program_id(0) == 0)
    def _():
        run_sc[...] = jnp.zeros_like(run_sc)

    rt = rt_ref[...]
    tr = rt.shape[0]
    lane = lax.broadcasted_iota(I32, rt.shape, 1)
    o1 = lane == rt[:, 0:1].astype(I32)
    o2 = lane == rt[:, 1:2].astype(I32)
    onehot = jnp.where(o1, 1.0, 0.0) + jnp.where(o2, 1.0, 0.0)
    r = lax.broadcasted_iota(I32, (tr, tr), 0)
    cc = lax.broadcasted_iota(I32, (tr, tr), 1)
    before = jnp.dot(jnp.where(r > cc, 1.0, 0.0).astype(BF16), onehot.astype(BF16),
                     preferred_element_type=F32) + run_sc[...]
    rank1 = jnp.sum(jnp.where(o1, before, 0.0), axis=-1, keepdims=True)
    rank2 = jnp.sum(jnp.where(o2, before, 0.0), axis=-1, keepdims=True)
    rank_ref[...] = jnp.where(lane == 0, rank1, jnp.where(lane == 1, rank2, 0.0))
    run_sc[...] = run_sc[...] + jnp.sum(onehot, axis=0, keepdims=True)
    cnt_ref[...] = run_sc[...]


def _rank_call(route, tr):
    T = route.shape[0]
    return pl.pallas_call(
        _rank_kernel, grid=(T // tr,),
        in_specs=[pl.BlockSpec((tr, LANES), lambda i: (i, 0))],
        out_specs=[pl.BlockSpec((tr, LANES), lambda i: (i, 0)),
                   pl.BlockSpec((1, LANES), lambda i: (0, 0))],
        out_shape=[_sds((T, LANES), F32), _sds((1, LANES), F32)],
        scratch_shapes=[pltpu.VMEM((1, LANES), F32)],
        compiler_params=_cp(("arbitrary",)), name="moe_rank",
    )(route)


def _dest_kernel(rt_ref, rank_ref, ps_ref, d_ref):
    rt = rt_ref[...]
    rank = rank_ref[...]
    ps = ps_ref[...]
    lane = lax.broadcasted_iota(I32, rt.shape, 1)
    d1 = jnp.sum(jnp.where(lane == rt[:, 0:1].astype(I32), ps, 0.0), axis=-1, keepdims=True) + rank[:, 0:1]
    d2 = jnp.sum(jnp.where(lane == rt[:, 1:2].astype(I32), ps, 0.0), axis=-1, keepdims=True) + rank[:, 1:2]
    dd = jnp.where(lane == 0, d1, jnp.where(lane == 1, d2, 0.0))
    d_ref[0] = dd.T[0:8, :].astype(I32)


def _dest_call(route, rank, pstart, tr):
    T = route.shape[0]
    return pl.pallas_call(
        _dest_kernel, grid=(T // tr,),
        in_specs=[pl.BlockSpec((tr, LANES), lambda i: (i, 0)),
                  pl.BlockSpec((tr, LANES), lambda i: (i, 0)),
                  pl.BlockSpec((1, LANES), lambda i: (0, 0))],
        out_specs=pl.BlockSpec((1, 8, tr), lambda i: (i, 0, 0)),
        out_shape=_sds((T // tr, 8, tr), I32),
        compiler_params=_cp(("arbitrary",)), name="moe_dest",
    )(route, rank, pstart)


def _row_copy(src, s, dst, d, sem):
    return pltpu.make_async_copy(src.at[pl.ds(s, 1), :], dst.at[pl.ds(d, 1), :], sem)


def _disp_kernel(h_ref, dest_hbm, xg_in, xg_hbm, d_sm, sem_i, sem_o, *, tm):
    del xg_in
    i = pl.program_id(0)
    idx_cp = pltpu.make_async_copy(dest_hbm.at[i], d_sm, sem_i)
    idx_cp.start()
    idx_cp.wait()

    def issue(t, carry):
        for kk in range(2):
            _row_copy(h_ref, t, xg_hbm, d_sm[kk, t], sem_o).start()
        return carry

    lax.fori_loop(0, tm, issue, 0, unroll=8)
    for kk in range(2):
        pltpu.make_async_copy(h_ref, xg_hbm.at[pl.ds(0, tm), :], sem_o).wait()


def _disp_call(h2, dest, xg0, tm):
    T = h2.shape[0]
    return pl.pallas_call(
        functools.partial(_disp_kernel, tm=tm), grid=(T // tm,),
        in_specs=[pl.BlockSpec((tm, D_MODEL), lambda i: (i, 0)),
                  pl.BlockSpec(memory_space=pl.ANY),
                  pl.BlockSpec(memory_space=pl.ANY)],
        out_specs=pl.BlockSpec(memory_space=pl.ANY),
        out_shape=_sds(xg0.shape, F32),
        scratch_shapes=[pltpu.SMEM((8, tm), I32), pltpu.SemaphoreType.DMA, pltpu.SemaphoreType.DMA],
        input_output_aliases={2: 0},
        compiler_params=_cp(("arbitrary",)), name="moe_dispatch",
    )(h2, dest, xg0)


def _expert_kernel(be_ref, nu_ref, x_ref, w1_ref, w3_ref, w2_ref, y_ref, w1b, w3b, w2b):
    i = pl.program_id(0)
    used = i < nu_ref[0]
    fresh = jnp.logical_or(i == 0, be_ref[i] != be_ref[jnp.maximum(i - 1, 0)])

    @pl.when(jnp.logical_and(used, fresh))
    def _():
        w1b[...] = w1_ref[0, 0].astype(BF16)
        w3b[...] = w3_ref[0, 0].astype(BF16)
        w2b[...] = w2_ref[0, 0].astype(BF16)

    @pl.when(used)
    def _():
        x = x_ref[...].astype(BF16)
        a = jnp.dot(x, w1b[...], preferred_element_type=F32)
        g = jnp.dot(x, w3b[...], preferred_element_type=F32)
        hm = (a * _sigmoid(a) * g).astype(BF16)
        y_ref[...] = jnp.dot(hm, w2b[...], preferred_element_type=F32)

    @pl.when(jnp.logical_not(used))
    def _():
        y_ref[...] = jnp.zeros(y_ref.shape, F32)


def _expert_call(blk_e, n_used, xg, w1, w3, w2, layer):
    P = xg.shape[0]
    nb = P // MOE_ROWS
    rowmap = lambda i, be, nu: (jnp.minimum(i, nu[0] - 1), 0)
    wmap = lambda i, be, nu: (layer, be[i], 0, 0)
    return pl.pallas_call(
        _expert_kernel,
        grid_spec=pltpu.PrefetchScalarGridSpec(
            num_scalar_prefetch=2, grid=(nb,),
            in_specs=[pl.BlockSpec((MOE_ROWS, D_MODEL), rowmap),
                      pl.BlockSpec((1, 1, D_MODEL, D_EXPERT), wmap),
                      pl.BlockSpec((1, 1, D_MODEL, D_EXPERT), wmap),
                      pl.BlockSpec((1, 1, D_EXPERT, D_MODEL), wmap)],
            out_specs=pl.BlockSpec((MOE_ROWS, D_MODEL), lambda i, be, nu: (i, 0)),
            scratch_shapes=[pltpu.VMEM((D_MODEL, D_EXPERT), BF16),
                            pltpu.VMEM((D_MODEL, D_EXPERT), BF16),
                            pltpu.VMEM((D_EXPERT, D_MODEL), BF16)]),
        out_shape=_sds((P, D_MODEL), F32),
        compiler_params=_cp(("arbitrary",)), name="moe_experts",
    )(blk_e, n_used, xg, w1, w3, w2)


def _comb_kernel(x_ref, rt_ref, mod_ref, fnw_ref, dest_hbm, y_hbm, o_ref, d_sm, buf, sem_i, sem_g,
                 *, tm, final):
    i = pl.program_id(0)
    idx_cp = pltpu.make_async_copy(dest_hbm.at[i], d_sm, sem_i)
    idx_cp.start()
    idx_cp.wait()

    def issue(t, carry):
        for kk in range(2):
            _row_copy(y_hbm, d_sm[kk, t], buf.at[kk], t, sem_g).start()
        return carry

    lax.fori_loop(0, tm, issue, 0, unroll=8)
    for kk in range(2):
        pltpu.make_async_copy(y_hbm.at[pl.ds(0, tm), :], buf.at[kk], sem_g).wait()
    rt = rt_ref[...]
    y = rt[:, 2:3] * buf[0] + rt[:, 3:4] * buf[1]
    out = x_ref[...] + mod_ref[0, 2] * y
    if final:
        out = _rms(out) * fnw_ref[...]
    o_ref[...] = out


def _comb_call(xn, route, mod, fnw, dest, yb, L, tm, final):
    T = xn.shape[0]
    per_b = L // tm
    return pl.pallas_call(
        functools.partial(_comb_kernel, tm=tm, final=final), grid=(T // tm,),
        in_specs=[pl.BlockSpec((tm, D_MODEL), lambda i: (i, 0)),
                  pl.BlockSpec((tm, LANES), lambda i: (i, 0)),
                  pl.BlockSpec((1, 3, 1, D_MODEL), lambda i: (i // per_b, 0, 0, 0)),
                  pl.BlockSpec((1, D_MODEL), lambda i: (0, 0)),
                  pl.BlockSpec(memory_space=pl.ANY),
                  pl.BlockSpec(memory_space=pl.ANY)],
        out_specs=pl.BlockSpec((tm, D_MODEL), lambda i: (i, 0)),
        out_shape=_sds((T, D_MODEL), F32),
        scratch_shapes=[pltpu.SMEM((8, tm), I32), pltpu.VMEM((2, tm, D_MODEL), F32),
                        pltpu.SemaphoreType.DMA, pltpu.SemaphoreType.DMA],
        compiler_params=_cp(("arbitrary",)), name="moe_combine",
    )(xn, route, mod, fnw, dest, yb)


def _moe(xn, h2, route, mod, fnw, w1, w3, w2, layer, L, final):
    T = xn.shape[0]
    tr = min(512, T)
    rank, counts = _rank_call(route, tr)
    cnt = counts[0, :N_EXPERTS].astype(I32)
    padded = (cnt + MOE_ROWS - 1) // MOE_ROWS * MOE_ROWS
    pend = jnp.cumsum(padded)
    pstart = jnp.zeros((1, LANES), F32).at[0, :N_EXPERTS].set((pend - padded).astype(F32))
    nb = -(-(2 * T + N_EXPERTS * (MOE_ROWS - 1)) // MOE_ROWS)
    blk_e = jnp.clip(jnp.searchsorted(pend, jnp.arange(nb, dtype=I32) * MOE_ROWS, side='right'),
                     0, N_EXPERTS - 1).astype(I32)
    n_used = (pend[N_EXPERTS - 1:] // MOE_ROWS).astype(I32)
    td = min(256, T)
    tc = min(td, L)
    dest_d = _dest_call(route, rank, pstart, td)
    dest_c = dest_d if tc == td else _dest_call(route, rank, pstart, tc)
    xg = _disp_call(h2, dest_d, jnp.zeros((nb * MOE_ROWS, D_MODEL), F32), td)
    yb = _expert_call(blk_e, n_used, xg, w1, w3, w2, layer)
    return _comb_call(xn, route, mod, fnw, dest_c, yb, L, tc, final)


def _in1_kernel(x_ref, mod_ref, nw_ref, win_ref, qnw_ref, wuq_ref, kvnw_ref, tq_ref, tk_ref,
                q_ref, ckv_ref, kr_ref):
    h = _norm_mod(x_ref[...], nw_ref[...], mod_ref[0, 0], mod_ref[0, 1])
    zz = _dot_bf(h, win_ref[...])
    cqn = _rms(zz[:, :MLA_Q_RANK]) * qnw_ref[...]
    q = _dot_bf(cqn, wuq_ref[...])
    tq = tq_ref[...]
    for hh in range(MLA_H):
        q_ref[:, hh * LANES:(hh + 1) * LANES] = (q[:, hh * LANES:(hh + 1) * LANES] * tq).astype(BF16)
    c0 = MLA_Q_RANK
    ckv_ref[...] = _rms(zz[:, c0:c0 + MLA_KV_RANK]) * kvnw_ref[...]
    c1 = c0 + MLA_KV_RANK
    tk = tk_ref[...]
    kr_ref[...] = (zz[:, c1:c1 + MLA_ROPE] * tk[:, :MLA_ROPE]
                   + zz[:, c1 + LANES:c1 + LANES + MLA_ROPE] * tk[:, MLA_ROPE:])


def _in1_call(x, mod, nw, w_in_r, q_nw, w_uq_r, kv_nw, tab_q, tab_k, L, tm):
    T = x.shape[0]
    per_b = L // tm
    tok = lambda i: (i, 0)
    const = lambda i: (0, 0)
    pos = lambda i: (i % per_b, 0)
    return pl.pallas_call(
        _in1_kernel, grid=(T // tm,),
        in_specs=[pl.BlockSpec((tm, D_MODEL), tok),
                  pl.BlockSpec((1, 3, 1, D_MODEL), lambda i: (i // per_b, 0, 0, 0)),
                  pl.BlockSpec((1, D_MODEL), const),
                  pl.BlockSpec((D_MODEL, MLA_IN_R), const),
                  pl.BlockSpec((1, MLA_Q_RANK), const),
                  pl.BlockSpec((MLA_Q_RANK, MLA_H * LANES), const),
                  pl.BlockSpec((1, MLA_KV_RANK), const),
                  pl.BlockSpec((tm, LANES), pos),
                  pl.BlockSpec((tm, 2 * MLA_ROPE), pos)],
        out_specs=[pl.BlockSpec((tm, MLA_H * LANES), tok),
                   pl.BlockSpec((tm, MLA_KV_RANK), tok),
                   pl.BlockSpec((tm, MLA_ROPE), tok)],
        out_shape=[_sds((T, MLA_H * LANES), BF16), _sds((T, MLA_KV_RANK), F32),
                   _sds((T, MLA_ROPE), F32)],
        compiler_params=_cp(("arbitrary",)), name="in1",
    )(x, mod, nw, w_in_r, q_nw, w_uq_r, kv_nw, tab_q, tab_k)


def _kv_kernel(ckv_ref, kr_ref, wk_ref, pe_ref, wv_ref, one_ref, k_ref, v_ref):
    c = ckv_ref[...].astype(BF16)
    k_ref[...] = (jnp.dot(c, wk_ref[...], preferred_element_type=F32)
                  + jnp.dot(kr_ref[...].astype(BF16), pe_ref[...], preferred_element_type=F32)).astype(BF16)
    v_ref[...] = (jnp.dot(c, wv_ref[...], preferred_element_type=F32) + one_ref[...]).astype(BF16)


def _kv_call(ckv, kr, wk_r, place, wv_r, ones_row, tm):
    T = ckv.shape[0]
    tok = lambda i: (i, 0)
    const = lambda i: (0, 0)
    wide = MLA_H * LANES
    return pl.pallas_call(
        _kv_kernel, grid=(T // tm,),
        in_specs=[pl.BlockSpec((tm, MLA_KV_RANK), tok),
                  pl.BlockSpec((tm, MLA_ROPE), tok),
                  pl.BlockSpec((MLA_KV_RANK, wide), const),
                  pl.BlockSpec((MLA_ROPE, wide), const),
                  pl.BlockSpec((MLA_KV_RANK, wide), const),
                  pl.BlockSpec((1, wide), const)],
        out_specs=[pl.BlockSpec((tm, wide), tok), pl.BlockSpec((tm, wide), tok)],
        out_shape=[_sds((T, wide), BF16), _sds((T, wide), BF16)],
        compiler_params=_cp(("arbitrary",)), name="mla_kv",
    )(ckv, kr, wk_r, place, wv_r, ones_row)


def _attn_kernel(q_ref, k_ref, v_ref, o_ref, m0, m1, a0, a1, s0, s1, *, tq, tk, Lk, pos0):
    m_scs, acc_scs, s_scs = (m0, m1), (a0, a1), (s0, s1)
    q_lo = pos0 + pl.program_id(2) * tq
    k_hi = jnp.minimum(((q_lo + tq - 1) // CHUNK + 1) * CHUNK, Lk)
    n_blk = (k_hi + tk - 1) // tk
    n_full = jnp.minimum(((q_lo // CHUNK + 1) * CHUNK) // tk, n_blk)
    q_chunk = jnp.right_shift(q_lo + lax.broadcasted_iota(I32, (tq, tk), 0), CHUNK_SHIFT)
    k_iota = lax.broadcasted_iota(I32, (tq, tk), 1)

    def scores(hh, k0):
        hc = slice(hh * LANES, (hh + 1) * LANES)
        return lax.dot_general(q_ref[:, hc], k_ref[pl.ds(k0, tk), hc], NT, preferred_element_type=F32)

    for hh in range(2):
        m_scs[hh][...] = jnp.full((tq, LANES), NEG, F32)
        acc_scs[hh][...] = jnp.zeros((tq, LANES), F32)
        s_scs[hh][...] = scores(hh, 0)

    def step(j, carry, masked):
        k0 = pl.multiple_of(j * tk, tk)
        k1 = pl.multiple_of(jnp.minimum(j + 1, n_blk - 1) * tk, tk)
        ahead = [scores(hh, k1) for hh in range(2)]
        for hh in range(2):
            hc = slice(hh * LANES, (hh + 1) * LANES)
            s = s_scs[hh][...]
            if masked:
                s = jnp.where(q_chunk >= jnp.right_shift(k0 + k_iota, CHUNK_SHIFT), s, NEG)
            m_prev = m_scs[hh][...]
            m_new = jnp.maximum(m_prev, jnp.max(s, axis=-1, keepdims=True))
            if tk % LANES == 0:
                p = jnp.exp2(s - jnp.concatenate([m_new] * (tk // LANES), axis=1))
            else:
                p = jnp.exp2(s - m_new[:, 0:1])
            acc_scs[hh][...] = (jnp.exp2(m_prev - m_new) * acc_scs[hh][...]
                                + jnp.dot(p.astype(BF16), v_ref[pl.ds(k0, tk), hc],
                                          preferred_element_type=F32))
            m_scs[hh][...] = m_new
        for hh in range(2):
            s_scs[hh][...] = ahead[hh]
        return carry

    lax.fori_loop(0, n_full, functools.partial(step, masked=False), 0)
    lax.fori_loop(n_full, n_blk, functools.partial(step, masked=True), 0)
    outs = []
    for hh in range(2):
        acc = acc_scs[hh][...]
        outs.append(acc[:, :MLA_V] / acc[:, MLA_V:MLA_V + 1])
    o_ref[...] = jnp.concatenate(outs, axis=1).astype(BF16)


def _attn_call(q, k, v, B, Lq, Lk, pos0, tq, tk):
    nq = Lq // tq
    return pl.pallas_call(
        functools.partial(_attn_kernel, tq=tq, tk=tk, Lk=Lk, pos0=pos0), grid=(B, MLA_H // 2, nq),
        in_specs=[pl.BlockSpec((tq, 2 * LANES), lambda b, hp, i: (b * nq + i, hp)),
                  pl.BlockSpec((Lk, 2 * LANES), lambda b, hp, i: (b, hp)),
                  pl.BlockSpec((Lk, 2 * LANES), lambda b, hp, i: (b, hp))],
        out_specs=pl.BlockSpec((tq, 2 * MLA_V), lambda b, hp, i: (b * nq + i, hp)),
        out_shape=_sds((B * Lq, MLA_H * MLA_V), BF16),
        scratch_shapes=[pltpu.VMEM((tq, LANES), F32)] * 4 + [pltpu.VMEM((tq, tk), F32)] * 2,
        compiler_params=_cp(("arbitrary", "arbitrary", "arbitrary")), name="mla_attn",
    )(q, k, v)


def _split_cols(w, widths):
    offs = [0]
    for n in widths:
        offs.append(offs[-1] + n)
    return [w[:, offs[i]:offs[i + 1]] for i in range(len(widths))]


def _prep_even(ab_w_in, gla_w_alpha, gdn_a_log, gdn_dt_bias):
    gq, gk, gv, glr, gr, qkv, ga, gb, gg = _split_cols(ab_w_in, AB_IN_WIDTHS)
    w_main = jnp.concatenate([gq, gk, gv, gr, qkv, gg], axis=1).astype(BF16)
    small = jnp.concatenate([glr, ga, gb], axis=1)
    w_small = jnp.pad(small, ((0, 0), (0, LANES - small.shape[1])))
    wa_pad = jnp.pad(gla_w_alpha, ((0, LANES - GLA_LR), (0, 0)))
    alog_v = jnp.zeros((1, LANES), F32).at[0, LANE_GA:LANE_GA + GDN_H].set(gdn_a_log)
    dtb_v = jnp.zeros((1, LANES), F32).at[0, LANE_GA:LANE_GA + GDN_H].set(gdn_dt_bias)
    return w_main, w_small, wa_pad, alog_v, dtb_v


def _swap_halves(w):
    half = w.shape[-1] // 2
    return jnp.concatenate([w[..., half:], w[..., :half]], axis=-1)


def _prep_odd(mla_w_in, mla_w_uq, mla_w_ukv):
    cq, ckv, kr = _split_cols(mla_w_in, (MLA_Q_RANK, MLA_KV_RANK, MLA_ROPE))
    pad = lambda w: jnp.pad(w, ((0, 0), (0, LANES - w.shape[1])))
    w_in_r = jnp.concatenate([cq, ckv, pad(kr), pad(_swap_halves(kr))], axis=1).astype(BF16)
    uq = mla_w_uq.reshape(MLA_Q_RANK, MLA_H, MLA_NOPE + MLA_ROPE)
    uq_rope = uq[..., MLA_NOPE:]
    w_uq_r = jnp.concatenate([uq, _swap_halves(uq_rope)], axis=-1).reshape(
        MLA_Q_RANK, MLA_H * LANES).astype(BF16)
    ukv = mla_w_ukv.reshape(MLA_KV_RANK, MLA_H, MLA_NOPE + MLA_V)
    wk_r = jnp.pad(ukv[..., :MLA_NOPE], ((0, 0), (0, 0), (0, LANES - MLA_NOPE))).reshape(
        MLA_KV_RANK, MLA_H * LANES).astype(BF16)
    wv_r = jnp.pad(ukv[..., MLA_NOPE:], ((0, 0), (0, 0), (0, LANES - MLA_V))).reshape(
        MLA_KV_RANK, MLA_H * LANES).astype(BF16)
    ones_row = jnp.tile((jnp.arange(LANES) == MLA_V).astype(F32), MLA_H)[None, :]
    eye = jnp.eye(MLA_ROPE, dtype=F32)
    place = jnp.concatenate([jnp.zeros((MLA_ROPE, MLA_NOPE), F32), eye, eye], axis=1)
    place = jnp.tile(place, (1, MLA_H)).astype(BF16)
    return w_in_r, w_uq_r, wk_r, wv_r, place, ones_row


def _rope_tables(pos0, L):
    half = MLA_ROPE // 2
    inv = jnp.exp(-math.log(ROPE_THETA) * jnp.arange(half, dtype=F32) / half)
    ang = (pos0 + jnp.arange(L, dtype=I32)).astype(F32)[:, None] * inv[None, :]
    cos, sin = jnp.cos(ang), jnp.sin(ang)
    tab_k = jnp.concatenate([cos, cos, -sin, sin], axis=1)
    scale = (MLA_NOPE + MLA_ROPE) ** -0.5 * math.log2(math.e)
    tab_q = jnp.concatenate([jnp.ones((L, MLA_NOPE), F32), tab_k], axis=1) * scale
    return tab_q, tab_k


def _mod4(mods, lo, hi):
    return [mods[i, lo:hi].reshape(hi - lo, 3, 1, D_MODEL) for i in range(mods.shape[0])]


def _trunk(x3, mods, pos0, gla_s, gdn_s, conv_s, past_ckv, past_kr, p):
    B, L, _ = x3.shape
    T = B * L
    tm = min(512, L)
    x = x3.reshape(T, D_MODEL)
    row = lambda a: a.reshape(1, -1)

    w_main, w_small, wa_pad, alog_v, dtb_v = p['even']
    zm, zs = _in0_call(x, mods[0], row(p['norm_w'][0, 0]), w_main, w_small, L, tm)
    o_gla, gla_t = _gla_call(zm, zs, wa_pad, row(p['gla_b_alpha'][0]), row(p['gla_norm_w'][0]),
                             jnp.swapaxes(gla_s, -1, -2), B, L)
    hist8 = jnp.pad(conv_s, ((0, 0), (8 - (CONV_W - 1), 0), (0, 0)))
    o_gdn, gdn_new = _gdn_call(zm, zs, p['gdn_conv_w'][0], alog_v, dtb_v, row(p['gdn_norm_w'][0]),
                               hist8, gdn_s, B, L)
    qkv0 = 3 * GLA_H * GLA_DV
    conv_new = zm.reshape(B, L, AB_MAIN)[:, L - (CONV_W - 1):, qkv0:qkv0 + GDN_CONV_DIM].astype(F32)
    w_out = p['ab_w_out_bf']
    half = GLA_H * GLA_DV
    xn, h2, route = _post_call([o_gla, o_gdn], [w_out[:half], w_out[half:]], x, mods[0], mods[1],
                               row(p['norm_w'][0, 1]), p['w_route'][0], p['b_route'][0], L, tm)
    x = _moe(xn, h2, route, mods[1], row(p['final_norm_w']), p['moe_w1'], p['moe_w3'], p['moe_w2'],
             0, L, final=False)

    w_in_r, w_uq_r, wk_r, wv_r, place, ones_row = p['odd']
    tab_q, tab_k = _rope_tables(pos0, L)
    q, ckv, kr = _in1_call(x, mods[2], row(p['norm_w'][1, 0]), w_in_r, row(p['mla_q_norm_w'][0]),
                           w_uq_r, row(p['mla_kv_norm_w'][0]), tab_q, tab_k, L, tm)
    if past_ckv is None:
        ckv_all, kr_all, Lk = ckv, kr, L
    else:
        Lk = past_ckv.shape[1] + L
        ckv_all = jnp.concatenate([past_ckv, ckv.reshape(B, L, -1)], axis=1).reshape(B * Lk, -1)
        kr_all = jnp.concatenate([past_kr, kr.reshape(B, L, -1)], axis=1).reshape(B * Lk, -1)
    tkv = 512 if (B * Lk) % 512 == 0 else Lk
    k_all, v_all = _kv_call(ckv_all, kr_all, wk_r, place, wv_r, ones_row, tkv)
    tq = min(ATTN_TILE, L)
    tk = ATTN_TILE if Lk % ATTN_TILE == 0 else Lk
    att = _attn_call(q, k_all, v_all, B, L, Lk, pos0, tq, tk)
    xn, h2, route = _post_call([att], [p['mla_w_out_bf']], x, mods[2], mods[3],
                               row(p['norm_w'][1, 1]), p['w_route'][1], p['b_route'][1], L, tm)
    y = _moe(xn, h2, route, mods[3], row(p['final_norm_w']), p['moe_w1'], p['moe_w3'], p['moe_w2'],
             1, L, final=True)
    return (y.reshape(B, L, D_MODEL), jnp.swapaxes(gla_t, -1, -2)[None], gdn_new[None], conv_new[None],
            ckv.reshape(1, B, L, MLA_KV_RANK), kr.reshape(1, B, L, MLA_ROPE))


def kernel(x_prompt, x_sample, state_gla, state_gdn, state_gdn_conv, cache_mla_ckv, cache_mla_krope,
           c_prompt, c_sample, ada_w, ada_b, norm_w, final_norm_w, ab_w_in, gla_w_alpha, gla_b_alpha,
           gla_norm_w, gdn_conv_w, gdn_a_log, gdn_dt_bias, gdn_norm_w, ab_w_out, mla_w_in, mla_q_norm_w,
           mla_w_uq, mla_kv_norm_w, mla_w_ukv, mla_w_out, moe_w_group, moe_b_group, moe_w_expert,
           moe_b_expert, moe_w1, moe_w3, moe_w2):
    depth = ada_w.shape[0]
    bp, bs = x_prompt.shape[0], x_sample.shape[0]
    mods = _ada_call(jnp.concatenate([c_prompt, c_sample], axis=0),
                     ada_w.reshape(2 * depth, D_MODEL, 3 * D_MODEL), ada_b.reshape(2 * depth, 1, 3 * D_MODEL))
    w_route = jnp.pad(jnp.concatenate([moe_w_group, moe_w_expert], axis=-1),
                      ((0, 0), (0, 0), (0, LANES - N_GROUPS - N_EXPERTS)))
    b_route = jnp.pad(jnp.concatenate([moe_b_group, moe_b_expert], axis=-1),
                      ((0, 0), (0, LANES - N_GROUPS - N_EXPERTS)))[:, None, :]
    p = dict(norm_w=norm_w, final_norm_w=final_norm_w, gla_b_alpha=gla_b_alpha, gla_norm_w=gla_norm_w,
             gdn_conv_w=gdn_conv_w, gdn_norm_w=gdn_norm_w, mla_q_norm_w=mla_q_norm_w,
             mla_kv_norm_w=mla_kv_norm_w, moe_w1=moe_w1, moe_w3=moe_w3, moe_w2=moe_w2,
             w_route=w_route, b_route=b_route,
             even=_prep_even(ab_w_in[0], gla_w_alpha[0], gdn_a_log[0], gdn_dt_bias[0]),
             odd=_prep_odd(mla_w_in[0], mla_w_uq[0], mla_w_ukv[0]),
             ab_w_out_bf=ab_w_out[0].astype(BF16), mla_w_out_bf=mla_w_out[0].astype(BF16))
    y_p, gla_p, gdn_p, conv_p, ckv_p, kr_p = _trunk(
        x_prompt, _mod4(mods, 0, bp), 0,
        jnp.zeros((bp, GLA_H, GLA_DK, GLA_DV), F32), jnp.zeros((bp, GDN_H, GDN_DK, GDN_DV), F32),
        jnp.zeros((bp, CONV_W - 1, GDN_CONV_DIM), F32), None, None, p)
    y_s, gla_s, gdn_s, conv_s, ckv_s, kr_s = _trunk(
        x_sample, _mod4(mods, bp, bp + bs), cache_mla_ckv.shape[2],
        state_gla[0], state_gdn[0], state_gdn_conv[0], cache_mla_ckv[0], cache_mla_krope[0], p)
    return (y_p, y_s, gla_p, gdn_p, conv_p, ckv_p, kr_p, gla_s, gdn_s, conv_s, ckv_s, kr_s)
```

```python
import functools
import math

import jax
import jax.numpy as jnp
from jax import lax
from jax.experimental import pallas as pl
from jax.experimental.pallas import tpu as pltpu

F32 = jnp.float32
BF16 = jnp.bfloat16
I32 = jnp.int32
HI = lax.Precision.HIGHEST

D_MODEL = 1024
CHUNK = 64
GLA_H, GLA_DK, GLA_DV, GLA_LR, GLA_TAU = 4, 64, 128, 16, 16.0
GDN_H, GDN_DK, GDN_DV, CONV_W = 4, 128, 128, 4
GDN_CONV_DIM = GDN_H * (2 * GDN_DK + GDN_DV)
AB_IN_WIDTHS = (GLA_H * GLA_DK, GLA_H * GLA_DK, GLA_H * GLA_DV, GLA_LR, GLA_H * GLA_DV,
                GDN_CONV_DIM, GDN_H, GDN_H, GDN_H * GDN_DV)
AB_MAIN = 3584
MLA_H, MLA_NOPE, MLA_ROPE, MLA_V = 16, 64, 32, 64
MLA_Q_RANK, MLA_KV_RANK = 384, 256
MLA_IN_R = MLA_Q_RANK + MLA_KV_RANK + 256
ROPE_THETA = 10000.0
N_GROUPS, EXPERTS_PER_GROUP, N_EXPERTS, D_EXPERT = 4, 8, 32, 512
GROUP_SHIFT = int(math.log2(EXPERTS_PER_GROUP))
CHUNK_SHIFT = int(math.log2(CHUNK))
EPS = 1e-6

LANES = 128
GLA_SUB = 16
SCAN_CHUNKS = 4
GLA_SAFE_LOG_DECAY = -60.0
MOE_ROWS = 256
MOE_TILE = 512
MOE_ALIGN = 8
MOE_BIG = 4
MOE_XCOLS = D_MODEL + LANES
ATTN_TILE = 512
NEG = -3.0e38
VMEM_LIMIT = 56 * 1024 * 1024

LANE_GA = GLA_LR
LANE_GB = GLA_LR + GDN_H

NT = (((1,), (1,)), ((), ()))
TN = (((0,), (0,)), ((), ()))


def _cp(sem, vmem=None):
    return pltpu.CompilerParams(dimension_semantics=sem, vmem_limit_bytes=vmem or VMEM_LIMIT)


def _sds(shape, dtype):
    return jax.ShapeDtypeStruct(shape, dtype)


def _sigmoid(x):
    return 1.0 / (1.0 + jnp.exp(-x))


def _softplus(x):
    return jnp.maximum(x, 0.0) + jnp.log(1.0 + jnp.exp(-jnp.abs(x)))


def _rms(x):
    return x * lax.rsqrt(jnp.mean(x * x, axis=-1, keepdims=True) + EPS)


def _norm_mod(x, nw, shift, scale):
    return (_rms(x) * nw) * (1.0 + scale) + shift


def _dot_bf(a, b):
    return jnp.dot(a.astype(BF16), b.astype(BF16), preferred_element_type=F32)


def _dot_hi(a, b):
    return jnp.dot(a, b, precision=HI, preferred_element_type=F32)


def _ada_kernel(c_ref, w_ref, b_ref, o_ref):
    c = c_ref[...]
    o_ref[0] = _dot_hi(c * _sigmoid(c), w_ref[0]) + b_ref[0]


def _ada_call(c_all, ada_w, ada_b):
    n = ada_w.shape[0]
    nb = c_all.shape[0]
    return pl.pallas_call(
        _ada_kernel, grid=(n, 3),
        in_specs=[pl.BlockSpec((nb, D_MODEL), lambda i, j: (0, 0)),
                  pl.BlockSpec((1, D_MODEL, D_MODEL), lambda i, j: (i, 0, j)),
                  pl.BlockSpec((1, 1, D_MODEL), lambda i, j: (i, 0, j))],
        out_specs=pl.BlockSpec((1, nb, D_MODEL), lambda i, j: (i, 0, j)),
        out_shape=_sds((n, nb, 3 * D_MODEL), F32),
        compiler_params=_cp(("arbitrary", "arbitrary")), name="ada",
    )(c_all, ada_w, ada_b)


def _in0_kernel(x_ref, mod_ref, nw_ref, wm_ref, ws_ref, zm_ref, zs_ref):
    h = _norm_mod(x_ref[...], nw_ref[...], mod_ref[0, 0], mod_ref[0, 1])
    h_hl = _split_bf(h)
    zm_ref[...] = jnp.dot(h_hl[0], wm_ref[...], preferred_element_type=F32).astype(BF16)
    zs_ref[...] = _dot3(h_hl, (ws_ref[0], ws_ref[1]))


def _in0_call(x, mod, nw, w_main, w_small, L, tm):
    T = x.shape[0]
    per_b = L // tm
    return pl.pallas_call(
        _in0_kernel, grid=(T // tm,),
        in_specs=[pl.BlockSpec((tm, D_MODEL), lambda i: (i, 0)),
                  pl.BlockSpec((1, 3, 1, D_MODEL), lambda i: (i // per_b, 0, 0, 0)),
                  pl.BlockSpec((1, D_MODEL), lambda i: (0, 0)),
                  pl.BlockSpec((D_MODEL, AB_MAIN), lambda i: (0, 0)),
                  pl.BlockSpec((2, D_MODEL, LANES), lambda i: (0, 0, 0))],
        out_specs=[pl.BlockSpec((tm, AB_MAIN), lambda i: (i, 0)),
                   pl.BlockSpec((tm, LANES), lambda i: (i, 0))],
        out_shape=[_sds((T, AB_MAIN), BF16), _sds((T, LANES), F32)],
        compiler_params=_cp(("arbitrary",)), name="in0",
    )(x, mod, nw, w_main, w_small)


def _gated_head_out(o, nw, gate):
    gate = gate.astype(F32)
    return (_rms(o) * nw * (gate * _sigmoid(gate))).astype(BF16)


def _gla_chunk_exact(q, k, v, b, st):
    CL = q.shape[0]
    row = lax.broadcasted_iota(I32, (GLA_SUB, 1), 0)
    bprev = jnp.zeros((1, GLA_DK), F32)
    outs = []
    for blk in range(CL // GLA_SUB):
        sl = slice(blk * GLA_SUB, (blk + 1) * GLA_SUB)
        qi, ki, vi = q[sl], k[sl], v[sl]
        brel = b[sl] - bprev
        bend = brel[GLA_SUB - 1:GLA_SUB]
        oi = lax.dot_general((qi * jnp.exp(brel)).astype(BF16), st.astype(BF16), NT,
                             preferred_element_type=F32)
        for j in range(GLA_SUB):
            d = brel - brel[j:j + 1]
            e = jnp.where(row >= j, jnp.exp(jnp.minimum(d, 0.0)), 0.0)
            a = jnp.sum(qi * ki[j:j + 1] * e, axis=-1, keepdims=True)
            oi = oi + a * vi[j:j + 1]
        outs.append(oi)
        khat = ki * jnp.exp(bend - brel)
        st = st * jnp.exp(bend) + lax.dot_general(vi.astype(BF16), khat.astype(BF16), TN,
                                                  preferred_element_type=F32)
        bprev = b[(blk + 1) * GLA_SUB - 1:(blk + 1) * GLA_SUB]
    return jnp.concatenate(outs, axis=0), st


def _gla_chunk_fast(q, k, v, b, st, causal):
    CL = q.shape[0]
    qh = (q * jnp.exp(b)).astype(BF16)
    kh = (k * jnp.exp(-b)).astype(BF16)
    vb = v.astype(BF16)
    att = jnp.where(causal, lax.dot_general(qh, kh, NT, preferred_element_type=F32), 0.0)
    o = (lax.dot_general(qh, st.astype(BF16), NT, preferred_element_type=F32)
         + jnp.dot(att.astype(BF16), vb, preferred_element_type=F32))
    blast = b[CL - 1:CL]
    kdec = (k * jnp.exp(blast - b)).astype(BF16)
    st = st * jnp.exp(blast) + lax.dot_general(vb, kdec, TN, preferred_element_type=F32)
    return o, st


def _gla_kernel(q_ref, k_ref, v_ref, gr_ref, zs_ref, wa_ref, ba_ref, nw_ref, s0_ref,
                o_ref, st_ref, st_sc, b_sc, *, CL, nch, nsteps):
    step = pl.program_id(1)

    @pl.when(step == 0)
    def _():
        st_sc[...] = s0_ref[0]

    r = lax.broadcasted_iota(I32, (CL, CL), 0)
    cc = lax.broadcasted_iota(I32, (CL, CL), 1)
    causal = r >= cc
    tri = causal.astype(F32)
    b_min = None
    for ch in range(nch):
        rows = slice(ch * CL, (ch + 1) * CL)
        pre = _dot_hi(zs_ref[rows, :], wa_ref[...]) + ba_ref[...]
        la = -_softplus(-pre) * (1.0 / GLA_TAU)
        b_ch = _dot_hi(tri, la)
        b_sc[rows, :] = b_ch
        lo = jnp.min(b_ch[CL - 1:CL, :])
        b_min = lo if b_min is None else jnp.minimum(b_min, lo)
    safe = b_min > GLA_SAFE_LOG_DECAY

    def run(chunk_fn):
        nw = nw_ref[...]
        for h in range(GLA_H):
            kc = slice(h * GLA_DK, (h + 1) * GLA_DK)
            vc = slice(h * GLA_DV, (h + 1) * GLA_DV)
            st = st_sc[h]
            for ch in range(nch):
                rows = slice(ch * CL, (ch + 1) * CL)
                q = q_ref[rows, kc].astype(F32) * GLA_DK ** -0.5
                o, st = chunk_fn(q, k_ref[rows, kc].astype(F32), v_ref[rows, vc].astype(F32),
                                 b_sc[rows, kc], st)
                o_ref[rows, vc] = _gated_head_out(o, nw, gr_ref[rows, vc])
            st_sc[h] = st

    @pl.when(safe)
    def _():
        run(functools.partial(_gla_chunk_fast, causal=causal))

    @pl.when(jnp.logical_not(safe))
    def _():
        run(_gla_chunk_exact)

    @pl.when(step == nsteps - 1)
    def _():
        st_ref[0] = st_sc[...]


def _gla_call(zm, zs, wa_pad, b_alpha, nw, s0t, B, L):
    CL = min(CHUNK, L)
    nch = SCAN_CHUNKS if L % (CL * SCAN_CHUNKS) == 0 else 1
    rows = CL * nch
    nsteps = L // rows
    T = B * L
    qk_w = GLA_H * GLA_DK
    v_w = GLA_H * GLA_DV
    row = lambda b, c: b * nsteps + c
    return pl.pallas_call(
        functools.partial(_gla_kernel, CL=CL, nch=nch, nsteps=nsteps), grid=(B, nsteps),
        in_specs=[pl.BlockSpec((rows, qk_w), lambda b, c: (row(b, c), 0)),
                  pl.BlockSpec((rows, qk_w), lambda b, c: (row(b, c), 1)),
                  pl.BlockSpec((rows, v_w), lambda b, c: (row(b, c), 1)),
                  pl.BlockSpec((rows, v_w), lambda b, c: (row(b, c), 2)),
                  pl.BlockSpec((rows, LANES), lambda b, c: (row(b, c), 0)),
                  pl.BlockSpec((LANES, qk_w), lambda b, c: (0, 0)),
                  pl.BlockSpec((1, qk_w), lambda b, c: (0, 0)),
                  pl.BlockSpec((1, GLA_DV), lambda b, c: (0, 0)),
                  pl.BlockSpec((1, GLA_H, GLA_DV, GLA_DK), lambda b, c: (b, 0, 0, 0))],
        out_specs=[pl.BlockSpec((rows, v_w), lambda b, c: (row(b, c), 0)),
                   pl.BlockSpec((1, GLA_H, GLA_DV, GLA_DK), lambda b, c: (b, 0, 0, 0))],
        out_shape=[_sds((T, v_w), BF16), _sds((B, GLA_H, GLA_DV, GLA_DK), F32)],
        scratch_shapes=[pltpu.VMEM((GLA_H, GLA_DV, GLA_DK), F32), pltpu.VMEM((rows, qk_w), F32)],
        compiler_params=_cp(("arbitrary", "arbitrary")), name="gla",
    )(zm, zm, zm, zm, zs, wa_pad, b_alpha, nw, s0t)


def _split_bf(a):
    hi = a.astype(BF16)
    return hi, (a - hi.astype(F32)).astype(BF16)


def _dot3(a_hl, b_hl):
    (ah, al), (bh, bl) = a_hl, b_hl
    d = functools.partial(jnp.dot, preferred_element_type=F32)
    return d(ah, bh) + d(ah, bl) + d(al, bh)


def _gdn_kernel(qkv_ref, zs_ref, gg_ref, cw_ref, alog_ref, dtb_ref, nw_ref, hist_ref, s0_ref,
                o_ref, s_ref, cv_sc, st_sc, *, CL, nch, nsteps):
    step = pl.program_id(1)
    R = CL * nch

    @pl.when(step == 0)
    def _():
        cv_sc[0:8, :] = hist_ref[0]
        st_sc[...] = s0_ref[0]

    x = qkv_ref[...].astype(F32)
    cv_sc[8:8 + R, :] = x
    cw = cw_ref[...]
    conv = (cv_sc[5:5 + R, :] * cw[0:1] + cv_sc[6:6 + R, :] * cw[1:2]
            + cv_sc[7:7 + R, :] * cw[2:3] + x * cw[3:4])
    cv_sc[0:8, :] = x[R - 8:R]
    conv = conv * _sigmoid(conv)

    zs = zs_ref[...]
    g_all = -jnp.exp(alog_ref[...]) * _softplus(zs + dtb_ref[...])
    beta_all = _sigmoid(zs)
    r = lax.broadcasted_iota(I32, (CL, CL), 0)
    cc = lax.broadcasted_iota(I32, (CL, CL), 1)
    tri = (r >= cc).astype(F32)
    eye = (r == cc).astype(F32)
    nw = nw_ref[...]
    qk_w = GDN_H * GDN_DK

    units = [(ch, h) for ch in range(nch) for h in range(GDN_H)]
    stage = {}
    for ch in range(nch):
        rows = slice(ch * CL, (ch + 1) * CL)
        gam = _dot_hi(tri, g_all[rows])
        gam_t = gam.T
        for h in range(GDN_H):
            q = conv[rows, h * GDN_DK:(h + 1) * GDN_DK]
            k = conv[rows, qk_w + h * GDN_DK:qk_w + (h + 1) * GDN_DK]
            v = conv[rows, 2 * qk_w + h * GDN_DV:2 * qk_w + (h + 1) * GDN_DV]
            q = q * lax.rsqrt(jnp.sum(q * q, axis=-1, keepdims=True) + EPS) * GDN_DK ** -0.5
            k = k * lax.rsqrt(jnp.sum(k * k, axis=-1, keepdims=True) + EPS)
            gcol = gam[:, LANE_GA + h:LANE_GA + h + 1]
            grow = gam_t[LANE_GA + h:LANE_GA + h + 1, :]
            bcol = beta_all[rows, LANE_GB + h:LANE_GB + h + 1]
            dec = jnp.where(r >= cc, jnp.exp(jnp.minimum(gcol - grow, 0.0)), 0.0)
            eg = jnp.exp(gcol)
            glast = gcol[CL - 1:CL]
            stage[ch, h] = dict(
                kb=k.astype(BF16), qb=q.astype(BF16), dec=dec, bcol=bcol,
                ub=(v * bcol).astype(BF16), wb=(k * (bcol * eg)).astype(BF16),
                qe=(q * eg).astype(BF16), kdec=(k * jnp.exp(glast - gcol)).astype(BF16),
                elast=jnp.exp(glast))
    for un in units:
        d = stage[un]
        kk = lax.dot_general(d['kb'], d['kb'], NT, preferred_element_type=F32)
        d['pw'] = jnp.where(r > cc, d['bcol'] * kk * d['dec'], 0.0)
        d['t'] = eye - d['pw']
    for _ in range(int(math.log2(CL)) - 1):
        for un in units:
            d = stage[un]
            hl = _split_bf(d['pw'])
            d['pw'] = _dot3(hl, hl)
        for un in units:
            d = stage[un]
            d['t'] = d['t'] + _dot3(_split_bf(d['t']), _split_bf(d['pw']))
    pre = {}
    for un in units:
        d = stage[un]
        tb = d['t'].astype(BF16)
        u = jnp.dot(tb, d['ub'], preferred_element_type=F32)
        w = jnp.dot(tb, d['wb'], preferred_element_type=F32).astype(BF16)
        qk = (lax.dot_general(d['qb'], d['kb'], NT, preferred_element_type=F32) * d['dec']).astype(BF16)
        pre[un] = (u, w, qk, d['qe'], d['kdec'], d['elast'])

    for h in range(GDN_H):
        vc = slice(h * GDN_DV, (h + 1) * GDN_DV)
        st = st_sc[h]
        for ch in range(nch):
            rows = slice(ch * CL, (ch + 1) * CL)
            u, w, qk, qe, kdec, elast = pre[ch, h]
            stb = st.astype(BF16)
            delta = u - jnp.dot(w, stb, preferred_element_type=F32)
            db = delta.astype(BF16)
            o = (jnp.dot(qe, stb, preferred_element_type=F32)
                 + jnp.dot(qk, db, preferred_element_type=F32))
            st = st * elast + lax.dot_general(kdec, db, TN, preferred_element_type=F32)
            o_ref[rows, vc] = _gated_head_out(o, nw, gg_ref[rows, vc])
        st_sc[h] = st

    @pl.when(step == nsteps - 1)
    def _():
        s_ref[0] = st_sc[...]


def _gdn_call(zm, zs, conv_w, alog_v, dtb_v, nw, hist8, s0, B, L):
    CL = min(CHUNK, L)
    nch = SCAN_CHUNKS if L % (CL * SCAN_CHUNKS) == 0 else 1
    rows = CL * nch
    nsteps = L // rows
    T = B * L
    v_w = GDN_H * GDN_DV
    row = lambda b, c: b * nsteps + c
    return pl.pallas_call(
        functools.partial(_gdn_kernel, CL=CL, nch=nch, nsteps=nsteps), grid=(B, nsteps),
        in_specs=[pl.BlockSpec((rows, GDN_CONV_DIM), lambda b, c: (row(b, c), 1)),
                  pl.BlockSpec((rows, LANES), lambda b, c: (row(b, c), 0)),
                  pl.BlockSpec((rows, v_w), lambda b, c: (row(b, c), 6)),
                  pl.BlockSpec((CONV_W, GDN_CONV_DIM), lambda b, c: (0, 0)),
                  pl.BlockSpec((1, LANES), lambda b, c: (0, 0)),
                  pl.BlockSpec((1, LANES), lambda b, c: (0, 0)),
                  pl.BlockSpec((1, GDN_DV), lambda b, c: (0, 0)),
                  pl.BlockSpec((1, 8, GDN_CONV_DIM), lambda b, c: (b, 0, 0)),
                  pl.BlockSpec((1, GDN_H, GDN_DK, GDN_DV), lambda b, c: (b, 0, 0, 0))],
        out_specs=[pl.BlockSpec((rows, v_w), lambda b, c: (row(b, c), 0)),
                   pl.BlockSpec((1, GDN_H, GDN_DK, GDN_DV), lambda b, c: (b, 0, 0, 0))],
        out_shape=[_sds((T, v_w), BF16), _sds((B, GDN_H, GDN_DK, GDN_DV), F32)],
        scratch_shapes=[pltpu.VMEM((8 + rows, GDN_CONV_DIM), F32),
                        pltpu.VMEM((GDN_H, GDN_DK, GDN_DV), F32)],
        compiler_params=_cp(("arbitrary", "arbitrary")), name="gdn",
    )(zm, zs, zm, conv_w, alog_v, dtb_v, nw, hist8, s0)


def _post_kernel(*refs, n_in):
    a_refs = refs[:n_in]
    w_refs = refs[n_in:2 * n_in]
    x_ref, mod1_ref, mod2_ref, nw2_ref, wr_ref, br_ref, xn_ref, h2_ref, rt_ref = refs[2 * n_in:]
    acc = jnp.dot(a_refs[0][...], w_refs[0][...], preferred_element_type=F32)
    for a_ref, w_ref in zip(a_refs[1:], w_refs[1:]):
        acc = acc + jnp.dot(a_ref[...], w_ref[...], preferred_element_type=F32)
    xn = x_ref[...] + mod1_ref[0, 2] * acc
    xn_ref[...] = xn
    h2 = _norm_mod(xn, nw2_ref[...], mod2_ref[0, 0], mod2_ref[0, 1])
    h2_ref[...] = h2
    logits = _dot3(_split_bf(h2), (wr_ref[0], wr_ref[1])) + br_ref[...]
    lane = lax.broadcasted_iota(I32, logits.shape, 1)
    is_g = lane < N_GROUPS
    gl = jnp.where(is_g, logits, NEG)
    gmax = jnp.max(gl, axis=-1, keepdims=True)
    g_sel = jnp.min(jnp.where(gl == gmax, lane, LANES), axis=-1, keepdims=True)
    p_g = 1.0 / jnp.sum(jnp.where(is_g, jnp.exp(logits - gmax), 0.0), axis=-1, keepdims=True)
    in_grp = jnp.logical_and(
        jnp.logical_and(lane >= N_GROUPS, lane < N_GROUPS + N_EXPERTS),
        jnp.right_shift(lane - N_GROUPS, GROUP_SHIFT) == g_sel)
    el = jnp.where(in_grp, logits, NEG)
    v1 = jnp.max(el, axis=-1, keepdims=True)
    i1 = jnp.min(jnp.where(jnp.logical_and(in_grp, el == v1), lane, LANES), axis=-1, keepdims=True)
    rest = jnp.logical_and(in_grp, lane != i1)
    el2 = jnp.where(rest, logits, NEG)
    v2 = jnp.max(el2, axis=-1, keepdims=True)
    i2 = jnp.min(jnp.where(jnp.logical_and(rest, el2 == v2), lane, LANES), axis=-1, keepdims=True)
    ex = jnp.exp(v2 - v1)
    w1 = 1.0 / (1.0 + ex)
    w2 = ex * w1
    e1 = (i1 - N_GROUPS).astype(F32)
    e2 = (i2 - N_GROUPS).astype(F32)
    rt_ref[...] = jnp.where(lane == 0, e1, jnp.where(lane == 1, e2, jnp.where(
        lane == 2, p_g * w1, jnp.where(lane == 3, p_g * w2, 0.0))))


def _post_call(a_list, w_list, x, mod1, mod2, nw2, w_route, b_route, L, tm):
    T = x.shape[0]
    per_b = L // tm
    n_in = len(a_list)
    tok = lambda i: (i, 0)
    const = lambda i: (0, 0)
    modmap = lambda i: (i // per_b, 0, 0, 0)
    in_specs = ([pl.BlockSpec((tm, a.shape[1]), tok) for a in a_list]
                + [pl.BlockSpec(w.shape, const) for w in w_list]
                + [pl.BlockSpec((tm, D_MODEL), tok),
                   pl.BlockSpec((1, 3, 1, D_MODEL), modmap),
                   pl.BlockSpec((1, 3, 1, D_MODEL), modmap),
                   pl.BlockSpec((1, D_MODEL), const),
                   pl.BlockSpec((2, D_MODEL, LANES), lambda i: (0, 0, 0)),
                   pl.BlockSpec((1, LANES), const)])
    return pl.pallas_call(
        functools.partial(_post_kernel, n_in=n_in), grid=(T // tm,),
        in_specs=in_specs,
        out_specs=[pl.BlockSpec((tm, D_MODEL), tok), pl.BlockSpec((tm, D_MODEL), tok),
                   pl.BlockSpec((tm, LANES), tok)],
        out_shape=[_sds((T, D_MODEL), F32), _sds((T, D_MODEL), F32), _sds((T, LANES), F32)],
        compiler_params=_cp(("arbitrary",)), name="post",
    )(*a_list, *w_list, x, mod1, mod2, nw2, w_route, b_route)


def _rank_kernel(rt_ref, rank_ref, cnt_ref, run_sc):
    @pl.when(pl.program_id(0) == 0)
    def _():
        run_sc[...] = jnp.zeros_like(run_sc)

    rt = rt_ref[...]
    tr = rt.shape[0]
    lane = lax.broadcasted_iota(I32, rt.shape, 1)
    o1 = lane == rt[:, 0:1].astype(I32)
    o2 = lane == rt[:, 1:2].astype(I32)
    onehot = jnp.where(o1, 1.0, 0.0) + jnp.where(o2, 1.0, 0.0)
    r = lax.broadcasted_iota(I32, (tr, tr), 0)
    cc = lax.broadcasted_iota(I32, (tr, tr), 1)
    before = jnp.dot(jnp.where(r > cc, 1.0, 0.0).astype(BF16), onehot.astype(BF16),
                     preferred_element_type=F32) + run_sc[...]
    rank1 = jnp.sum(jnp.where(o1, before, 0.0), axis=-1, keepdims=True)
    rank2 = jnp.sum(jnp.where(o2, before, 0.0), axis=-1, keepdims=True)
    rank_ref[...] = jnp.where(lane == 0, rank1, jnp.where(lane == 1, rank2, 0.0))
    run_sc[...] = run_sc[...] + jnp.sum(onehot, axis=0, keepdims=True)
    cnt_ref[...] = run_sc[...]


def _rank_call(route, tr):
    T = route.shape[0]
    return pl.pallas_call(
        _rank_kernel, grid=(T // tr,),
        in_specs=[pl.BlockSpec((tr, LANES), lambda i: (i, 0))],
        out_specs=[pl.BlockSpec((tr, LANES), lambda i: (i, 0)),
                   pl.BlockSpec((1, LANES), lambda i: (0, 0))],
        out_shape=[_sds((T, LANES), F32), _sds((1, LANES), F32)],
        scratch_shapes=[pltpu.VMEM((1, LANES), F32)],
        compiler_params=_cp(("arbitrary",)), name="moe_rank",
    )(route)


def _dest_kernel(rt_ref, rank_ref, ps_ref, d_ref):
    rt = rt_ref[...]
    rank = rank_ref[...]
    ps = ps_ref[...]
    lane = lax.broadcasted_iota(I32, rt.shape, 1)
    d1 = jnp.sum(jnp.where(lane == rt[:, 0:1].astype(I32), ps, 0.0), axis=-1, keepdims=True) + rank[:, 0:1]
    d2 = jnp.sum(jnp.where(lane == rt[:, 1:2].astype(I32), ps, 0.0), axis=-1, keepdims=True) + rank[:, 1:2]
    dd = jnp.where(lane == 0, d1, jnp.where(lane == 1, d2, 0.0))
    d_ref[0] = dd.T[0:8, :].astype(I32)


def _dest_call(route, rank, pstart, tr):
    T = route.shape[0]
    return pl.pallas_call(
        _dest_kernel, grid=(T // tr,),
        in_specs=[pl.BlockSpec((tr, LANES), lambda i: (i, 0)),
                  pl.BlockSpec((tr, LANES), lambda i: (i, 0)),
                  pl.BlockSpec((1, LANES), lambda i: (0, 0))],
        out_specs=pl.BlockSpec((1, 8, tr), lambda i: (i, 0, 0)),
        out_shape=_sds((T // tr, 8, tr), I32),
        compiler_params=_cp(("arbitrary",)), name="moe_dest",
    )(route, rank, pstart)


def _row_copy(src, s, dst, d, sem):
    return pltpu.make_async_copy(src.at[pl.ds(s, 1), :], dst.at[pl.ds(d, 1), :], sem)


def _disp_kernel(h_ref, dest_hbm, xg_in, xg_hbm, d_sm, sem_i, sem_o, *, tm):
    del xg_in
    i = pl.program_id(0)
    idx_cp = pltpu.make_async_copy(dest_hbm.at[i], d_sm, sem_i)
    idx_cp.start()
    idx_cp.wait()

    def issue(t, carry):
        for kk in range(2):
            _row_copy(h_ref, t, xg_hbm, d_sm[kk, t], sem_o).start()
        return carry

    lax.fori_loop(0, tm, issue, 0, unroll=8)
    for kk in range(2):
        pltpu.make_async_copy(h_ref, xg_hbm.at[pl.ds(0, tm), :], sem_o).wait()


def _disp_call(h2, dest, xg0, tm):
    T = h2.shape[0]
    return pl.pallas_call(
        functools.partial(_disp_kernel, tm=tm), grid=(T // tm,),
        in_specs=[pl.BlockSpec((tm, D_MODEL), lambda i: (i, 0)),
                  pl.BlockSpec(memory_space=pl.ANY),
                  pl.BlockSpec(memory_space=pl.ANY)],
        out_specs=pl.BlockSpec(memory_space=pl.ANY),
        out_shape=_sds(xg0.shape, F32),
        scratch_shapes=[pltpu.SMEM((8, tm), I32), pltpu.SemaphoreType.DMA, pltpu.SemaphoreType.DMA],
        input_output_aliases={2: 0},
        compiler_params=_cp(("arbitrary",)), name="moe_dispatch",
    )(h2, dest, xg0)


def _expert_kernel(be_ref, nu_ref, x_ref, w1_ref, w3_ref, w2_ref, y_ref, w1b, w3b, w2b):
    i = pl.program_id(0)
    used = i < nu_ref[0]
    fresh = jnp.logical_or(i == 0, be_ref[i] != be_ref[jnp.maximum(i - 1, 0)])

    @pl.when(jnp.logical_and(used, fresh))
    def _():
        w1b[...] = w1_ref[0, 0].astype(BF16)
        w3b[...] = w3_ref[0, 0].astype(BF16)
        w2b[...] = w2_ref[0, 0].astype(BF16)

    @pl.when(used)
    def _():
        x = x_ref[...].astype(BF16)
        a = jnp.dot(x, w1b[...], preferred_element_type=F32)
        g = jnp.dot(x, w3b[...], preferred_element_type=F32)
        hm = (a * _sigmoid(a) * g).astype(BF16)
        y_ref[...] = jnp.dot(hm, w2b[...], preferred_element_type=F32)

    @pl.when(jnp.logical_not(used))
    def _():
        y_ref[...] = jnp.zeros(y_ref.shape, F32)


def _expert_call(blk_e, n_used, xg, w1, w3, w2, layer):
    P = xg.shape[0]
    nb = P // MOE_ROWS
    rowmap = lambda i, be, nu: (jnp.minimum(i, nu[0] - 1), 0)
    wmap = lambda i, be, nu: (layer, be[i], 0, 0)
    return pl.pallas_call(
        _expert_kernel,
        grid_spec=pltpu.PrefetchScalarGridSpec(
            num_scalar_prefetch=2, grid=(nb,),
            in_specs=[pl.BlockSpec((MOE_ROWS, D_MODEL), rowmap),
                      pl.BlockSpec((1, 1, D_MODEL, D_EXPERT), wmap),
                      pl.BlockSpec((1, 1, D_MODEL, D_EXPERT), wmap),
                      pl.BlockSpec((1, 1, D_EXPERT, D_MODEL), wmap)],
            out_specs=pl.BlockSpec((MOE_ROWS, D_MODEL), lambda i, be, nu: (i, 0)),
            scratch_shapes=[pltpu.VMEM((D_MODEL, D_EXPERT), BF16),
                            pltpu.VMEM((D_MODEL, D_EXPERT), BF16),
                            pltpu.VMEM((D_EXPERT, D_MODEL), BF16)]),
        out_shape=_sds((P, D_MODEL), F32),
        compiler_params=_cp(("arbitrary",)), name="moe_experts",
    )(blk_e, n_used, xg, w1, w3, w2)


def _comb_kernel(x_ref, rt_ref, mod_ref, fnw_ref, dest_hbm, y_hbm, o_ref, d_sm, buf, sem_i, sem_g,
                 *, tm, final):
    i = pl.program_id(0)
    idx_cp = pltpu.make_async_copy(dest_hbm.at[i], d_sm, sem_i)
    idx_cp.start()
    idx_cp.wait()

    def issue(t, carry):
        for kk in range(2):
            _row_copy(y_hbm, d_sm[kk, t], buf.at[kk], t, sem_g).start()
        return carry

    lax.fori_loop(0, tm, issue, 0, unroll=8)
    for kk in range(2):
        pltpu.make_async_copy(y_hbm.at[pl.ds(0, tm), :], buf.at[kk], sem_g).wait()
    rt = rt_ref[...]
    y = rt[:, 2:3] * buf[0] + rt[:, 3:4] * buf[1]
    out = x_ref[...] + mod_ref[0, 2] * y
    if final:
        out = _rms(out) * fnw_ref[...]
    o_ref[...] = out


def _comb_call(xn, route, mod, fnw, dest, yb, L, tm, final):
    T = xn.shape[0]
    per_b = L // tm
    return pl.pallas_call(
        functools.partial(_comb_kernel, tm=tm, final=final), grid=(T // tm,),
        in_specs=[pl.BlockSpec((tm, D_MODEL), lambda i: (i, 0)),
                  pl.BlockSpec((tm, LANES), lambda i: (i, 0)),
                  pl.BlockSpec((1, 3, 1, D_MODEL), lambda i: (i // per_b, 0, 0, 0)),
                  pl.BlockSpec((1, D_MODEL), lambda i: (0, 0)),
                  pl.BlockSpec(memory_space=pl.ANY),
                  pl.BlockSpec(memory_space=pl.ANY)],
        out_specs=pl.BlockSpec((tm, D_MODEL), lambda i: (i, 0)),
        out_shape=_sds((T, D_MODEL), F32),
        scratch_shapes=[pltpu.SMEM((8, tm), I32), pltpu.VMEM((2, tm, D_MODEL), F32),
                        pltpu.SemaphoreType.DMA, pltpu.SemaphoreType.DMA],
        compiler_params=_cp(("arbitrary",)), name="moe_combine",
    )(xn, route, mod, fnw, dest, yb)


def _moe(xn, h2, route, mod, fnw, w1, w3, w2, layer, L, final):
    T = xn.shape[0]
    tr = min(512, T)
    rank, counts = _rank_call(route, tr)
    cnt = counts[0, :N_EXPERTS].astype(I32)
    padded = (cnt + MOE_ROWS - 1) // MOE_ROWS * MOE_ROWS
    pend = jnp.cumsum(padded)
    pstart = jnp.zeros((1, LANES), F32).at[0, :N_EXPERTS].set((pend - padded).astype(F32))
    nb = -(-(2 * T + N_EXPERTS * (MOE_ROWS - 1)) // MOE_ROWS)
    blk_e = jnp.clip(jnp.searchsorted(pend, jnp.arange(nb, dtype=I32) * MOE_ROWS, side='right'),
                     0, N_EXPERTS - 1).astype(I32)
    n_used = (pend[N_EXPERTS - 1:] // MOE_ROWS).astype(I32)
    td = min(256, T)
    tc = min(td, L)
    dest_d = _dest_call(route, rank, pstart, td)
    dest_c = dest_d if tc == td else _dest_call(route, rank, pstart, tc)
    xg = _disp_call(h2, dest_d, jnp.zeros((nb * MOE_ROWS, D_MODEL), F32), td)
    yb = _expert_call(blk_e, n_used, xg, w1, w3, w2, layer)
    return _comb_call(xn, route, mod, fnw, dest_c, yb, L, tc, final)


def _rank_kernel(rt_ref, pos_ref, post_ref, cnt_ref):
    rt = rt_ref[...]
    tm = rt.shape[0]
    lane = lax.broadcasted_iota(I32, rt.shape, 1)
    o1 = lane == rt[:, 0:1].astype(I32)
    o2 = lane == rt[:, 1:2].astype(I32)
    onehot = jnp.where(o1, 1.0, 0.0) + jnp.where(o2, 1.0, 0.0)
    r = lax.broadcasted_iota(I32, (tm, tm), 0)
    cc = lax.broadcasted_iota(I32, (tm, tm), 1)
    before = jnp.dot(jnp.where(r > cc, 1.0, 0.0).astype(BF16), onehot.astype(BF16),
                     preferred_element_type=F32)
    cnt = jnp.sum(onehot, axis=0, keepdims=True)
    er = lax.broadcasted_iota(I32, (LANES, LANES), 0)
    ec = lax.broadcasted_iota(I32, (LANES, LANES), 1)
    cnt8 = jnp.floor((cnt + (MOE_ALIGN - 1)) * (1.0 / MOE_ALIGN)) * MOE_ALIGN
    start = _dot_hi(jnp.broadcast_to(cnt8, (8, LANES)), jnp.where(er < ec, 1.0, 0.0))[0:1]
    where = before + start
    p1 = jnp.sum(jnp.where(o1, where, 0.0), axis=-1, keepdims=True)
    p2 = jnp.sum(jnp.where(o2, where, 0.0), axis=-1, keepdims=True)
    pos = jnp.where(lane == 0, p1, jnp.where(lane == 1, p2, 0.0))
    pos_ref[...] = pos
    post_ref[0] = pos.T[0:8, :].astype(I32)
    cnt_ref[0] = cnt


def _rank_call(route, tm):
    T = route.shape[0]
    nt = T // tm
    return pl.pallas_call(
        _rank_kernel, grid=(nt,),
        in_specs=[pl.BlockSpec((tm, LANES), lambda i: (i, 0))],
        out_specs=[pl.BlockSpec((tm, LANES), lambda i: (i, 0)),
                   pl.BlockSpec((1, 8, tm), lambda i: (i, 0, 0)),
                   pl.BlockSpec((1, 1, LANES), lambda i: (i, 0, 0))],
        out_shape=[_sds((T, LANES), F32), _sds((nt, 8, tm), I32), _sds((nt, 1, LANES), F32)],
        compiler_params=_cp(("arbitrary",)), name="moe_rank",
    )(route)


def _segment_dmas(tile, n_ref, src_ref, dst_ref, copy, act):
    big_rows = MOE_BIG * MOE_ALIGN

    def per_expert(e, carry):
        idx = tile * N_EXPERTS + e
        u, s, d = n_ref[idx], src_ref[idx], dst_ref[idx]
        n_big = u // MOE_BIG

        def big(c, carry):
            o = c * big_rows
            act(copy(pl.multiple_of(s + o, MOE_ALIGN), pl.multiple_of(d + o, MOE_ALIGN), big_rows))
            return carry

        lax.fori_loop(0, n_big, big, 0)

        def small(c, carry):
            o = n_big * big_rows + c * MOE_ALIGN
            act(copy(pl.multiple_of(s + o, MOE_ALIGN), pl.multiple_of(d + o, MOE_ALIGN), MOE_ALIGN))
            return carry

        lax.fori_loop(0, u - n_big * MOE_BIG, small, 0)
        return carry

    lax.fori_loop(0, N_EXPERTS, per_expert, 0)


def _disp_kernel(n_ref, src_ref, dst_ref, last_ref, h_ref, rt_ref, post_ref, xg_hbm, xs_sc, sem, *, tm):
    i = pl.program_id(0)

    @pl.when(i == 0)
    def _():
        xs_sc[0:MOE_ROWS, :] = jnp.zeros((MOE_ROWS, MOE_XCOLS), F32)
        zero_block = lambda blk: pltpu.make_async_copy(
            xs_sc.at[pl.ds(0, MOE_ROWS), :], xg_hbm.at[pl.ds(blk * MOE_ROWS, MOE_ROWS), :], sem)
        n_blocks = xg_hbm.shape[0] // MOE_ROWS

        def fill(e, act):
            @pl.when(last_ref[e] >= 0)
            def _():
                act(zero_block(last_ref[e]))

        def start_all(e, carry):
            fill(e, lambda cp: cp.start())
            return carry

        def wait_all(e, carry):
            fill(e, lambda cp: cp.wait())
            return carry

        def start_tail(blk, carry):
            zero_block(blk).start()
            return carry

        def wait_tail(blk, carry):
            zero_block(blk).wait()
            return carry

        lax.fori_loop(0, N_EXPERTS, start_all, 0)
        lax.fori_loop(last_ref[N_EXPERTS], n_blocks, start_tail, 0)
        lax.fori_loop(0, N_EXPERTS, wait_all, 0)
        lax.fori_loop(last_ref[N_EXPERTS], n_blocks, wait_tail, 0)

    post = post_ref[0]
    cap = xs_sc.shape[0]
    rows = lax.broadcasted_iota(I32, (cap, tm), 0)
    p1 = jnp.where(rows == post[0:1, :], 1.0, 0.0).astype(BF16)
    p2 = jnp.where(rows == post[1:2, :], 1.0, 0.0).astype(BF16)
    xs_sc[:, 0:D_MODEL] = jnp.dot(p1 + p2, h_ref[...].astype(BF16), preferred_element_type=F32)
    rt = rt_ref[...]
    lane = lax.broadcasted_iota(I32, rt.shape, 1)

    def hi_lo(col):
        hi = col.astype(BF16).astype(F32)
        return jnp.where(lane == 0, hi, jnp.where(lane == 1, col - hi, 0.0)).astype(BF16)

    xs_sc[:, D_MODEL:] = (jnp.dot(p1, hi_lo(rt[:, 2:3]), preferred_element_type=F32)
                          + jnp.dot(p2, hi_lo(rt[:, 3:4]), preferred_element_type=F32))

    copy = lambda s, d, n: pltpu.make_async_copy(xs_sc.at[pl.ds(s, n), :], xg_hbm.at[pl.ds(d, n), :], sem)
    _segment_dmas(i, n_ref, src_ref, dst_ref, copy, lambda cp: cp.start())
    _segment_dmas(i, n_ref, src_ref, dst_ref, copy, lambda cp: cp.wait())


def _disp_call(seg_n, seg_src, seg_dst, last_blk, h2, route, post, n_rows, tm):
    T = h2.shape[0]
    return pl.pallas_call(
        functools.partial(_disp_kernel, tm=tm),
        grid_spec=pltpu.PrefetchScalarGridSpec(
            num_scalar_prefetch=4, grid=(T // tm,),
            in_specs=[pl.BlockSpec((tm, D_MODEL), lambda i, *_: (i, 0)),
                      pl.BlockSpec((tm, LANES), lambda i, *_: (i, 0)),
                      pl.BlockSpec((1, 8, tm), lambda i, *_: (i, 0, 0))],
            out_specs=pl.BlockSpec(memory_space=pl.ANY),
            scratch_shapes=[pltpu.VMEM((_moe_cap(tm), MOE_XCOLS), F32), pltpu.SemaphoreType.DMA]),
        out_shape=_sds((n_rows, MOE_XCOLS), F32),
        compiler_params=_cp(("arbitrary",)), name="moe_dispatch",
    )(seg_n, seg_src, seg_dst, last_blk, h2, route, post)


def _expert_kernel(be_ref, nu_ref, x_ref, w1_ref, w3_ref, w2_ref, y_ref, w1b, w3b, w2b):
    i = pl.program_id(0)
    used = i < nu_ref[0]
    fresh = jnp.logical_or(i == 0, be_ref[i] != be_ref[jnp.maximum(i - 1, 0)])

    @pl.when(jnp.logical_and(used, fresh))
    def _():
        w1b[...] = w1_ref[0, 0].astype(BF16)
        w3b[...] = w3_ref[0, 0].astype(BF16)
        w2b[...] = w2_ref[0, 0].astype(BF16)

    @pl.when(used)
    def _():
        x = x_ref[:, 0:D_MODEL].astype(BF16)
        a = jnp.dot(x, w1b[...], preferred_element_type=F32)
        g = jnp.dot(x, w3b[...], preferred_element_type=F32)
        hm = (a * _sigmoid(a) * g).astype(BF16)
        row_w = x_ref[:, D_MODEL:D_MODEL + 1] + x_ref[:, D_MODEL + 1:D_MODEL + 2]
        y_ref[...] = jnp.dot(hm, w2b[...], preferred_element_type=F32) * row_w

    @pl.when(jnp.logical_not(used))
    def _():
        y_ref[...] = jnp.zeros(y_ref.shape, F32)


def _expert_call(blk_e, n_used, xg, w1, w3, w2, layer):
    P = xg.shape[0]
    nb = P // MOE_ROWS
    rowmap = lambda i, be, nu: (jnp.minimum(i, nu[0] - 1), 0)
    wmap = lambda i, be, nu: (layer, be[i], 0, 0)
    return pl.pallas_call(
        _expert_kernel,
        grid_spec=pltpu.PrefetchScalarGridSpec(
            num_scalar_prefetch=2, grid=(nb,),
            in_specs=[pl.BlockSpec((MOE_ROWS, MOE_XCOLS), rowmap),
                      pl.BlockSpec((1, 1, D_MODEL, D_EXPERT), wmap),
                      pl.BlockSpec((1, 1, D_MODEL, D_EXPERT), wmap),
                      pl.BlockSpec((1, 1, D_EXPERT, D_MODEL), wmap)],
            out_specs=pl.BlockSpec((MOE_ROWS, D_MODEL), lambda i, be, nu: (i, 0)),
            scratch_shapes=[pltpu.VMEM((D_MODEL, D_EXPERT), BF16),
                            pltpu.VMEM((D_MODEL, D_EXPERT), BF16),
                            pltpu.VMEM((D_EXPERT, D_MODEL), BF16)]),
        out_shape=_sds((P, D_MODEL), F32),
        compiler_params=_cp(("arbitrary",)), name="moe_experts",
    )(blk_e, n_used, xg, w1, w3, w2)


def _comb_kernel(n_ref, src_ref, dst_ref, x_ref, pos_ref, mod_ref, fnw_ref, y_hbm, o_ref, ys_sc, sem,
                 *, tm, final):
    i = pl.program_id(0)
    cap = ys_sc.shape[0]
    ys_sc[2 * tm:cap, :] = jnp.zeros((cap - 2 * tm, D_MODEL), F32)
    copy = lambda s, d, n: pltpu.make_async_copy(y_hbm.at[pl.ds(d, n), :], ys_sc.at[pl.ds(s, n), :], sem)
    _segment_dmas(i, n_ref, src_ref, dst_ref, copy, lambda cp: cp.start())
    _segment_dmas(i, n_ref, src_ref, dst_ref, copy, lambda cp: cp.wait())
    pos = pos_ref[...]
    cols = lax.broadcasted_iota(I32, (tm, cap), 1)
    pick = (jnp.where(cols == pos[:, 0:1].astype(I32), 1.0, 0.0)
            + jnp.where(cols == pos[:, 1:2].astype(I32), 1.0, 0.0)).astype(BF16)
    y = jnp.dot(pick, ys_sc[...].astype(BF16), preferred_element_type=F32)
    out = x_ref[...] + mod_ref[0, 2] * y
    if final:
        out = _rms(out) * fnw_ref[...]
    o_ref[...] = out


def _comb_call(seg_n, seg_src, seg_dst, xn, pos, mod, fnw, yb, L, tm, final):
    T = xn.shape[0]
    per_b = L // tm
    return pl.pallas_call(
        functools.partial(_comb_kernel, tm=tm, final=final),
        grid_spec=pltpu.PrefetchScalarGridSpec(
            num_scalar_prefetch=3, grid=(T // tm,),
            in_specs=[pl.BlockSpec((tm, D_MODEL), lambda i, *_: (i, 0)),
                      pl.BlockSpec((tm, LANES), lambda i, *_: (i, 0)),
                      pl.BlockSpec((1, 3, 1, D_MODEL), lambda i, *_: (i // per_b, 0, 0, 0)),
                      pl.BlockSpec((1, D_MODEL), lambda i, *_: (0, 0)),
                      pl.BlockSpec(memory_space=pl.ANY)],
            out_specs=pl.BlockSpec((tm, D_MODEL), lambda i, *_: (i, 0)),
            scratch_shapes=[pltpu.VMEM((_moe_cap(tm), D_MODEL), F32), pltpu.SemaphoreType.DMA]),
        out_shape=_sds((T, D_MODEL), F32),
        compiler_params=_cp(("arbitrary",)), name="moe_combine",
    )(seg_n, seg_src, seg_dst, xn, pos, mod, fnw, yb)


def _moe_cap(tm):
    return max(2 * tm + N_EXPERTS * MOE_ALIGN, MOE_ROWS)


def _moe(xn, h2, route, mod, fnw, w1, w3, w2, layer, L, final):
    T = xn.shape[0]
    tm = min(MOE_TILE, L)
    nt = T // tm
    pos, post, counts = _rank_call(route, tm)
    cnt = counts[:, 0, :N_EXPERTS].astype(I32)
    cnt = (cnt + MOE_ALIGN - 1) // MOE_ALIGN * MOE_ALIGN
    total = jnp.sum(cnt, axis=0)
    padded = (total + MOE_ROWS - 1) // MOE_ROWS * MOE_ROWS
    pend = jnp.cumsum(padded)
    seg_dst = (pend - padded)[None, :] + jnp.cumsum(cnt, axis=0) - cnt
    seg_src = jnp.cumsum(cnt, axis=1) - cnt
    nb = -(-(2 * T + nt * N_EXPERTS * (MOE_ALIGN - 1) + N_EXPERTS * (MOE_ROWS - 1)) // MOE_ROWS)
    blk_e = jnp.minimum(jnp.sum(pend[None, :] <= (jnp.arange(nb, dtype=I32) * MOE_ROWS)[:, None], axis=1),
                        N_EXPERTS - 1).astype(I32)
    n_used = (pend[N_EXPERTS - 1:] // MOE_ROWS).astype(I32)
    last_blk = jnp.concatenate([jnp.where(padded > 0, pend // MOE_ROWS - 1, -1).astype(I32), n_used])
    flat = lambda a: a.reshape(nt * N_EXPERTS).astype(I32)
    seg = (flat(cnt // MOE_ALIGN), flat(seg_src), flat(seg_dst))
    xg = _disp_call(*seg, last_blk, h2, route, post, nb * MOE_ROWS, tm)
    yb = _expert_call(blk_e, n_used, xg, w1, w3, w2, layer)
    return _comb_call(*seg, xn, pos, mod, fnw, yb, L, tm, final)


def _in1_kernel(x_ref, mod_ref, nw_ref, win_ref, qnw_ref, wuq_ref, kvnw_ref, tq_ref, tk_ref,
                q_ref, ckv_ref, kr_ref):
    h = _norm_mod(x_ref[...], nw_ref[...], mod_ref[0, 0], mod_ref[0, 1])
    zz = _dot_bf(h, win_ref[...])
    cqn = _rms(zz[:, :MLA_Q_RANK]) * qnw_ref[...]
    q = _dot_bf(cqn, wuq_ref[...])
    tq = tq_ref[...]
    for hh in range(MLA_H):
        q_ref[:, hh * LANES:(hh + 1) * LANES] = (q[:, hh * LANES:(hh + 1) * LANES] * tq).astype(BF16)
    c0 = MLA_Q_RANK
    ckv_ref[...] = _rms(zz[:, c0:c0 + MLA_KV_RANK]) * kvnw_ref[...]
    c1 = c0 + MLA_KV_RANK
    tk = tk_ref[...]
    kr_ref[...] = (zz[:, c1:c1 + MLA_ROPE] * tk[:, :MLA_ROPE]
                   + zz[:, c1 + LANES:c1 + LANES + MLA_ROPE] * tk[:, MLA_ROPE:])


def _in1_call(x, mod, nw, w_in_r, q_nw, w_uq_r, kv_nw, tab_q, tab_k, L, tm):
    T = x.shape[0]
    per_b = L // tm
    tok = lambda i: (i, 0)
    const = lambda i: (0, 0)
    pos = lambda i: (i % per_b, 0)
    return pl.pallas_call(
        _in1_kernel, grid=(T // tm,),
        in_specs=[pl.BlockSpec((tm, D_MODEL), tok),
                  pl.BlockSpec((1, 3, 1, D_MODEL), lambda i: (i // per_b, 0, 0, 0)),
                  pl.BlockSpec((1, D_MODEL), const),
                  pl.BlockSpec((D_MODEL, MLA_IN_R), const),
                  pl.BlockSpec((1, MLA_Q_RANK), const),
                  pl.BlockSpec((MLA_Q_RANK, MLA_H * LANES), const),
                  pl.BlockSpec((1, MLA_KV_RANK), const),
                  pl.BlockSpec((tm, LANES), pos),
                  pl.BlockSpec((tm, 2 * MLA_ROPE), pos)],
        out_specs=[pl.BlockSpec((tm, MLA_H * LANES), tok),
                   pl.BlockSpec((tm, MLA_KV_RANK), tok),
                   pl.BlockSpec((tm, MLA_ROPE), tok)],
        out_shape=[_sds((T, MLA_H * LANES), BF16), _sds((T, MLA_KV_RANK), F32),
                   _sds((T, MLA_ROPE), F32)],
        compiler_params=_cp(("arbitrary",)), name="in1",
    )(x, mod, nw, w_in_r, q_nw, w_uq_r, kv_nw, tab_q, tab_k)


def _kv_kernel(ckv_ref, kr_ref, wk_ref, pe_ref, wv_ref, one_ref, k_ref, v_ref):
    c = ckv_ref[...].astype(BF16)
    k_ref[...] = (jnp.dot(c, wk_ref[...], preferred_element_type=F32)
                  + jnp.dot(kr_ref[...].astype(BF16), pe_ref[...], preferred_element_type=F32)).astype(BF16)
    v_ref[...] = (jnp.dot(c, wv_ref[...], preferred_element_type=F32) + one_ref[...]).astype(BF16)


def _kv_call(ckv, kr, wk_r, place, wv_r, ones_row, tm):
    T = ckv.shape[0]
    tok = lambda i: (i, 0)
    const = lambda i: (0, 0)
    wide = MLA_H * LANES
    return pl.pallas_call(
        _kv_kernel, grid=(T // tm,),
        in_specs=[pl.BlockSpec((tm, MLA_KV_RANK), tok),
                  pl.BlockSpec((tm, MLA_ROPE), tok),
                  pl.BlockSpec((MLA_KV_RANK, wide), const),
                  pl.BlockSpec((MLA_ROPE, wide), const),
                  pl.BlockSpec((MLA_KV_RANK, wide), const),
                  pl.BlockSpec((1, wide), const)],
        out_specs=[pl.BlockSpec((tm, wide), tok), pl.BlockSpec((tm, wide), tok)],
        out_shape=[_sds((T, wide), BF16), _sds((T, wide), BF16)],
        compiler_params=_cp(("arbitrary",)), name="mla_kv",
    )(ckv, kr, wk_r, place, wv_r, ones_row)


def _attn_kernel(q_ref, k_ref, v_ref, o_ref, m0, m1, a0, a1, s0, s1, *, tq, tk, Lk, pos0):
    m_scs, acc_scs, s_scs = (m0, m1), (a0, a1), (s0, s1)
    q_lo = pos0 + pl.program_id(2) * tq
    k_hi = jnp.minimum(((q_lo + tq - 1) // CHUNK + 1) * CHUNK, Lk)
    n_blk = (k_hi + tk - 1) // tk
    n_full = jnp.minimum(((q_lo // CHUNK + 1) * CHUNK) // tk, n_blk)
    q_chunk = jnp.right_shift(q_lo + lax.broadcasted_iota(I32, (tq, tk), 0), CHUNK_SHIFT)
    k_iota = lax.broadcasted_iota(I32, (tq, tk), 1)

    def scores(hh, k0):
        hc = slice(hh * LANES, (hh + 1) * LANES)
        return lax.dot_general(q_ref[:, hc], k_ref[pl.ds(k0, tk), hc], NT, preferred_element_type=F32)

    for hh in range(2):
        m_scs[hh][...] = jnp.full((tq, LANES), NEG, F32)
        acc_scs[hh][...] = jnp.zeros((tq, LANES), F32)
        s_scs[hh][...] = scores(hh, 0)

    def step(j, carry, masked):
        k0 = pl.multiple_of(j * tk, tk)
        k1 = pl.multiple_of(jnp.minimum(j + 1, n_blk - 1) * tk, tk)
        ahead = [scores(hh, k1) for hh in range(2)]
        for hh in range(2):
            hc = slice(hh * LANES, (hh + 1) * LANES)
            s = s_scs[hh][...]
            if masked:
                s = jnp.where(q_chunk >= jnp.right_shift(k0 + k_iota, CHUNK_SHIFT), s, NEG)
            m_prev = m_scs[hh][...]
            m_new = jnp.maximum(m_prev, jnp.max(s, axis=-1, keepdims=True))
            if tk % LANES == 0:
                p = jnp.exp2(s - jnp.concatenate([m_new] * (tk // LANES), axis=1))
            else:
                p = jnp.exp2(s - m_new[:, 0:1])
            acc_scs[hh][...] = (jnp.exp2(m_prev - m_new) * acc_scs[hh][...]
                                + jnp.dot(p.astype(BF16), v_ref[pl.ds(k0, tk), hc],
                                          preferred_element_type=F32))
            m_scs[hh][...] = m_new
        for hh in range(2):
            s_scs[hh][...] = ahead[hh]
        return carry

    lax.fori_loop(0, n_full, functools.partial(step, masked=False), 0)
    lax.fori_loop(n_full, n_blk, functools.partial(step, masked=True), 0)
    outs = []
    for hh in range(2):
        acc = acc_scs[hh][...]
        outs.append(acc[:, :MLA_V] / acc[:, MLA_V:MLA_V + 1])
    o_ref[...] = jnp.concatenate(outs, axis=1).astype(BF16)


def _attn_call(q, k, v, B, Lq, Lk, pos0, tq, tk):
    nq = Lq // tq
    return pl.pallas_call(
        functools.partial(_attn_kernel, tq=tq, tk=tk, Lk=Lk, pos0=pos0), grid=(B, MLA_H // 2, nq),
        in_specs=[pl.BlockSpec((tq, 2 * LANES), lambda b, hp, i: (b * nq + i, hp)),
                  pl.BlockSpec((Lk, 2 * LANES), lambda b, hp, i: (b, hp)),
                  pl.BlockSpec((Lk, 2 * LANES), lambda b, hp, i: (b, hp))],
        out_specs=pl.BlockSpec((tq, 2 * MLA_V), lambda b, hp, i: (b * nq + i, hp)),
        out_shape=_sds((B * Lq, MLA_H * MLA_V), BF16),
        scratch_shapes=[pltpu.VMEM((tq, LANES), F32)] * 4 + [pltpu.VMEM((tq, tk), F32)] * 2,
        compiler_params=_cp(("arbitrary", "arbitrary", "arbitrary")), name="mla_attn",
    )(q, k, v)


def _split_cols(w, widths):
    offs = [0]
    for n in widths:
        offs.append(offs[-1] + n)
    return [w[:, offs[i]:offs[i + 1]] for i in range(len(widths))]


def _hi_lo_pair(w):
    hi = w.astype(BF16)
    return jnp.stack([hi, (w - hi.astype(F32)).astype(BF16)], axis=-3)


def _prep_even(ab_w_in, gla_w_alpha, gdn_a_log, gdn_dt_bias):
    gq, gk, gv, glr, gr, qkv, ga, gb, gg = _split_cols(ab_w_in, AB_IN_WIDTHS)
    w_main = jnp.concatenate([gq, gk, gv, gr, qkv, gg], axis=1).astype(BF16)
    small = jnp.concatenate([glr, ga, gb], axis=1)
    w_small = _hi_lo_pair(jnp.pad(small, ((0, 0), (0, LANES - small.shape[1]))))
    wa_pad = jnp.pad(gla_w_alpha, ((0, LANES - GLA_LR), (0, 0)))
    alog_v = jnp.zeros((1, LANES), F32).at[0, LANE_GA:LANE_GA + GDN_H].set(gdn_a_log)
    dtb_v = jnp.zeros((1, LANES), F32).at[0, LANE_GA:LANE_GA + GDN_H].set(gdn_dt_bias)
    return w_main, w_small, wa_pad, alog_v, dtb_v


def _swap_halves(w):
    half = w.shape[-1] // 2
    return jnp.concatenate([w[..., half:], w[..., :half]], axis=-1)


def _prep_odd(mla_w_in, mla_w_uq, mla_w_ukv):
    cq, ckv, kr = _split_cols(mla_w_in, (MLA_Q_RANK, MLA_KV_RANK, MLA_ROPE))
    pad = lambda w: jnp.pad(w, ((0, 0), (0, LANES - w.shape[1])))
    w_in_r = jnp.concatenate([cq, ckv, pad(kr), pad(_swap_halves(kr))], axis=1).astype(BF16)
    uq = mla_w_uq.reshape(MLA_Q_RANK, MLA_H, MLA_NOPE + MLA_ROPE)
    uq_rope = uq[..., MLA_NOPE:]
    w_uq_r = jnp.concatenate([uq, _swap_halves(uq_rope)], axis=-1).reshape(
        MLA_Q_RANK, MLA_H * LANES).astype(BF16)
    ukv = mla_w_ukv.reshape(MLA_KV_RANK, MLA_H, MLA_NOPE + MLA_V)
    wk_r = jnp.pad(ukv[..., :MLA_NOPE], ((0, 0), (0, 0), (0, LANES - MLA_NOPE))).reshape(
        MLA_KV_RANK, MLA_H * LANES).astype(BF16)
    wv_r = jnp.pad(ukv[..., MLA_NOPE:], ((0, 0), (0, 0), (0, LANES - MLA_V))).reshape(
        MLA_KV_RANK, MLA_H * LANES).astype(BF16)
    ones_row = jnp.tile((jnp.arange(LANES) == MLA_V).astype(F32), MLA_H)[None, :]
    eye = jnp.eye(MLA_ROPE, dtype=F32)
    place = jnp.concatenate([jnp.zeros((MLA_ROPE, MLA_NOPE), F32), eye, eye], axis=1)
    place = jnp.tile(place, (1, MLA_H)).astype(BF16)
    return w_in_r, w_uq_r, wk_r, wv_r, place, ones_row


def _rope_tables(pos0, L):
    half = MLA_ROPE // 2
    inv = jnp.exp(-math.log(ROPE_THETA) * jnp.arange(half, dtype=F32) / half)
    ang = (pos0 + jnp.arange(L, dtype=I32)).astype(F32)[:, None] * inv[None, :]
    cos, sin = jnp.cos(ang), jnp.sin(ang)
    tab_k = jnp.concatenate([cos, cos, -sin, sin], axis=1)
    scale = (MLA_NOPE + MLA_ROPE) ** -0.5 * math.log2(math.e)
    tab_q = jnp.concatenate([jnp.ones((L, MLA_NOPE), F32), tab_k], axis=1) * scale
    return tab_q, tab_k


def _mod4(mods, lo, hi):
    return [mods[i, lo:hi].reshape(hi - lo, 3, 1, D_MODEL) for i in range(mods.shape[0])]


def _trunk(x3, mods, pos0, gla_s, gdn_s, conv_s, past_ckv, past_kr, p):
    B, L, _ = x3.shape
    T = B * L
    tm = min(512, L)
    x = x3.reshape(T, D_MODEL)
    row = lambda a: a.reshape(1, -1)

    w_main, w_small, wa_pad, alog_v, dtb_v = p['even']
    zm, zs = _in0_call(x, mods[0], row(p['norm_w'][0, 0]), w_main, w_small, L, tm)
    o_gla, gla_t = _gla_call(zm, zs, wa_pad, row(p['gla_b_alpha'][0]), row(p['gla_norm_w'][0]),
                             jnp.swapaxes(gla_s, -1, -2), B, L)
    hist8 = jnp.pad(conv_s, ((0, 0), (8 - (CONV_W - 1), 0), (0, 0)))
    o_gdn, gdn_new = _gdn_call(zm, zs, p['gdn_conv_w'][0], alog_v, dtb_v, row(p['gdn_norm_w'][0]),
                               hist8, gdn_s, B, L)
    qkv0 = 3 * GLA_H * GLA_DV
    conv_new = zm.reshape(B, L, AB_MAIN)[:, L - (CONV_W - 1):, qkv0:qkv0 + GDN_CONV_DIM].astype(F32)
    w_out = p['ab_w_out_bf']
    half = GLA_H * GLA_DV
    xn, h2, route = _post_call([o_gla, o_gdn], [w_out[:half], w_out[half:]], x, mods[0], mods[1],
                               row(p['norm_w'][0, 1]), p['w_route'][0], p['b_route'][0], L, tm)
    x = _moe(xn, h2, route, mods[1], row(p['final_norm_w']), p['moe_w1'], p['moe_w3'], p['moe_w2'],
             0, L, final=False)

    w_in_r, w_uq_r, wk_r, wv_r, place, ones_row = p['odd']
    tab_q, tab_k = _rope_tables(pos0, L)
    q, ckv, kr = _in1_call(x, mods[2], row(p['norm_w'][1, 0]), w_in_r, row(p['mla_q_norm_w'][0]),
                           w_uq_r, row(p['mla_kv_norm_w'][0]), tab_q, tab_k, L, tm)
    if past_ckv is None:
        ckv_all, kr_all, Lk = ckv, kr, L
    else:
        Lk = past_ckv.shape[1] + L
        ckv_all = jnp.concatenate([past_ckv, ckv.reshape(B, L, -1)], axis=1).reshape(B * Lk, -1)
        kr_all = jnp.concatenate([past_kr, kr.reshape(B, L, -1)], axis=1).reshape(B * Lk, -1)
    tkv = 512 if (B * Lk) % 512 == 0 else Lk
    k_all, v_all = _kv_call(ckv_all, kr_all, wk_r, place, wv_r, ones_row, tkv)
    tq = min(ATTN_TILE, L)
    tk = ATTN_TILE if Lk % ATTN_TILE == 0 else Lk
    att = _attn_call(q, k_all, v_all, B, L, Lk, pos0, tq, tk)
    xn, h2, route = _post_call([att], [p['mla_w_out_bf']], x, mods[2], mods[3],
                               row(p['norm_w'][1, 1]), p['w_route'][1], p['b_route'][1], L, tm)
    y = _moe(xn, h2, route, mods[3], row(p['final_norm_w']), p['moe_w1'], p['moe_w3'], p['moe_w2'],
             1, L, final=True)
    return (y.reshape(B, L, D_MODEL), jnp.swapaxes(gla_t, -1, -2)[None], gdn_new[None], conv_new[None],
            ckv.reshape(1, B, L, MLA_KV_RANK), kr.reshape(1, B, L, MLA_ROPE))


def kernel(x_prompt, x_sample, state_gla, state_gdn, state_gdn_conv, cache_mla_ckv, cache_mla_krope,
           c_prompt, c_sample, ada_w, ada_b, norm_w, final_norm_w, ab_w_in, gla_w_alpha, gla_b_alpha,
           gla_norm_w, gdn_conv_w, gdn_a_log, gdn_dt_bias, gdn_norm_w, ab_w_out, mla_w_in, mla_q_norm_w,
           mla_w_uq, mla_kv_norm_w, mla_w_ukv, mla_w_out, moe_w_group, moe_b_group, moe_w_expert,
           moe_b_expert, moe_w1, moe_w3, moe_w2):
    depth = ada_w.shape[0]
    bp, bs = x_prompt.shape[0], x_sample.shape[0]
    mods = _ada_call(jnp.concatenate([c_prompt, c_sample], axis=0),
                     ada_w.reshape(2 * depth, D_MODEL, 3 * D_MODEL), ada_b.reshape(2 * depth, 1, 3 * D_MODEL))
    w_route = _hi_lo_pair(jnp.pad(jnp.concatenate([moe_w_group, moe_w_expert], axis=-1),
                                  ((0, 0), (0, 0), (0, LANES - N_GROUPS - N_EXPERTS))))
    b_route = jnp.pad(jnp.concatenate([moe_b_group, moe_b_expert], axis=-1),
                      ((0, 0), (0, LANES - N_GROUPS - N_EXPERTS)))[:, None, :]
    p = dict(norm_w=norm_w, final_norm_w=final_norm_w, gla_b_alpha=gla_b_alpha, gla_norm_w=gla_norm_w,
             gdn_conv_w=gdn_conv_w, gdn_norm_w=gdn_norm_w, mla_q_norm_w=mla_q_norm_w,
             mla_kv_norm_w=mla_kv_norm_w, moe_w1=moe_w1, moe_w3=moe_w3, moe_w2=moe_w2,
             w_route=w_route, b_route=b_route,
             even=_prep_even(ab_w_in[0], gla_w_alpha[0], gdn_a_log[0], gdn_dt_bias[0]),
             odd=_prep_odd(mla_w_in[0], mla_w_uq[0], mla_w_ukv[0]),
             ab_w_out_bf=ab_w_out[0].astype(BF16), mla_w_out_bf=mla_w_out[0].astype(BF16))
    y_p, gla_p, gdn_p, conv_p, ckv_p, kr_p = _trunk(
        x_prompt, _mod4(mods, 0, bp), 0,
        jnp.zeros((bp, GLA_H, GLA_DK, GLA_DV), F32), jnp.zeros((bp, GDN_H, GDN_DK, GDN_DV), F32),
        jnp.zeros((bp, CONV_W - 1, GDN_CONV_DIM), F32), None, None, p)
    y_s, gla_s, gdn_s, conv_s, ckv_s, kr_s = _trunk(
        x_sample, _mod4(mods, bp, bp + bs), cache_mla_ckv.shape[2],
        state_gla[0], state_gdn[0], state_gdn_conv[0], cache_mla_ckv[0], cache_mla_krope[0], p)
    return (y_p, y_s, gla_p, gdn_p, conv_p, ckv_p, kr_p, gla_s, gdn_s, conv_s, ckv_s, kr_s)
```

```python
import functools
import math

import jax
import jax.numpy as jnp
from jax import lax
from jax.experimental import pallas as pl
from jax.experimental.pallas import tpu as pltpu

F32 = jnp.float32
BF16 = jnp.bfloat16
I32 = jnp.int32
HI = lax.Precision.HIGHEST

D_MODEL = 1024
CHUNK = 64
GLA_H, GLA_DK, GLA_DV, GLA_LR, GLA_TAU = 4, 64, 128, 16, 16.0
GDN_H, GDN_DK, GDN_DV, CONV_W = 4, 128, 128, 4
GDN_CONV_DIM = GDN_H * (2 * GDN_DK + GDN_DV)
AB_IN_WIDTHS = (GLA_H * GLA_DK, GLA_H * GLA_DK, GLA_H * GLA_DV, GLA_LR, GLA_H * GLA_DV,
                GDN_CONV_DIM, GDN_H, GDN_H, GDN_H * GDN_DV)
AB_MAIN = 3584
MLA_H, MLA_NOPE, MLA_ROPE, MLA_V = 16, 64, 32, 64
MLA_Q_RANK, MLA_KV_RANK = 384, 256
MLA_IN_R = MLA_Q_RANK + MLA_KV_RANK + 256
ROPE_THETA = 10000.0
N_GROUPS, EXPERTS_PER_GROUP, N_EXPERTS, D_EXPERT = 4, 8, 32, 512
GROUP_SHIFT = int(math.log2(EXPERTS_PER_GROUP))
CHUNK_SHIFT = int(math.log2(CHUNK))
EPS = 1e-6

LANES = 128
GLA_SUB = 16
SCAN_CHUNKS = 4
GLA_SAFE_LOG_DECAY = -60.0
MOE_ROWS = 256
MOE_TILE = 512
MOE_ALIGN = 8
MOE_BIG = 4
MOE_XCOLS = D_MODEL + LANES
ATTN_TILE = 512
NEG = -3.0e38
VMEM_LIMIT = 56 * 1024 * 1024

LANE_GA = GLA_LR
LANE_GB = GLA_LR + GDN_H

NT = (((1,), (1,)), ((), ()))
TN = (((0,), (0,)), ((), ()))


def _cp(sem, vmem=None):
    return pltpu.CompilerParams(dimension_semantics=sem, vmem_limit_bytes=vmem or VMEM_LIMIT)


def _sds(shape, dtype):
    return jax.ShapeDtypeStruct(shape, dtype)


def _sigmoid(x):
    return 1.0 / (1.0 + jnp.exp(-x))


def _softplus(x):
    return jnp.maximum(x, 0.0) + jnp.log(1.0 + jnp.exp(-jnp.abs(x)))


def _rms(x):
    return x * lax.rsqrt(jnp.mean(x * x, axis=-1, keepdims=True) + EPS)


def _norm_mod(x, nw, shift, scale):
    return (_rms(x) * nw) * (1.0 + scale) + shift


def _dot_bf(a, b):
    return jnp.dot(a.astype(BF16), b.astype(BF16), preferred_element_type=F32)


def _dot_hi(a, b):
    return jnp.dot(a, b, precision=HI, preferred_element_type=F32)


def _ada_kernel(c_ref, w_ref, b_ref, o_ref):
    c = c_ref[...]
    o_ref[0] = _dot_hi(c * _sigmoid(c), w_ref[0]) + b_ref[0]


def _ada_call(c_all, ada_w, ada_b):
    n = ada_w.shape[0]
    nb = c_all.shape[0]
    return pl.pallas_call(
        _ada_kernel, grid=(n, 3),
        in_specs=[pl.BlockSpec((nb, D_MODEL), lambda i, j: (0, 0)),
                  pl.BlockSpec((1, D_MODEL, D_MODEL), lambda i, j: (i, 0, j)),
                  pl.BlockSpec((1, 1, D_MODEL), lambda i, j: (i, 0, j))],
        out_specs=pl.BlockSpec((1, nb, D_MODEL), lambda i, j: (i, 0, j)),
        out_shape=_sds((n, nb, 3 * D_MODEL), F32),
        compiler_params=_cp(("arbitrary", "arbitrary")), name="ada",
    )(c_all, ada_w, ada_b)


def _in0_kernel(x_ref, mod_ref, nw_ref, wm_ref, ws_ref, zm_ref, zs_ref):
    h = _norm_mod(x_ref[...], nw_ref[...], mod_ref[0, 0], mod_ref[0, 1])
    h_hl = _split_bf(h)
    zm_ref[...] = jnp.dot(h_hl[0], wm_ref[...], preferred_element_type=F32).astype(BF16)
    zs_ref[...] = _dot3(h_hl, (ws_ref[0], ws_ref[1]))


def _in0_call(x, mod, nw, w_main, w_small, L, tm):
    T = x.shape[0]
    per_b = L // tm
    return pl.pallas_call(
        _in0_kernel, grid=(T // tm,),
        in_specs=[pl.BlockSpec((tm, D_MODEL), lambda i: (i, 0)),
                  pl.BlockSpec((1, 3, 1, D_MODEL), lambda i: (i // per_b, 0, 0, 0)),
                  pl.BlockSpec((1, D_MODEL), lambda i: (0, 0)),
                  pl.BlockSpec((D_MODEL, AB_MAIN), lambda i: (0, 0)),
                  pl.BlockSpec((2, D_MODEL, LANES), lambda i: (0, 0, 0))],
        out_specs=[pl.BlockSpec((tm, AB_MAIN), lambda i: (i, 0)),
                   pl.BlockSpec((tm, LANES), lambda i: (i, 0))],
        out_shape=[_sds((T, AB_MAIN), BF16), _sds((T, LANES), F32)],
        compiler_params=_cp(("arbitrary",)), name="in0",
    )(x, mod, nw, w_main, w_small)


def _gated_head_out(o, nw, gate):
    gate = gate.astype(F32)
    return (_rms(o) * nw * (gate * _sigmoid(gate))).astype(BF16)


def _gla_chunk_exact(q, k, v, b, st):
    CL = q.shape[0]
    row = lax.broadcasted_iota(I32, (GLA_SUB, 1), 0)
    bprev = jnp.zeros((1, GLA_DK), F32)
    outs = []
    for blk in range(CL // GLA_SUB):
        sl = slice(blk * GLA_SUB, (blk + 1) * GLA_SUB)
        qi, ki, vi = q[sl], k[sl], v[sl]
        brel = b[sl] - bprev
        bend = brel[GLA_SUB - 1:GLA_SUB]
        oi = lax.dot_general((qi * jnp.exp(brel)).astype(BF16), st.astype(BF16), NT,
                             preferred_element_type=F32)
        for j in range(GLA_SUB):
            d = brel - brel[j:j + 1]
            e = jnp.where(row >= j, jnp.exp(jnp.minimum(d, 0.0)), 0.0)
            a = jnp.sum(qi * ki[j:j + 1] * e, axis=-1, keepdims=True)
            oi = oi + a * vi[j:j + 1]
        outs.append(oi)
        khat = ki * jnp.exp(bend - brel)
        st = st * jnp.exp(bend) + lax.dot_general(vi.astype(BF16), khat.astype(BF16), TN,
                                                  preferred_element_type=F32)
        bprev = b[(blk + 1) * GLA_SUB - 1:(blk + 1) * GLA_SUB]
    return jnp.concatenate(outs, axis=0), st


def _gla_chunk_fast(q, k, v, b, st, causal):
    CL = q.shape[0]
    qh = (q * jnp.exp(b)).astype(BF16)
    kh = (k * jnp.exp(-b)).astype(BF16)
    vb = v.astype(BF16)
    att = jnp.where(causal, lax.dot_general(qh, kh, NT, preferred_element_type=F32), 0.0)
    o = (lax.dot_general(qh, st.astype(BF16), NT, preferred_element_type=F32)
         + jnp.dot(att.astype(BF16), vb, preferred_element_type=F32))
    blast = b[CL - 1:CL]
    kdec = (k * jnp.exp(blast - b)).astype(BF16)
    st = st * jnp.exp(blast) + lax.dot_general(vb, kdec, TN, preferred_element_type=F32)
    return o, st


def _gla_kernel(q_ref, k_ref, v_ref, gr_ref, zs_ref, wa_ref, ba_ref, nw_ref, s0_ref,
                o_ref, st_ref, st_sc, b_sc, *, CL, nch, nsteps):
    step = pl.program_id(1)

    @pl.when(step == 0)
    def _():
        st_sc[...] = s0_ref[0]

    r = lax.broadcasted_iota(I32, (CL, CL), 0)
    cc = lax.broadcasted_iota(I32, (CL, CL), 1)
    causal = r >= cc
    tri = causal.astype(F32)
    b_min = None
    for ch in range(nch):
        rows = slice(ch * CL, (ch + 1) * CL)
        pre = _dot_hi(zs_ref[rows, :], wa_ref[...]) + ba_ref[...]
        la = -_softplus(-pre) * (1.0 / GLA_TAU)
        b_ch = _dot_hi(tri, la)
        b_sc[rows, :] = b_ch
        lo = jnp.min(b_ch[CL - 1:CL, :])
        b_min = lo if b_min is None else jnp.minimum(b_min, lo)
    safe = b_min > GLA_SAFE_LOG_DECAY

    def run(chunk_fn):
        nw = nw_ref[...]
        for h in range(GLA_H):
            kc = slice(h * GLA_DK, (h + 1) * GLA_DK)
            vc = slice(h * GLA_DV, (h + 1) * GLA_DV)
            st = st_sc[h]
            for ch in range(nch):
                rows = slice(ch * CL, (ch + 1) * CL)
                q = q_ref[rows, kc].astype(F32) * GLA_DK ** -0.5
                o, st = chunk_fn(q, k_ref[rows, kc].astype(F32), v_ref[rows, vc].astype(F32),
                                 b_sc[rows, kc], st)
                o_ref[rows, vc] = _gated_head_out(o, nw, gr_ref[rows, vc])
            st_sc[h] = st

    @pl.when(safe)
    def _():
        run(functools.partial(_gla_chunk_fast, causal=causal))

    @pl.when(jnp.logical_not(safe))
    def _():
        run(_gla_chunk_exact)

    @pl.when(step == nsteps - 1)
    def _():
        st_ref[0] = st_sc[...]


def _gla_call(zm, zs, wa_pad, b_alpha, nw, s0t, B, L):
    CL = min(CHUNK, L)
    nch = SCAN_CHUNKS if L % (CL * SCAN_CHUNKS) == 0 else 1
    rows = CL * nch
    nsteps = L // rows
    T = B * L
    qk_w = GLA_H * GLA_DK
    v_w = GLA_H * GLA_DV
    row = lambda b, c: b * nsteps + c
    return pl.pallas_call(
        functools.partial(_gla_kernel, CL=CL, nch=nch, nsteps=nsteps), grid=(B, nsteps),
        in_specs=[pl.BlockSpec((rows, qk_w), lambda b, c: (row(b, c), 0)),
                  pl.BlockSpec((rows, qk_w), lambda b, c: (row(b, c), 1)),
                  pl.BlockSpec((rows, v_w), lambda b, c: (row(b, c), 1)),
                  pl.BlockSpec((rows, v_w), lambda b, c: (row(b, c), 2)),
                  pl.BlockSpec((rows, LANES), lambda b, c: (row(b, c), 0)),
                  pl.BlockSpec((LANES, qk_w), lambda b, c: (0, 0)),
                  pl.BlockSpec((1, qk_w), lambda b, c: (0, 0)),
                  pl.BlockSpec((1, GLA_DV), lambda b, c: (0, 0)),
                  pl.BlockSpec((1, GLA_H, GLA_DV, GLA_DK), lambda b, c: (b, 0, 0, 0))],
        out_specs=[pl.BlockSpec((rows, v_w), lambda b, c: (row(b, c), 0)),
                   pl.BlockSpec((1, GLA_H, GLA_DV, GLA_DK), lambda b, c: (b, 0, 0, 0))],
        out_shape=[_sds((T, v_w), BF16), _sds((B, GLA_H, GLA_DV, GLA_DK), F32)],
        scratch_shapes=[pltpu.VMEM((GLA_H, GLA_DV, GLA_DK), F32), pltpu.VMEM((rows, qk_w), F32)],
        compiler_params=_cp(("arbitrary", "arbitrary")), name="gla",
    )(zm, zm, zm, zm, zs, wa_pad, b_alpha, nw, s0t)


def _split_bf(a):
    hi = a.astype(BF16)
    return hi, (a - hi.astype(F32)).astype(BF16)


def _dot3(a_hl, b_hl):
    (ah, al), (bh, bl) = a_hl, b_hl
    d = functools.partial(jnp.dot, preferred_element_type=F32)
    return d(ah, bh) + d(ah, bl) + d(al, bh)


def _gdn_kernel(qkv_ref, zs_ref, gg_ref, cw_ref, alog_ref, dtb_ref, nw_ref, hist_ref, s0_ref,
                o_ref, s_ref, cv_sc, st_sc, *, CL, nch, nsteps):
    step = pl.program_id(1)
    R = CL * nch

    @pl.when(step == 0)
    def _():
        cv_sc[0:8, :] = hist_ref[0]
        st_sc[...] = s0_ref[0]

    x = qkv_ref[...].astype(F32)
    cv_sc[8:8 + R, :] = x
    cw = cw_ref[...]
    conv = (cv_sc[5:5 + R, :] * cw[0:1] + cv_sc[6:6 + R, :] * cw[1:2]
            + cv_sc[7:7 + R, :] * cw[2:3] + x * cw[3:4])
    cv_sc[0:8, :] = x[R - 8:R]
    conv = conv * _sigmoid(conv)

    zs = zs_ref[...]
    g_all = -jnp.exp(alog_ref[...]) * _softplus(zs + dtb_ref[...])
    beta_all = _sigmoid(zs)
    r = lax.broadcasted_iota(I32, (CL, CL), 0)
    cc = lax.broadcasted_iota(I32, (CL, CL), 1)
    tri = (r >= cc).astype(F32)
    eye = (r == cc).astype(F32)
    nw = nw_ref[...]
    qk_w = GDN_H * GDN_DK

    units = [(ch, h) for ch in range(nch) for h in range(GDN_H)]
    stage = {}
    for ch in range(nch):
        rows = slice(ch * CL, (ch + 1) * CL)
        gam = _dot_hi(tri, g_all[rows])
        gam_t = gam.T
        for h in range(GDN_H):
            q = conv[rows, h * GDN_DK:(h + 1) * GDN_DK]
            k = conv[rows, qk_w + h * GDN_DK:qk_w + (h + 1) * GDN_DK]
            v = conv[rows, 2 * qk_w + h * GDN_DV:2 * qk_w + (h + 1) * GDN_DV]
            q = q * lax.rsqrt(jnp.sum(q * q, axis=-1, keepdims=True) + EPS) * GDN_DK ** -0.5
            k = k * lax.rsqrt(jnp.sum(k * k, axis=-1, keepdims=True) + EPS)
            gcol = gam[:, LANE_GA + h:LANE_GA + h + 1]
            grow = gam_t[LANE_GA + h:LANE_GA + h + 1, :]
            bcol = beta_all[rows, LANE_GB + h:LANE_GB + h + 1]
            dec = jnp.where(r >= cc, jnp.exp(jnp.minimum(gcol - grow, 0.0)), 0.0)
            eg = jnp.exp(gcol)
            glast = gcol[CL - 1:CL]
            stage[ch, h] = dict(
                kb=k.astype(BF16), qb=q.astype(BF16), dec=dec, bcol=bcol,
                ub=(v * bcol).astype(BF16), wb=(k * (bcol * eg)).astype(BF16),
                qe=(q * eg).astype(BF16), kdec=(k * jnp.exp(glast - gcol)).astype(BF16),
                elast=jnp.exp(glast))
    for un in units:
        d = stage[un]
        kk = lax.dot_general(d['kb'], d['kb'], NT, preferred_element_type=F32)
        d['pw'] = jnp.where(r > cc, d['bcol'] * kk * d['dec'], 0.0)
        d['t'] = eye - d['pw']
    for _ in range(int(math.log2(CL)) - 1):
        for un in units:
            d = stage[un]
            hl = _split_bf(d['pw'])
            d['pw'] = _dot3(hl, hl)
        for un in units:
            d = stage[un]
            d['t'] = d['t'] + _dot3(_split_bf(d['t']), _split_bf(d['pw']))
    pre = {}
    for un in units:
        d = stage[un]
        tb = d['t'].astype(BF16)
        u = jnp.dot(tb, d['ub'], preferred_element_type=F32)
        w = jnp.dot(tb, d['wb'], preferred_element_type=F32).astype(BF16)
        qk = (lax.dot_general(d['qb'], d['kb'], NT, preferred_element_type=F32) * d['dec']).astype(BF16)
        pre[un] = (u, w, qk, d['qe'], d['kdec'], d['elast'])

    for h in range(GDN_H):
        vc = slice(h * GDN_DV, (h + 1) * GDN_DV)
        st = st_sc[h]
        for ch in range(nch):
            rows = slice(ch * CL, (ch + 1) * CL)
            u, w, qk, qe, kdec, elast = pre[ch, h]
            stb = st.astype(BF16)
            delta = u - jnp.dot(w, stb, preferred_element_type=F32)
            db = delta.astype(BF16)
            o = (jnp.dot(qe, stb, preferred_element_type=F32)
                 + jnp.dot(qk, db, preferred_element_type=F32))
            st = st * elast + lax.dot_general(kdec, db, TN, preferred_element_type=F32)
            o_ref[rows, vc] = _gated_head_out(o, nw, gg_ref[rows, vc])
        st_sc[h] = st

    @pl.when(step == nsteps - 1)
    def _():
        s_ref[0] = st_sc[...]


def _gdn_call(zm, zs, conv_w, alog_v, dtb_v, nw, hist8, s0, B, L):
    CL = min(CHUNK, L)
    nch = SCAN_CHUNKS if L % (CL * SCAN_CHUNKS) == 0 else 1
    rows = CL * nch
    nsteps = L // rows
    T = B * L
    v_w = GDN_H * GDN_DV
    row = lambda b, c: b * nsteps + c
    return pl.pallas_call(
        functools.partial(_gdn_kernel, CL=CL, nch=nch, nsteps=nsteps), grid=(B, nsteps),
        in_specs=[pl.BlockSpec((rows, GDN_CONV_DIM), lambda b, c: (row(b, c), 1)),
                  pl.BlockSpec((rows, LANES), lambda b, c: (row(b, c), 0)),
                  pl.BlockSpec((rows, v_w), lambda b, c: (row(b, c), 6)),
                  pl.BlockSpec((CONV_W, GDN_CONV_DIM), lambda b, c: (0, 0)),
                  pl.BlockSpec((1, LANES), lambda b, c: (0, 0)),
                  pl.BlockSpec((1, LANES), lambda b, c: (0, 0)),
                  pl.BlockSpec((1, GDN_DV), lambda b, c: (0, 0)),
                  pl.BlockSpec((1, 8, GDN_CONV_DIM), lambda b, c: (b, 0, 0)),
                  pl.BlockSpec((1, GDN_H, GDN_DK, GDN_DV), lambda b, c: (b, 0, 0, 0))],
        out_specs=[pl.BlockSpec((rows, v_w), lambda b, c: (row(b, c), 0)),
                   pl.BlockSpec((1, GDN_H, GDN_DK, GDN_DV), lambda b, c: (b, 0, 0, 0))],
        out_shape=[_sds((T, v_w), BF16), _sds((B, GDN_H, GDN_DK, GDN_DV), F32)],
        scratch_shapes=[pltpu.VMEM((8 + rows, GDN_CONV_DIM), F32),
                        pltpu.VMEM((GDN_H, GDN_DK, GDN_DV), F32)],
        compiler_params=_cp(("arbitrary", "arbitrary")), name="gdn",
    )(zm, zs, zm, conv_w, alog_v, dtb_v, nw, hist8, s0)


def _post_kernel(*refs, n_in):
    a_refs = refs[:n_in]
    w_refs = refs[n_in:2 * n_in]
    x_ref, mod1_ref, mod2_ref, nw2_ref, wr_ref, br_ref, xn_ref, h2_ref, rt_ref = refs[2 * n_in:]
    acc = jnp.dot(a_refs[0][...], w_refs[0][...], preferred_element_type=F32)
    for a_ref, w_ref in zip(a_refs[1:], w_refs[1:]):
        acc = acc + jnp.dot(a_ref[...], w_ref[...], preferred_element_type=F32)
    xn = x_ref[...] + mod1_ref[0, 2] * acc
    xn_ref[...] = xn
    h2 = _norm_mod(xn, nw2_ref[...], mod2_ref[0, 0], mod2_ref[0, 1])
    h2_ref[...] = h2
    logits = _dot3(_split_bf(h2), (wr_ref[0], wr_ref[1])) + br_ref[...]
    lane = lax.broadcasted_iota(I32, logits.shape, 1)
    is_g = lane < N_GROUPS
    gl = jnp.where(is_g, logits, NEG)
    gmax = jnp.max(gl, axis=-1, keepdims=True)
    g_sel = jnp.min(jnp.where(gl == gmax, lane, LANES), axis=-1, keepdims=True)
    p_g = 1.0 / jnp.sum(jnp.where(is_g, jnp.exp(logits - gmax), 0.0), axis=-1, keepdims=True)
    in_grp = jnp.logical_and(
        jnp.logical_and(lane >= N_GROUPS, lane < N_GROUPS + N_EXPERTS),
        jnp.right_shift(lane - N_GROUPS, GROUP_SHIFT) == g_sel)
    el = jnp.where(in_grp, logits, NEG)
    v1 = jnp.max(el, axis=-1, keepdims=True)
    i1 = jnp.min(jnp.where(jnp.logical_and(in_grp, el == v1), lane, LANES), axis=-1, keepdims=True)
    rest = jnp.logical_and(in_grp, lane != i1)
    el2 = jnp.where(rest, logits, NEG)
    v2 = jnp.max(el2, axis=-1, keepdims=True)
    i2 = jnp.min(jnp.where(jnp.logical_and(rest, el2 == v2), lane, LANES), axis=-1, keepdims=True)
    ex = jnp.exp(v2 - v1)
    w1 = 1.0 / (1.0 + ex)
    w2 = ex * w1
    e1 = (i1 - N_GROUPS).astype(F32)
    e2 = (i2 - N_GROUPS).astype(F32)
    rt_ref[...] = jnp.where(lane == 0, e1, jnp.where(lane == 1, e2, jnp.where(
        lane == 2, p_g * w1, jnp.where(lane == 3, p_g * w2, 0.0))))


def _post_call(a_list, w_list, x, mod1, mod2, nw2, w_route, b_route, L, tm):
    T = x.shape[0]
    per_b = L // tm
    n_in = len(a_list)
    tok = lambda i: (i, 0)
    const = lambda i: (0, 0)
    modmap = lambda i: (i // per_b, 0, 0, 0)
    in_specs = ([pl.BlockSpec((tm, a.shape[1]), tok) for a in a_list]
                + [pl.BlockSpec(w.shape, const) for w in w_list]
                + [pl.BlockSpec((tm, D_MODEL), tok),
                   pl.BlockSpec((1, 3, 1, D_MODEL), modmap),
                   pl.BlockSpec((1, 3, 1, D_MODEL), modmap),
                   pl.BlockSpec((1, D_MODEL), const),
                   pl.BlockSpec((2, D_MODEL, LANES), lambda i: (0, 0, 0)),
                   pl.BlockSpec((1, LANES), const)])
    return pl.pallas_call(
        functools.partial(_post_kernel, n_in=n_in), grid=(T // tm,),
        in_specs=in_specs,
        out_specs=[pl.BlockSpec((tm, D_MODEL), tok), pl.BlockSpec((tm, D_MODEL), tok),
                   pl.BlockSpec((tm, LANES), tok)],
        out_shape=[_sds((T, D_MODEL), F32), _sds((T, D_MODEL), F32), _sds((T, LANES), F32)],
        compiler_params=_cp(("arbitrary",)), name="post",
    )(*a_list, *w_list, x, mod1, mod2, nw2, w_route, b_route)


def _rank_kernel(rt_ref, pos_ref, post_ref, cnt_ref):
    rt = rt_ref[...]
    tm = rt.shape[0]
    lane = lax.broadcasted_iota(I32, rt.shape, 1)
    o1 = lane == rt[:, 0:1].astype(I32)
    o2 = lane == rt[:, 1:2].astype(I32)
    onehot = jnp.where(o1, 1.0, 0.0) + jnp.where(o2, 1.0, 0.0)
    r = lax.broadcasted_iota(I32, (tm, tm), 0)
    cc = lax.broadcasted_iota(I32, (tm, tm), 1)
    before = jnp.dot(jnp.where(r > cc, 1.0, 0.0).astype(BF16), onehot.astype(BF16),
                     preferred_element_type=F32)
    cnt = jnp.sum(onehot, axis=0, keepdims=True)
    er = lax.broadcasted_iota(I32, (LANES, LANES), 0)
    ec = lax.broadcasted_iota(I32, (LANES, LANES), 1)
    cnt8 = jnp.floor((cnt + (MOE_ALIGN - 1)) * (1.0 / MOE_ALIGN)) * MOE_ALIGN
    start = _dot_hi(jnp.broadcast_to(cnt8, (8, LANES)), jnp.where(er < ec, 1.0, 0.0))[0:1]
    where = before + start
    p1 = jnp.sum(jnp.where(o1, where, 0.0), axis=-1, keepdims=True)
    p2 = jnp.sum(jnp.where(o2, where, 0.0), axis=-1, keepdims=True)
    pos = jnp.where(lane == 0, p1, jnp.where(lane == 1, p2, 0.0))
    pos_ref[...] = pos
    post_ref[0] = pos.T[0:8, :].astype(I32)
    cnt_ref[0] = cnt


def _rank_call(route, tm):
    T = route.shape[0]
    nt = T // tm
    return pl.pallas_call(
        _rank_kernel, grid=(nt,),
        in_specs=[pl.BlockSpec((tm, LANES), lambda i: (i, 0))],
        out_specs=[pl.BlockSpec((tm, LANES), lambda i: (i, 0)),
                   pl.BlockSpec((1, 8, tm), lambda i: (i, 0, 0)),
                   pl.BlockSpec((1, 1, LANES), lambda i: (i, 0, 0))],
        out_shape=[_sds((T, LANES), F32), _sds((nt, 8, tm), I32), _sds((nt, 1, LANES), F32)],
        compiler_params=_cp(("arbitrary",)), name="moe_rank",
    )(route)


def _segment_dmas(tile, n_ref, src_ref, dst_ref, copy, act):
    big_rows = MOE_BIG * MOE_ALIGN

    def per_expert(e, carry):
        idx = tile * N_EXPERTS + e
        u, s, d = n_ref[idx], src_ref[idx], dst_ref[idx]
        n_big = u // MOE_BIG

        def big(c, carry):
            o = c * big_rows
            act(copy(pl.multiple_of(s + o, MOE_ALIGN), pl.multiple_of(d + o, MOE_ALIGN), big_rows))
            return carry

        lax.fori_loop(0, n_big, big, 0)

        def small(c, carry):
            o = n_big * big_rows + c * MOE_ALIGN
            act(copy(pl.multiple_of(s + o, MOE_ALIGN), pl.multiple_of(d + o, MOE_ALIGN), MOE_ALIGN))
            return carry

        lax.fori_loop(0, u - n_big * MOE_BIG, small, 0)
        return carry

    lax.fori_loop(0, N_EXPERTS, per_expert, 0)


def _disp_kernel(n_ref, src_ref, dst_ref, last_ref, h_ref, rt_ref, post_ref, xg_hbm, xs_sc, sems, *, tm, nt):
    i = pl.program_id(0)
    slot = i % 2
    cap = xs_sc.shape[1]

    def runs(tile, sl, act):
        copy = lambda s, d, n: pltpu.make_async_copy(
            xs_sc.at[sl, pl.ds(s, n), :], xg_hbm.at[pl.ds(d, n), :], sems.at[sl])
        _segment_dmas(tile, n_ref, src_ref, dst_ref, copy, act)

    @pl.when(i == 0)
    def _():
        xs_sc[0, 0:MOE_ROWS, :] = jnp.zeros((MOE_ROWS, MOE_XCOLS), F32)
        zero_block = lambda blk: pltpu.make_async_copy(
            xs_sc.at[0, pl.ds(0, MOE_ROWS), :], xg_hbm.at[pl.ds(blk * MOE_ROWS, MOE_ROWS), :], sems.at[0])
        n_blocks = xg_hbm.shape[0] // MOE_ROWS

        def fill(e, act):
            @pl.when(last_ref[e] >= 0)
            def _():
                act(zero_block(last_ref[e]))

        def start_all(e, carry):
            fill(e, lambda cp: cp.start())
            return carry

        def wait_all(e, carry):
            fill(e, lambda cp: cp.wait())
            return carry

        def start_tail(blk, carry):
            zero_block(blk).start()
            return carry

        def wait_tail(blk, carry):
            zero_block(blk).wait()
            return carry

        lax.fori_loop(0, N_EXPERTS, start_all, 0)
        lax.fori_loop(last_ref[N_EXPERTS], n_blocks, start_tail, 0)
        lax.fori_loop(0, N_EXPERTS, wait_all, 0)
        lax.fori_loop(last_ref[N_EXPERTS], n_blocks, wait_tail, 0)

    @pl.when(i >= 2)
    def _():
        runs(i - 2, slot, lambda cp: cp.wait())

    post = post_ref[0]
    rows = lax.broadcasted_iota(I32, (cap, tm), 0)
    p1 = jnp.where(rows == post[0:1, :], 1.0, 0.0).astype(BF16)
    p2 = jnp.where(rows == post[1:2, :], 1.0, 0.0).astype(BF16)
    xs_sc[slot, :, 0:D_MODEL] = jnp.dot(p1 + p2, h_ref[...].astype(BF16), preferred_element_type=F32)
    rt = rt_ref[...]
    lane = lax.broadcasted_iota(I32, rt.shape, 1)

    def hi_lo(col):
        hi = col.astype(BF16).astype(F32)
        return jnp.where(lane == 0, hi, jnp.where(lane == 1, col - hi, 0.0)).astype(BF16)

    xs_sc[slot, :, D_MODEL:] = (jnp.dot(p1, hi_lo(rt[:, 2:3]), preferred_element_type=F32)
                                + jnp.dot(p2, hi_lo(rt[:, 3:4]), preferred_element_type=F32))
    runs(i, slot, lambda cp: cp.start())

    @pl.when(i == nt - 1)
    def _():
        if nt > 1:
            runs(i - 1, 1 - slot, lambda cp: cp.wait())
        runs(i, slot, lambda cp: cp.wait())


def _disp_call(seg_n, seg_src, seg_dst, last_blk, h2, route, post, n_rows, tm):
    T = h2.shape[0]
    nt = T // tm
    return pl.pallas_call(
        functools.partial(_disp_kernel, tm=tm, nt=nt),
        grid_spec=pltpu.PrefetchScalarGridSpec(
            num_scalar_prefetch=4, grid=(nt,),
            in_specs=[pl.BlockSpec((tm, D_MODEL), lambda i, *_: (i, 0)),
                      pl.BlockSpec((tm, LANES), lambda i, *_: (i, 0)),
                      pl.BlockSpec((1, 8, tm), lambda i, *_: (i, 0, 0))],
            out_specs=pl.BlockSpec(memory_space=pl.ANY),
            scratch_shapes=[pltpu.VMEM((2, _moe_cap(tm), MOE_XCOLS), F32), pltpu.SemaphoreType.DMA((2,))]),
        out_shape=_sds((n_rows, MOE_XCOLS), F32),
        compiler_params=_cp(("arbitrary",)), name="moe_dispatch",
    )(seg_n, seg_src, seg_dst, last_blk, h2, route, post)


def _expert_kernel(be_ref, nu_ref, x_ref, w1_ref, w3_ref, w2_ref, y_ref, w1b, w3b, w2b):
    i = pl.program_id(0)
    used = i < nu_ref[0]
    fresh = jnp.logical_or(i == 0, be_ref[i] != be_ref[jnp.maximum(i - 1, 0)])

    @pl.when(jnp.logical_and(used, fresh))
    def _():
        w1b[...] = w1_ref[0, 0].astype(BF16)
        w3b[...] = w3_ref[0, 0].astype(BF16)
        w2b[...] = w2_ref[0, 0].astype(BF16)

    @pl.when(used)
    def _():
        x = x_ref[:, 0:D_MODEL].astype(BF16)
        a = jnp.dot(x, w1b[...], preferred_element_type=F32)
        g = jnp.dot(x, w3b[...], preferred_element_type=F32)
        hm = (a * _sigmoid(a) * g).astype(BF16)
        row_w = x_ref[:, D_MODEL:D_MODEL + 1] + x_ref[:, D_MODEL + 1:D_MODEL + 2]
        y_ref[...] = jnp.dot(hm, w2b[...], preferred_element_type=F32) * row_w

    @pl.when(jnp.logical_not(used))
    def _():
        y_ref[...] = jnp.zeros(y_ref.shape, F32)


def _expert_call(blk_e, n_used, xg, w1, w3, w2, layer):
    P = xg.shape[0]
    nb = P // MOE_ROWS
    rowmap = lambda i, be, nu: (jnp.minimum(i, nu[0] - 1), 0)
    wmap = lambda i, be, nu: (layer, be[i], 0, 0)
    return pl.pallas_call(
        _expert_kernel,
        grid_spec=pltpu.PrefetchScalarGridSpec(
            num_scalar_prefetch=2, grid=(nb,),
            in_specs=[pl.BlockSpec((MOE_ROWS, MOE_XCOLS), rowmap),
                      pl.BlockSpec((1, 1, D_MODEL, D_EXPERT), wmap),
                      pl.BlockSpec((1, 1, D_MODEL, D_EXPERT), wmap),
                      pl.BlockSpec((1, 1, D_EXPERT, D_MODEL), wmap)],
            out_specs=pl.BlockSpec((MOE_ROWS, D_MODEL), lambda i, be, nu: (i, 0)),
            scratch_shapes=[pltpu.VMEM((D_MODEL, D_EXPERT), BF16),
                            pltpu.VMEM((D_MODEL, D_EXPERT), BF16),
                            pltpu.VMEM((D_EXPERT, D_MODEL), BF16)]),
        out_shape=_sds((P, D_MODEL), F32),
        compiler_params=_cp(("arbitrary",)), name="moe_experts",
    )(blk_e, n_used, xg, w1, w3, w2)


def _comb_kernel(n_ref, src_ref, dst_ref, x_ref, pos_ref, mod_ref, fnw_ref, y_hbm, o_ref, ys_sc, sems,
                 *, tm, final, nt):
    i = pl.program_id(0)
    slot = i % 2
    cap = ys_sc.shape[1]

    def runs(tile, sl, act):
        copy = lambda s, d, n: pltpu.make_async_copy(
            y_hbm.at[pl.ds(d, n), :], ys_sc.at[sl, pl.ds(s, n), :], sems.at[sl])
        _segment_dmas(tile, n_ref, src_ref, dst_ref, copy, act)

    def begin(tile, sl):
        ys_sc[sl, 2 * tm:cap, :] = jnp.zeros((cap - 2 * tm, D_MODEL), F32)
        runs(tile, sl, lambda cp: cp.start())

    @pl.when(i == 0)
    def _():
        begin(0, 0)

    @pl.when(i + 1 < nt)
    def _():
        begin(i + 1, 1 - slot)

    runs(i, slot, lambda cp: cp.wait())
    pos = pos_ref[...]
    cols = lax.broadcasted_iota(I32, (tm, cap), 1)
    pick = (jnp.where(cols == pos[:, 0:1].astype(I32), 1.0, 0.0)
            + jnp.where(cols == pos[:, 1:2].astype(I32), 1.0, 0.0)).astype(BF16)
    y = jnp.dot(pick, ys_sc[slot].astype(BF16), preferred_element_type=F32)
    out = x_ref[...] + mod_ref[0, 2] * y
    if final:
        out = _rms(out) * fnw_ref[...]
    o_ref[...] = out


def _comb_call(seg_n, seg_src, seg_dst, xn, pos, mod, fnw, yb, L, tm, final):
    T = xn.shape[0]
    nt = T // tm
    per_b = L // tm
    return pl.pallas_call(
        functools.partial(_comb_kernel, tm=tm, final=final, nt=nt),
        grid_spec=pltpu.PrefetchScalarGridSpec(
            num_scalar_prefetch=3, grid=(nt,),
            in_specs=[pl.BlockSpec((tm, D_MODEL), lambda i, *_: (i, 0)),
                      pl.BlockSpec((tm, LANES), lambda i, *_: (i, 0)),
                      pl.BlockSpec((1, 3, 1, D_MODEL), lambda i, *_: (i // per_b, 0, 0, 0)),
                      pl.BlockSpec((1, D_MODEL), lambda i, *_: (0, 0)),
                      pl.BlockSpec(memory_space=pl.ANY)],
            out_specs=pl.BlockSpec((tm, D_MODEL), lambda i, *_: (i, 0)),
            scratch_shapes=[pltpu.VMEM((2, _moe_cap(tm), D_MODEL), F32), pltpu.SemaphoreType.DMA((2,))]),
        out_shape=_sds((T, D_MODEL), F32),
        compiler_params=_cp(("arbitrary",)), name="moe_combine",
    )(seg_n, seg_src, seg_dst, xn, pos, mod, fnw, yb)


def _moe_cap(tm):
    return max(2 * tm + N_EXPERTS * MOE_ALIGN, MOE_ROWS)


def _moe(xn, h2, route, mod, fnw, w1, w3, w2, layer, L, final):
    T = xn.shape[0]
    tm = min(MOE_TILE, L)
    nt = T // tm
    pos, post, counts = _rank_call(route, tm)
    cnt = counts[:, 0, :N_EXPERTS].astype(I32)
    cnt = (cnt + MOE_ALIGN - 1) // MOE_ALIGN * MOE_ALIGN
    total = jnp.sum(cnt, axis=0)
    padded = (total + MOE_ROWS - 1) // MOE_ROWS * MOE_ROWS
    pend = jnp.cumsum(padded)
    seg_dst = (pend - padded)[None, :] + jnp.cumsum(cnt, axis=0) - cnt
    seg_src = jnp.cumsum(cnt, axis=1) - cnt
    nb = -(-(2 * T + nt * N_EXPERTS * (MOE_ALIGN - 1) + N_EXPERTS * (MOE_ROWS - 1)) // MOE_ROWS)
    blk_e = jnp.minimum(jnp.sum(pend[None, :] <= (jnp.arange(nb, dtype=I32) * MOE_ROWS)[:, None], axis=1),
                        N_EXPERTS - 1).astype(I32)
    n_used = (pend[N_EXPERTS - 1:] // MOE_ROWS).astype(I32)
    last_blk = jnp.concatenate([jnp.where(padded > 0, pend // MOE_ROWS - 1, -1).astype(I32), n_used])
    flat = lambda a: a.reshape(nt * N_EXPERTS).astype(I32)
    seg = (flat(cnt // MOE_ALIGN), flat(seg_src), flat(seg_dst))
    xg = _disp_call(*seg, last_blk, h2, route, post, nb * MOE_ROWS, tm)
    yb = _expert_call(blk_e, n_used, xg, w1, w3, w2, layer)
    return _comb_call(*seg, xn, pos, mod, fnw, yb, L, tm, final)


def _in1_kernel(x_ref, mod_ref, nw_ref, win_ref, qnw_ref, wuq_ref, kvnw_ref, tq_ref, tk_ref,
                q_ref, ckv_ref, kr_ref):
    h = _norm_mod(x_ref[...], nw_ref[...], mod_ref[0, 0], mod_ref[0, 1])
    zz = _dot_bf(h, win_ref[...])
    cqn = _rms(zz[:, :MLA_Q_RANK]) * qnw_ref[...]
    q = _dot_bf(cqn, wuq_ref[...])
    tq = tq_ref[...]
    for hh in range(MLA_H):
        q_ref[:, hh * LANES:(hh + 1) * LANES] = (q[:, hh * LANES:(hh + 1) * LANES] * tq).astype(BF16)
    c0 = MLA_Q_RANK
    ckv_ref[...] = _rms(zz[:, c0:c0 + MLA_KV_RANK]) * kvnw_ref[...]
    c1 = c0 + MLA_KV_RANK
    tk = tk_ref[...]
    kr_ref[...] = (zz[:, c1:c1 + MLA_ROPE] * tk[:, :MLA_ROPE]
                   + zz[:, c1 + LANES:c1 + LANES + MLA_ROPE] * tk[:, MLA_ROPE:])


def _in1_call(x, mod, nw, w_in_r, q_nw, w_uq_r, kv_nw, tab_q, tab_k, L, tm):
    T = x.shape[0]
    per_b = L // tm
    tok = lambda i: (i, 0)
    const = lambda i: (0, 0)
    pos = lambda i: (i % per_b, 0)
    return pl.pallas_call(
        _in1_kernel, grid=(T // tm,),
        in_specs=[pl.BlockSpec((tm, D_MODEL), tok),
                  pl.BlockSpec((1, 3, 1, D_MODEL), lambda i: (i // per_b, 0, 0, 0)),
                  pl.BlockSpec((1, D_MODEL), const),
                  pl.BlockSpec((D_MODEL, MLA_IN_R), const),
                  pl.BlockSpec((1, MLA_Q_RANK), const),
                  pl.BlockSpec((MLA_Q_RANK, MLA_H * LANES), const),
                  pl.BlockSpec((1, MLA_KV_RANK), const),
                  pl.BlockSpec((tm, LANES), pos),
                  pl.BlockSpec((tm, 2 * MLA_ROPE), pos)],
        out_specs=[pl.BlockSpec((tm, MLA_H * LANES), tok),
                   pl.BlockSpec((tm, MLA_KV_RANK), tok),
                   pl.BlockSpec((tm, MLA_ROPE), tok)],
        out_shape=[_sds((T, MLA_H * LANES), BF16), _sds((T, MLA_KV_RANK), F32),
                   _sds((T, MLA_ROPE), F32)],
        compiler_params=_cp(("arbitrary",)), name="in1",
    )(x, mod, nw, w_in_r, q_nw, w_uq_r, kv_nw, tab_q, tab_k)


def _kv_kernel(ckv_ref, kr_ref, wk_ref, pe_ref, wv_ref, one_ref, k_ref, v_ref):
    c = ckv_ref[...].astype(BF16)
    k_ref[...] = (jnp.dot(c, wk_ref[...], preferred_element_type=F32)
                  + jnp.dot(kr_ref[...].astype(BF16), pe_ref[...], preferred_element_type=F32)).astype(BF16)
    v_ref[...] = (jnp.dot(c, wv_ref[...], preferred_element_type=F32) + one_ref[...]).astype(BF16)


def _kv_call(ckv, kr, wk_r, place, wv_r, ones_row, tm):
    T = ckv.shape[0]
    tok = lambda i: (i, 0)
    const = lambda i: (0, 0)
    wide = MLA_H * LANES
    return pl.pallas_call(
        _kv_kernel, grid=(T // tm,),
        in_specs=[pl.BlockSpec((tm, MLA_KV_RANK), tok),
                  pl.BlockSpec((tm, MLA_ROPE), tok),
                  pl.BlockSpec((MLA_KV_RANK, wide), const),
                  pl.BlockSpec((MLA_ROPE, wide), const),
                  pl.BlockSpec((MLA_KV_RANK, wide), const),
                  pl.BlockSpec((1, wide), const)],
        out_specs=[pl.BlockSpec((tm, wide), tok), pl.BlockSpec((tm, wide), tok)],
        out_shape=[_sds((T, wide), BF16), _sds((T, wide), BF16)],
        compiler_params=_cp(("arbitrary",)), name="mla_kv",
    )(ckv, kr, wk_r, place, wv_r, ones_row)


def _attn_kernel(q_ref, k_ref, v_ref, o_ref, m0, m1, a0, a1, s0, s1, *, tq, tk, Lk, pos0):
    m_scs, acc_scs, s_scs = (m0, m1), (a0, a1), (s0, s1)
    q_lo = pos0 + pl.program_id(2) * tq
    k_hi = jnp.minimum(((q_lo + tq - 1) // CHUNK + 1) * CHUNK, Lk)
    n_blk = (k_hi + tk - 1) // tk
    n_full = jnp.minimum(((q_lo // CHUNK + 1) * CHUNK) // tk, n_blk)
    q_chunk = jnp.right_shift(q_lo + lax.broadcasted_iota(I32, (tq, tk), 0), CHUNK_SHIFT)
    k_iota = lax.broadcasted_iota(I32, (tq, tk), 1)

    def scores(hh, k0):
        hc = slice(hh * LANES, (hh + 1) * LANES)
        return lax.dot_general(q_ref[:, hc], k_ref[pl.ds(k0, tk), hc], NT, preferred_element_type=F32)

    for hh in range(2):
        m_scs[hh][...] = jnp.full((tq, LANES), NEG, F32)
        acc_scs[hh][...] = jnp.zeros((tq, LANES), F32)
        s_scs[hh][...] = scores(hh, 0)

    def step(j, masked, look_ahead):
        k0 = pl.multiple_of(j * tk, tk)
        if look_ahead:
            ahead = [scores(hh, pl.multiple_of((j + 1) * tk, tk)) for hh in range(2)]
        for hh in range(2):
            hc = slice(hh * LANES, (hh + 1) * LANES)
            s = s_scs[hh][...]
            if masked:
                s = jnp.where(q_chunk >= jnp.right_shift(k0 + k_iota, CHUNK_SHIFT), s, NEG)
            m_prev = m_scs[hh][...]
            m_new = jnp.maximum(m_prev, jnp.max(s, axis=-1, keepdims=True))
            if tk % LANES == 0:
                p = jnp.exp2(s - jnp.concatenate([m_new] * (tk // LANES), axis=1))
            else:
                p = jnp.exp2(s - m_new[:, 0:1])
            acc_scs[hh][...] = (jnp.exp2(m_prev - m_new) * acc_scs[hh][...]
                                + jnp.dot(p.astype(BF16), v_ref[pl.ds(k0, tk), hc],
                                          preferred_element_type=F32))
            m_scs[hh][...] = m_new
        if look_ahead:
            for hh in range(2):
                s_scs[hh][...] = ahead[hh]

    def loop(lo, hi, masked, unroll=1):
        def body(jj, carry):
            for u in range(unroll):
                step(lo + jj * unroll + u, masked, True)
            return carry
        lax.fori_loop(0, (hi - lo) // unroll, body, 0)
        return lo + (hi - lo) // unroll * unroll

    n_pre = jnp.minimum(n_full, n_blk - 1)
    done = loop(0, n_pre, False, unroll=2)
    loop(done, n_pre, False)
    loop(n_full, n_blk - 1, True)

    @pl.when(n_full < n_blk)
    def _():
        step(n_blk - 1, True, False)

    @pl.when(n_full == n_blk)
    def _():
        step(n_blk - 1, False, False)

    outs = []
    for hh in range(2):
        acc = acc_scs[hh][...]
        outs.append(acc[:, :MLA_V] / acc[:, MLA_V:MLA_V + 1])
    o_ref[...] = jnp.concatenate(outs, axis=1).astype(BF16)


def _attn_call(q, k, v, B, Lq, Lk, pos0, tq, tk):
    nq = Lq // tq
    return pl.pallas_call(
        functools.partial(_attn_kernel, tq=tq, tk=tk, Lk=Lk, pos0=pos0), grid=(B, MLA_H // 2, nq),
        in_specs=[pl.BlockSpec((tq, 2 * LANES), lambda b, hp, i: (b * nq + i, hp)),
                  pl.BlockSpec((Lk, 2 * LANES), lambda b, hp, i: (b, hp)),
                  pl.BlockSpec((Lk, 2 * LANES), lambda b, hp, i: (b, hp))],
        out_specs=pl.BlockSpec((tq, 2 * MLA_V), lambda b, hp, i: (b * nq + i, hp)),
        out_shape=_sds((B * Lq, MLA_H * MLA_V), BF16),
        scratch_shapes=[pltpu.VMEM((tq, LANES), F32)] * 4 + [pltpu.VMEM((tq, tk), F32)] * 2,
        compiler_params=_cp(("arbitrary", "arbitrary", "arbitrary")), name="mla_attn",
    )(q, k, v)


def _split_cols(w, widths):
    offs = [0]
    for n in widths:
        offs.append(offs[-1] + n)
    return [w[:, offs[i]:offs[i + 1]] for i in range(len(widths))]


def _hi_lo_pair(w):
    hi = w.astype(BF16)
    return jnp.stack([hi, (w - hi.astype(F32)).astype(BF16)], axis=-3)


def _prep_even(ab_w_in, gla_w_alpha, gdn_a_log, gdn_dt_bias):
    gq, gk, gv, glr, gr, qkv, ga, gb, gg = _split_cols(ab_w_in, AB_IN_WIDTHS)
    w_main = jnp.concatenate([gq, gk, gv, gr, qkv, gg], axis=1).astype(BF16)
    small = jnp.concatenate([glr, ga, gb], axis=1)
    w_small = _hi_lo_pair(jnp.pad(small, ((0, 0), (0, LANES - small.shape[1]))))
    wa_pad = jnp.pad(gla_w_alpha, ((0, LANES - GLA_LR), (0, 0)))
    alog_v = jnp.zeros((1, LANES), F32).at[0, LANE_GA:LANE_GA + GDN_H].set(gdn_a_log)
    dtb_v = jnp.zeros((1, LANES), F32).at[0, LANE_GA:LANE_GA + GDN_H].set(gdn_dt_bias)
    return w_main, w_small, wa_pad, alog_v, dtb_v


def _swap_halves(w):
    half = w.shape[-1] // 2
    return jnp.concatenate([w[..., half:], w[..., :half]], axis=-1)


def _prep_odd(mla_w_in, mla_w_uq, mla_w_ukv):
    cq, ckv, kr = _split_cols(mla_w_in, (MLA_Q_RANK, MLA_KV_RANK, MLA_ROPE))
    pad = lambda w: jnp.pad(w, ((0, 0), (0, LANES - w.shape[1])))
    w_in_r = jnp.concatenate([cq, ckv, pad(kr), pad(_swap_halves(kr))], axis=1).astype(BF16)
    uq = mla_w_uq.reshape(MLA_Q_RANK, MLA_H, MLA_NOPE + MLA_ROPE)
    uq_rope = uq[..., MLA_NOPE:]
    w_uq_r = jnp.concatenate([uq, _swap_halves(uq_rope)], axis=-1).reshape(
        MLA_Q_RANK, MLA_H * LANES).astype(BF16)
    ukv = mla_w_ukv.reshape(MLA_KV_RANK, MLA_H, MLA_NOPE + MLA_V)
    wk_r = jnp.pad(ukv[..., :MLA_NOPE], ((0, 0), (0, 0), (0, LANES - MLA_NOPE))).reshape(
        MLA_KV_RANK, MLA_H * LANES).astype(BF16)
    wv_r = jnp.pad(ukv[..., MLA_NOPE:], ((0, 0), (0, 0), (0, LANES - MLA_V))).reshape(
        MLA_KV_RANK, MLA_H * LANES).astype(BF16)
    ones_row = jnp.tile((jnp.arange(LANES) == MLA_V).astype(F32), MLA_H)[None, :]
    eye = jnp.eye(MLA_ROPE, dtype=F32)
    place = jnp.concatenate([jnp.zeros((MLA_ROPE, MLA_NOPE), F32), eye, eye], axis=1)
    place = jnp.tile(place, (1, MLA_H)).astype(BF16)
    return w_in_r, w_uq_r, wk_r, wv_r, place, ones_row


def _rope_tables(pos0, L):
    half = MLA_ROPE // 2
    inv = jnp.exp(-math.log(ROPE_THETA) * jnp.arange(half, dtype=F32) / half)
    ang = (pos0 + jnp.arange(L, dtype=I32)).astype(F32)[:, None] * inv[None, :]
    cos, sin = jnp.cos(ang), jnp.sin(ang)
    tab_k = jnp.concatenate([cos, cos, -sin, sin], axis=1)
    scale = (MLA_NOPE + MLA_ROPE) ** -0.5 * math.log2(math.e)
    tab_q = jnp.concatenate([jnp.ones((L, MLA_NOPE), F32), tab_k], axis=1) * scale
    return tab_q, tab_k


def _mod4(mods, lo, hi):
    return [mods[i, lo:hi].reshape(hi - lo, 3, 1, D_MODEL) for i in range(mods.shape[0])]


def _trunk(x3, mods, pos0, gla_s, gdn_s, conv_s, past_ckv, past_kr, p):
    B, L, _ = x3.shape
    T = B * L
    tm = min(512, L)
    x = x3.reshape(T, D_MODEL)
    row = lambda a: a.reshape(1, -1)

    w_main, w_small, wa_pad, alog_v, dtb_v = p['even']
    zm, zs = _in0_call(x, mods[0], row(p['norm_w'][0, 0]), w_main, w_small, L, tm)
    o_gla, gla_t = _gla_call(zm, zs, wa_pad, row(p['gla_b_alpha'][0]), row(p['gla_norm_w'][0]),
                             jnp.swapaxes(gla_s, -1, -2), B, L)
    hist8 = jnp.pad(conv_s, ((0, 0), (8 - (CONV_W - 1), 0), (0, 0)))
    o_gdn, gdn_new = _gdn_call(zm, zs, p['gdn_conv_w'][0], alog_v, dtb_v, row(p['gdn_norm_w'][0]),
                               hist8, gdn_s, B, L)
    qkv0 = 3 * GLA_H * GLA_DV
    conv_new = zm.reshape(B, L, AB_MAIN)[:, L - (CONV_W - 1):, qkv0:qkv0 + GDN_CONV_DIM].astype(F32)
    w_out = p['ab_w_out_bf']
    half = GLA_H * GLA_DV
    xn, h2, route = _post_call([o_gla, o_gdn], [w_out[:half], w_out[half:]], x, mods[0], mods[1],
                               row(p['norm_w'][0, 1]), p['w_route'][0], p['b_route'][0], L, tm)
    x = _moe(xn, h2, route, mods[1], row(p['final_norm_w']), p['moe_w1'], p['moe_w3'], p['moe_w2'],
             0, L, final=False)

    w_in_r, w_uq_r, wk_r, wv_r, place, ones_row = p['odd']
    tab_q, tab_k = _rope_tables(pos0, L)
    q, ckv, kr = _in1_call(x, mods[2], row(p['norm_w'][1, 0]), w_in_r, row(p['mla_q_norm_w'][0]),
                           w_uq_r, row(p['mla_kv_norm_w'][0]), tab_q, tab_k, L, tm)
    if past_ckv is None:
        ckv_all, kr_all, Lk = ckv, kr, L
    else:
        Lk = past_ckv.shape[1] + L
        ckv_all = jnp.concatenate([past_ckv, ckv.reshape(B, L, -1)], axis=1).reshape(B * Lk, -1)
        kr_all = jnp.concatenate([past_kr, kr.reshape(B, L, -1)], axis=1).reshape(B * Lk, -1)
    tkv = 512 if (B * Lk) % 512 == 0 else Lk
    k_all, v_all = _kv_call(ckv_all, kr_all, wk_r, place, wv_r, ones_row, tkv)
    tq = min(ATTN_TILE, L)
    tk = ATTN_TILE if Lk % ATTN_TILE == 0 else Lk
    att = _attn_call(q, k_all, v_all, B, L, Lk, pos0, tq, tk)
    xn, h2, route = _post_call([att], [p['mla_w_out_bf']], x, mods[2], mods[3],
                               row(p['norm_w'][1, 1]), p['w_route'][1], p['b_route'][1], L, tm)
    y = _moe(xn, h2, route, mods[3], row(p['final_norm_w']), p['moe_w1'], p['moe_w3'], p['moe_w2'],
             1, L, final=True)
    return (y.reshape(B, L, D_MODEL), jnp.swapaxes(gla_t, -1, -2)[None], gdn_new[None], conv_new[None],
            ckv.reshape(1, B, L, MLA_KV_RANK), kr.reshape(1, B, L, MLA_ROPE))


def kernel(x_prompt, x_sample, state_gla, state_gdn, state_gdn_conv, cache_mla_ckv, cache_mla_krope,
           c_prompt, c_sample, ada_w, ada_b, norm_w, final_norm_w, ab_w_in, gla_w_alpha, gla_b_alpha,
           gla_norm_w, gdn_conv_w, gdn_a_log, gdn_dt_bias, gdn_norm_w, ab_w_out, mla_w_in, mla_q_norm_w,
           mla_w_uq, mla_kv_norm_w, mla_w_ukv, mla_w_out, moe_w_group, moe_b_group, moe_w_expert,
           moe_b_expert, moe_w1, moe_w3, moe_w2):
    depth = ada_w.shape[0]
    bp, bs = x_prompt.shape[0], x_sample.shape[0]
    mods = _ada_call(jnp.concatenate([c_prompt, c_sample], axis=0),
                     ada_w.reshape(2 * depth, D_MODEL, 3 * D_MODEL), ada_b.reshape(2 * depth, 1, 3 * D_MODEL))
    w_route = _hi_lo_pair(jnp.pad(jnp.concatenate([moe_w_group, moe_w_expert], axis=-1),
                                  ((0, 0), (0, 0), (0, LANES - N_GROUPS - N_EXPERTS))))
    b_route = jnp.pad(jnp.concatenate([moe_b_group, moe_b_expert], axis=-1),
                      ((0, 0), (0, LANES - N_GROUPS - N_EXPERTS)))[:, None, :]
    p = dict(norm_w=norm_w, final_norm_w=final_norm_w, gla_b_alpha=gla_b_alpha, gla_norm_w=gla_norm_w,
             gdn_conv_w=gdn_conv_w, gdn_norm_w=gdn_norm_w, mla_q_norm_w=mla_q_norm_w,
             mla_kv_norm_w=mla_kv_norm_w, moe_w1=moe_w1, moe_w3=moe_w3, moe_w2=moe_w2,
             w_route=w_route, b_route=b_route,
             even=_prep_even(ab_w_in[0], gla_w_alpha[0], gdn_a_log[0], gdn_dt_bias[0]),
             odd=_prep_odd(mla_w_in[0], mla_w_uq[0], mla_w_ukv[0]),
             ab_w_out_bf=ab_w_out[0].astype(BF16), mla_w_out_bf=mla_w_out[0].astype(BF16))
    y_p, gla_p, gdn_p, conv_p, ckv_p, kr_p = _trunk(
        x_prompt, _mod4(mods, 0, bp), 0,
        jnp.zeros((bp, GLA_H, GLA_DK, GLA_DV), F32), jnp.zeros((bp, GDN_H, GDN_DK, GDN_DV), F32),
        jnp.zeros((bp, CONV_W - 1, GDN_CONV_DIM), F32), None, None, p)
    y_s, gla_s, gdn_s, conv_s, ckv_s, kr_s = _trunk(
        x_sample, _mod4(mods, bp, bp + bs), cache_mla_ckv.shape[2],
        state_gla[0], state_gdn[0], state_gdn_conv[0], cache_mla_ckv[0], cache_mla_krope[0], p)
    return (y_p, y_s, gla_p, gdn_p, conv_p, ckv_p, kr_p, gla_s, gdn_s, conv_s, ckv_s, kr_s)
```

```python
import functools
import math

import jax
import jax.numpy as jnp
from jax import lax
from jax.experimental import pallas as pl
from jax.experimental.pallas import tpu as pltpu

F32 = jnp.float32
BF16 = jnp.bfloat16
I32 = jnp.int32
HI = lax.Precision.HIGHEST

D_MODEL = 1024
CHUNK = 64
GLA_H, GLA_DK, GLA_DV, GLA_LR, GLA_TAU = 4, 64, 128, 16, 16.0
GDN_H, GDN_DK, GDN_DV, CONV_W = 4, 128, 128, 4
GDN_CONV_DIM = GDN_H * (2 * GDN_DK + GDN_DV)
AB_IN_WIDTHS = (GLA_H * GLA_DK, GLA_H * GLA_DK, GLA_H * GLA_DV, GLA_LR, GLA_H * GLA_DV,
                GDN_CONV_DIM, GDN_H, GDN_H, GDN_H * GDN_DV)
AB_MAIN = 3584
MLA_H, MLA_NOPE, MLA_ROPE, MLA_V = 16, 64, 32, 64
MLA_Q_RANK, MLA_KV_RANK = 384, 256
MLA_IN_R = MLA_Q_RANK + MLA_KV_RANK + 256
ROPE_THETA = 10000.0
N_GROUPS, EXPERTS_PER_GROUP, N_EXPERTS, D_EXPERT = 4, 8, 32, 512
GROUP_SHIFT = int(math.log2(EXPERTS_PER_GROUP))
CHUNK_SHIFT = int(math.log2(CHUNK))
EPS = 1e-6

LANES = 128
GLA_SUB = 16
SCAN_CHUNKS = 4
GLA_SAFE_LOG_DECAY = -60.0
MOE_ROWS = 256
MOE_TILE = 512
MOE_ALIGN = 8
MOE_BIG = 4
MOE_XCOLS = D_MODEL + LANES
ATTN_TILE = 512
NEG = -3.0e38
VMEM_LIMIT = 56 * 1024 * 1024

LANE_GA = GLA_LR
LANE_GB = GLA_LR + GDN_H

NT = (((1,), (1,)), ((), ()))
TN = (((0,), (0,)), ((), ()))


def _cp(sem, vmem=None):
    return pltpu.CompilerParams(dimension_semantics=sem, vmem_limit_bytes=vmem or VMEM_LIMIT)


def _sds(shape, dtype):
    return jax.ShapeDtypeStruct(shape, dtype)


def _sigmoid(x):
    return 1.0 / (1.0 + jnp.exp(-x))


def _softplus(x):
    return jnp.maximum(x, 0.0) + jnp.log(1.0 + jnp.exp(-jnp.abs(x)))


def _rms(x):
    return x * lax.rsqrt(jnp.mean(x * x, axis=-1, keepdims=True) + EPS)


def _norm_mod(x, nw, shift, scale):
    return (_rms(x) * nw) * (1.0 + scale) + shift


def _dot_bf(a, b):
    return jnp.dot(a.astype(BF16), b.astype(BF16), preferred_element_type=F32)


def _dot_hi(a, b):
    return jnp.dot(a, b, precision=HI, preferred_element_type=F32)


def _ada_kernel(c_ref, w_ref, b_ref, o_ref):
    c = c_ref[...]
    o_ref[0] = _dot_hi(c * _sigmoid(c), w_ref[0]) + b_ref[0]


def _ada_call(c_all, ada_w, ada_b):
    n = ada_w.shape[0]
    nb = c_all.shape[0]
    return pl.pallas_call(
        _ada_kernel, grid=(n, 3),
        in_specs=[pl.BlockSpec((nb, D_MODEL), lambda i, j: (0, 0)),
                  pl.BlockSpec((1, D_MODEL, D_MODEL), lambda i, j: (i, 0, j)),
                  pl.BlockSpec((1, 1, D_MODEL), lambda i, j: (i, 0, j))],
        out_specs=pl.BlockSpec((1, nb, D_MODEL), lambda i, j: (i, 0, j)),
        out_shape=_sds((n, nb, 3 * D_MODEL), F32),
        compiler_params=_cp(("arbitrary", "arbitrary")), name="ada",
    )(c_all, ada_w, ada_b)


def _in0_kernel(x_ref, mod_ref, nw_ref, wm_ref, ws_ref, zm_ref, zs_ref):
    h = _norm_mod(x_ref[...], nw_ref[...], mod_ref[0, 0], mod_ref[0, 1])
    h_hl = _split_bf(h)
    zm_ref[...] = jnp.dot(h_hl[0], wm_ref[...], preferred_element_type=F32).astype(BF16)
    zs_ref[...] = _dot3(h_hl, (ws_ref[0], ws_ref[1]))


def _in0_call(x, mod, nw, w_main, w_small, L, tm):
    T = x.shape[0]
    per_b = L // tm
    return pl.pallas_call(
        _in0_kernel, grid=(T // tm,),
        in_specs=[pl.BlockSpec((tm, D_MODEL), lambda i: (i, 0)),
                  pl.BlockSpec((1, 3, 1, D_MODEL), lambda i: (i // per_b, 0, 0, 0)),
                  pl.BlockSpec((1, D_MODEL), lambda i: (0, 0)),
                  pl.BlockSpec((D_MODEL, AB_MAIN), lambda i: (0, 0)),
                  pl.BlockSpec((2, D_MODEL, LANES), lambda i: (0, 0, 0))],
        out_specs=[pl.BlockSpec((tm, AB_MAIN), lambda i: (i, 0)),
                   pl.BlockSpec((tm, LANES), lambda i: (i, 0))],
        out_shape=[_sds((T, AB_MAIN), BF16), _sds((T, LANES), F32)],
        compiler_params=_cp(("arbitrary",)), name="in0",
    )(x, mod, nw, w_main, w_small)


def _gated_head_out(o, nw, gate):
    gate = gate.astype(F32)
    return (_rms(o) * nw * (gate * _sigmoid(gate))).astype(BF16)


def _gla_chunk_exact(q, k, v, b, st):
    CL = q.shape[0]
    row = lax.broadcasted_iota(I32, (GLA_SUB, 1), 0)
    bprev = jnp.zeros((1, GLA_DK), F32)
    outs = []
    for blk in range(CL // GLA_SUB):
        sl = slice(blk * GLA_SUB, (blk + 1) * GLA_SUB)
        qi, ki, vi = q[sl], k[sl], v[sl]
        brel = b[sl] - bprev
        bend = brel[GLA_SUB - 1:GLA_SUB]
        oi = lax.dot_general((qi * jnp.exp(brel)).astype(BF16), st.astype(BF16), NT,
                             preferred_element_type=F32)
        for j in range(GLA_SUB):
            d = brel - brel[j:j + 1]
            e = jnp.where(row >= j, jnp.exp(jnp.minimum(d, 0.0)), 0.0)
            a = jnp.sum(qi * ki[j:j + 1] * e, axis=-1, keepdims=True)
            oi = oi + a * vi[j:j + 1]
        outs.append(oi)
        khat = ki * jnp.exp(bend - brel)
        st = st * jnp.exp(bend) + lax.dot_general(vi.astype(BF16), khat.astype(BF16), TN,
                                                  preferred_element_type=F32)
        bprev = b[(blk + 1) * GLA_SUB - 1:(blk + 1) * GLA_SUB]
    return jnp.concatenate(outs, axis=0), st


def _gla_chunk_fast(q, k, v, b, st, causal):
    CL = q.shape[0]
    qh = (q * jnp.exp(b)).astype(BF16)
    kh = (k * jnp.exp(-b)).astype(BF16)
    vb = v.astype(BF16)
    att = jnp.where(causal, lax.dot_general(qh, kh, NT, preferred_element_type=F32), 0.0)
    o = (lax.dot_general(qh, st.astype(BF16), NT, preferred_element_type=F32)
         + jnp.dot(att.astype(BF16), vb, preferred_element_type=F32))
    blast = b[CL - 1:CL]
    kdec = (k * jnp.exp(blast - b)).astype(BF16)
    st = st * jnp.exp(blast) + lax.dot_general(vb, kdec, TN, preferred_element_type=F32)
    return o, st


def _gla_kernel(q_ref, k_ref, v_ref, gr_ref, zs_ref, wa_ref, ba_ref, nw_ref, s0_ref,
                o_ref, st_ref, st_sc, b_sc, *, CL, nch, nsteps):
    step = pl.program_id(1)

    @pl.when(step == 0)
    def _():
        st_sc[...] = s0_ref[0]

    r = lax.broadcasted_iota(I32, (CL, CL), 0)
    cc = lax.broadcasted_iota(I32, (CL, CL), 1)
    causal = r >= cc
    tri = causal.astype(F32)
    b_min = None
    for ch in range(nch):
        rows = slice(ch * CL, (ch + 1) * CL)
        pre = _dot_hi(zs_ref[rows, :], wa_ref[...]) + ba_ref[...]
        la = -_softplus(-pre) * (1.0 / GLA_TAU)
        b_ch = _dot_hi(tri, la)
        b_sc[rows, :] = b_ch
        lo = jnp.min(b_ch[CL - 1:CL, :])
        b_min = lo if b_min is None else jnp.minimum(b_min, lo)
    safe = b_min > GLA_SAFE_LOG_DECAY

    def run(chunk_fn):
        nw = nw_ref[...]
        for h in range(GLA_H):
            kc = slice(h * GLA_DK, (h + 1) * GLA_DK)
            vc = slice(h * GLA_DV, (h + 1) * GLA_DV)
            st = st_sc[h]
            for ch in range(nch):
                rows = slice(ch * CL, (ch + 1) * CL)
                q = q_ref[rows, kc].astype(F32) * GLA_DK ** -0.5
                o, st = chunk_fn(q, k_ref[rows, kc].astype(F32), v_ref[rows, vc].astype(F32),
                                 b_sc[rows, kc], st)
                o_ref[rows, vc] = _gated_head_out(o, nw, gr_ref[rows, vc])
            st_sc[h] = st

    @pl.when(safe)
    def _():
        run(functools.partial(_gla_chunk_fast, causal=causal))

    @pl.when(jnp.logical_not(safe))
    def _():
        run(_gla_chunk_exact)

    @pl.when(step == nsteps - 1)
    def _():
        st_ref[0] = st_sc[...]


def _gla_call(zm, zs, wa_pad, b_alpha, nw, s0t, B, L):
    CL = min(CHUNK, L)
    nch = SCAN_CHUNKS if L % (CL * SCAN_CHUNKS) == 0 else 1
    rows = CL * nch
    nsteps = L // rows
    T = B * L
    qk_w = GLA_H * GLA_DK
    v_w = GLA_H * GLA_DV
    row = lambda b, c: b * nsteps + c
    return pl.pallas_call(
        functools.partial(_gla_kernel, CL=CL, nch=nch, nsteps=nsteps), grid=(B, nsteps),
        in_specs=[pl.BlockSpec((rows, qk_w), lambda b, c: (row(b, c), 0)),
                  pl.BlockSpec((rows, qk_w), lambda b, c: (row(b, c), 1)),
                  pl.BlockSpec((rows, v_w), lambda b, c: (row(b, c), 1)),
                  pl.BlockSpec((rows, v_w), lambda b, c: (row(b, c), 2)),
                  pl.BlockSpec((rows, LANES), lambda b, c: (row(b, c), 0)),
                  pl.BlockSpec((LANES, qk_w), lambda b, c: (0, 0)),
                  pl.BlockSpec((1, qk_w), lambda b, c: (0, 0)),
                  pl.BlockSpec((1, GLA_DV), lambda b, c: (0, 0)),
                  pl.BlockSpec((1, GLA_H, GLA_DV, GLA_DK), lambda b, c: (b, 0, 0, 0))],
        out_specs=[pl.BlockSpec((rows, v_w), lambda b, c: (row(b, c), 0)),
                   pl.BlockSpec((1, GLA_H, GLA_DV, GLA_DK), lambda b, c: (b, 0, 0, 0))],
        out_shape=[_sds((T, v_w), BF16), _sds((B, GLA_H, GLA_DV, GLA_DK), F32)],
        scratch_shapes=[pltpu.VMEM((GLA_H, GLA_DV, GLA_DK), F32), pltpu.VMEM((rows, qk_w), F32)],
        compiler_params=_cp(("arbitrary", "arbitrary")), name="gla",
    )(zm, zm, zm, zm, zs, wa_pad, b_alpha, nw, s0t)


def _split_bf(a):
    hi = a.astype(BF16)
    return hi, (a - hi.astype(F32)).astype(BF16)


def _dot3(a_hl, b_hl):
    (ah, al), (bh, bl) = a_hl, b_hl
    d = functools.partial(jnp.dot, preferred_element_type=F32)
    return d(ah, bh) + d(ah, bl) + d(al, bh)


def _gdn_kernel(qkv_ref, zs_ref, gg_ref, cw_ref, alog_ref, dtb_ref, nw_ref, hist_ref, s0_ref,
                o_ref, s_ref, cv_sc, st_sc, *, CL, nch, nsteps):
    step = pl.program_id(1)
    R = CL * nch

    @pl.when(step == 0)
    def _():
        cv_sc[0:8, :] = hist_ref[0]
        st_sc[...] = s0_ref[0]

    x = qkv_ref[...].astype(F32)
    cv_sc[8:8 + R, :] = x
    cw = cw_ref[...]
    conv = (cv_sc[5:5 + R, :] * cw[0:1] + cv_sc[6:6 + R, :] * cw[1:2]
            + cv_sc[7:7 + R, :] * cw[2:3] + x * cw[3:4])
    cv_sc[0:8, :] = x[R - 8:R]
    conv = conv * _sigmoid(conv)

    zs = zs_ref[...]
    g_all = -jnp.exp(alog_ref[...]) * _softplus(zs + dtb_ref[...])
    beta_all = _sigmoid(zs)
    r = lax.broadcasted_iota(I32, (CL, CL), 0)
    cc = lax.broadcasted_iota(I32, (CL, CL), 1)
    tri = (r >= cc).astype(F32)
    eye = (r == cc).astype(F32)
    nw = nw_ref[...]
    qk_w = GDN_H * GDN_DK

    units = [(ch, h) for ch in range(nch) for h in range(GDN_H)]
    stage = {}
    for ch in range(nch):
        rows = slice(ch * CL, (ch + 1) * CL)
        gam = _dot_hi(tri, g_all[rows])
        gam_t = gam.T
        for h in range(GDN_H):
            q = conv[rows, h * GDN_DK:(h + 1) * GDN_DK]
            k = conv[rows, qk_w + h * GDN_DK:qk_w + (h + 1) * GDN_DK]
            v = conv[rows, 2 * qk_w + h * GDN_DV:2 * qk_w + (h + 1) * GDN_DV]
            q = q * lax.rsqrt(jnp.sum(q * q, axis=-1, keepdims=True) + EPS) * GDN_DK ** -0.5
            k = k * lax.rsqrt(jnp.sum(k * k, axis=-1, keepdims=True) + EPS)
            gcol = gam[:, LANE_GA + h:LANE_GA + h + 1]
            grow = gam_t[LANE_GA + h:LANE_GA + h + 1, :]
            bcol = beta_all[rows, LANE_GB + h:LANE_GB + h + 1]
            dec = jnp.where(r >= cc, jnp.exp(jnp.minimum(gcol - grow, 0.0)), 0.0)
            eg = jnp.exp(gcol)
            glast = gcol[CL - 1:CL]
            stage[ch, h] = dict(
                kb=k.astype(BF16), qb=q.astype(BF16), dec=dec, bcol=bcol,
                ub=(v * bcol).astype(BF16), wb=(k * (bcol * eg)).astype(BF16),
                qe=(q * eg).astype(BF16), kdec=(k * jnp.exp(glast - gcol)).astype(BF16),
                elast=jnp.exp(glast))
    for un in units:
        d = stage[un]
        kk = lax.dot_general(d['kb'], d['kb'], NT, preferred_element_type=F32)
        d['pw'] = jnp.where(r > cc, d['bcol'] * kk * d['dec'], 0.0)
        d['t'] = eye - d['pw']
    for _ in range(int(math.log2(CL)) - 1):
        for un in units:
            d = stage[un]
            hl = _split_bf(d['pw'])
            d['pw'] = _dot3(hl, hl)
        for un in units:
            d = stage[un]
            d['t'] = d['t'] + _dot3(_split_bf(d['t']), _split_bf(d['pw']))
    pre = {}
    for un in units:
        d = stage[un]
        tb = d['t'].astype(BF16)
        u = jnp.dot(tb, d['ub'], preferred_element_type=F32)
        w = jnp.dot(tb, d['wb'], preferred_element_type=F32).astype(BF16)
        qk = (lax.dot_general(d['qb'], d['kb'], NT, preferred_element_type=F32) * d['dec']).astype(BF16)
        pre[un] = (u, w, qk, d['qe'], d['kdec'], d['elast'])

    for h in range(GDN_H):
        vc = slice(h * GDN_DV, (h + 1) * GDN_DV)
        st = st_sc[h]
        for ch in range(nch):
            rows = slice(ch * CL, (ch + 1) * CL)
            u, w, qk, qe, kdec, elast = pre[ch, h]
            stb = st.astype(BF16)
            delta = u - jnp.dot(w, stb, preferred_element_type=F32)
            db = delta.astype(BF16)
            o = (jnp.dot(qe, stb, preferred_element_type=F32)
                 + jnp.dot(qk, db, preferred_element_type=F32))
            st = st * elast + lax.dot_general(kdec, db, TN, preferred_element_type=F32)
            o_ref[rows, vc] = _gated_head_out(o, nw, gg_ref[rows, vc])
        st_sc[h] = st

    @pl.when(step == nsteps - 1)
    def _():
        s_ref[0] = st_sc[...]


def _gdn_call(zm, zs, conv_w, alog_v, dtb_v, nw, hist8, s0, B, L):
    CL = min(CHUNK, L)
    nch = SCAN_CHUNKS if L % (CL * SCAN_CHUNKS) == 0 else 1
    rows = CL * nch
    nsteps = L // rows
    T = B * L
    v_w = GDN_H * GDN_DV
    row = lambda b, c: b * nsteps + c
    return pl.pallas_call(
        functools.partial(_gdn_kernel, CL=CL, nch=nch, nsteps=nsteps), grid=(B, nsteps),
        in_specs=[pl.BlockSpec((rows, GDN_CONV_DIM), lambda b, c: (row(b, c), 1)),
                  pl.BlockSpec((rows, LANES), lambda b, c: (row(b, c), 0)),
                  pl.BlockSpec((rows, v_w), lambda b, c: (row(b, c), 6)),
                  pl.BlockSpec((CONV_W, GDN_CONV_DIM), lambda b, c: (0, 0)),
                  pl.BlockSpec((1, LANES), lambda b, c: (0, 0)),
                  pl.BlockSpec((1, LANES), lambda b, c: (0, 0)),
                  pl.BlockSpec((1, GDN_DV), lambda b, c: (0, 0)),
                  pl.BlockSpec((1, 8, GDN_CONV_DIM), lambda b, c: (b, 0, 0)),
                  pl.BlockSpec((1, GDN_H, GDN_DK, GDN_DV), lambda b, c: (b, 0, 0, 0))],
        out_specs=[pl.BlockSpec((rows, v_w), lambda b, c: (row(b, c), 0)),
                   pl.BlockSpec((1, GDN_H, GDN_DK, GDN_DV), lambda b, c: (b, 0, 0, 0))],
        out_shape=[_sds((T, v_w), BF16), _sds((B, GDN_H, GDN_DK, GDN_DV), F32)],
        scratch_shapes=[pltpu.VMEM((8 + rows, GDN_CONV_DIM), F32),
                        pltpu.VMEM((GDN_H, GDN_DK, GDN_DV), F32)],
        compiler_params=_cp(("arbitrary", "arbitrary")), name="gdn",
    )(zm, zs, zm, conv_w, alog_v, dtb_v, nw, hist8, s0)


def _post_kernel(*refs, n_in):
    a_refs = refs[:n_in]
    w_refs = refs[n_in:2 * n_in]
    x_ref, mod1_ref, mod2_ref, nw2_ref, wr_ref, br_ref, xn_ref, h2_ref, rt_ref = refs[2 * n_in:]
    acc = jnp.dot(a_refs[0][...], w_refs[0][...], preferred_element_type=F32)
    for a_ref, w_ref in zip(a_refs[1:], w_refs[1:]):
        acc = acc + jnp.dot(a_ref[...], w_ref[...], preferred_element_type=F32)
    xn = x_ref[...] + mod1_ref[0, 2] * acc
    xn_ref[...] = xn
    h2 = _norm_mod(xn, nw2_ref[...], mod2_ref[0, 0], mod2_ref[0, 1])
    h2_ref[...] = h2
    logits = _dot3(_split_bf(h2), (wr_ref[0], wr_ref[1])) + br_ref[...]
    lane = lax.broadcasted_iota(I32, logits.shape, 1)
    is_g = lane < N_GROUPS
    gl = jnp.where(is_g, logits, NEG)
    gmax = jnp.max(gl, axis=-1, keepdims=True)
    g_sel = jnp.min(jnp.where(gl == gmax, lane, LANES), axis=-1, keepdims=True)
    p_g = 1.0 / jnp.sum(jnp.where(is_g, jnp.exp(logits - gmax), 0.0), axis=-1, keepdims=True)
    in_grp = jnp.logical_and(
        jnp.logical_and(lane >= N_GROUPS, lane < N_GROUPS + N_EXPERTS),
        jnp.right_shift(lane - N_GROUPS, GROUP_SHIFT) == g_sel)
    el = jnp.where(in_grp, logits, NEG)
    v1 = jnp.max(el, axis=-1, keepdims=True)
    i1 = jnp.min(jnp.where(jnp.logical_and(in_grp, el == v1), lane, LANES), axis=-1, keepdims=True)
    rest = jnp.logical_and(in_grp, lane != i1)
    el2 = jnp.where(rest, logits, NEG)
    v2 = jnp.max(el2, axis=-1, keepdims=True)
    i2 = jnp.min(jnp.where(jnp.logical_and(rest, el2 == v2), lane, LANES), axis=-1, keepdims=True)
    ex = jnp.exp(v2 - v1)
    w1 = 1.0 / (1.0 + ex)
    w2 = ex * w1
    e1 = (i1 - N_GROUPS).astype(F32)
    e2 = (i2 - N_GROUPS).astype(F32)
    rt_ref[...] = jnp.where(lane == 0, e1, jnp.where(lane == 1, e2, jnp.where(
        lane == 2, p_g * w1, jnp.where(lane == 3, p_g * w2, 0.0))))


def _post_call(a_list, w_list, x, mod1, mod2, nw2, w_route, b_route, L, tm):
    T = x.shape[0]
    per_b = L // tm
    n_in = len(a_list)
    tok = lambda i: (i, 0)
    const = lambda i: (0, 0)
    modmap = lambda i: (i // per_b, 0, 0, 0)
    in_specs = ([pl.BlockSpec((tm, a.shape[1]), tok) for a in a_list]
                + [pl.BlockSpec(w.shape, const) for w in w_list]
                + [pl.BlockSpec((tm, D_MODEL), tok),
                   pl.BlockSpec((1, 3, 1, D_MODEL), modmap),
                   pl.BlockSpec((1, 3, 1, D_MODEL), modmap),
                   pl.BlockSpec((1, D_MODEL), const),
                   pl.BlockSpec((2, D_MODEL, LANES), lambda i: (0, 0, 0)),
                   pl.BlockSpec((1, LANES), const)])
    return pl.pallas_call(
        functools.partial(_post_kernel, n_in=n_in), grid=(T // tm,),
        in_specs=in_specs,
        out_specs=[pl.BlockSpec((tm, D_MODEL), tok), pl.BlockSpec((tm, D_MODEL), tok),
                   pl.BlockSpec((tm, LANES), tok)],
        out_shape=[_sds((T, D_MODEL), F32), _sds((T, D_MODEL), F32), _sds((T, LANES), F32)],
        compiler_params=_cp(("arbitrary",)), name="post",
    )(*a_list, *w_list, x, mod1, mod2, nw2, w_route, b_route)


def _rank_kernel(rt_ref, pos_ref, post_ref, cnt_ref):
    rt = rt_ref[...]
    tm = rt.shape[0]
    lane = lax.broadcasted_iota(I32, rt.shape, 1)
    o1 = lane == rt[:, 0:1].astype(I32)
    o2 = lane == rt[:, 1:2].astype(I32)
    onehot = jnp.where(o1, 1.0, 0.0) + jnp.where(o2, 1.0, 0.0)
    r = lax.broadcasted_iota(I32, (tm, tm), 0)
    cc = lax.broadcasted_iota(I32, (tm, tm), 1)
    before = jnp.dot(jnp.where(r > cc, 1.0, 0.0).astype(BF16), onehot.astype(BF16),
                     preferred_element_type=F32)
    cnt = jnp.sum(onehot, axis=0, keepdims=True)
    er = lax.broadcasted_iota(I32, (LANES, LANES), 0)
    ec = lax.broadcasted_iota(I32, (LANES, LANES), 1)
    cnt8 = jnp.floor((cnt + (MOE_ALIGN - 1)) * (1.0 / MOE_ALIGN)) * MOE_ALIGN
    start = _dot_hi(jnp.broadcast_to(cnt8, (8, LANES)), jnp.where(er < ec, 1.0, 0.0))[0:1]
    where = before + start
    p1 = jnp.sum(jnp.where(o1, where, 0.0), axis=-1, keepdims=True)
    p2 = jnp.sum(jnp.where(o2, where, 0.0), axis=-1, keepdims=True)
    pos = jnp.where(lane == 0, p1, jnp.where(lane == 1, p2, 0.0))
    pos_ref[...] = pos
    post_ref[0] = pos.T[0:8, :].astype(I32)
    cnt_ref[0] = cnt


def _rank_call(route, tm):
    T = route.shape[0]
    nt = T // tm
    return pl.pallas_call(
        _rank_kernel, grid=(nt,),
        in_specs=[pl.BlockSpec((tm, LANES), lambda i: (i, 0))],
        out_specs=[pl.BlockSpec((tm, LANES), lambda i: (i, 0)),
                   pl.BlockSpec((1, 8, tm), lambda i: (i, 0, 0)),
                   pl.BlockSpec((1, 1, LANES), lambda i: (i, 0, 0))],
        out_shape=[_sds((T, LANES), F32), _sds((nt, 8, tm), I32), _sds((nt, 1, LANES), F32)],
        compiler_params=_cp(("arbitrary",)), name="moe_rank",
    )(route)


def _start_dmas(tile, lists, copy):
    n_big, n_small, big_src, big_dst, small_src, small_dst = lists
    max_big = big_src.shape[0] // n_big.shape[0]
    max_small = small_src.shape[0] // n_small.shape[0]

    def big(k, carry):
        idx = tile * max_big + k
        copy(pl.multiple_of(big_src[idx], MOE_ALIGN), pl.multiple_of(big_dst[idx], MOE_ALIGN),
             MOE_BIG * MOE_ALIGN).start()
        return carry

    def small(k, carry):
        idx = tile * max_small + k
        copy(pl.multiple_of(small_src[idx], MOE_ALIGN), pl.multiple_of(small_dst[idx], MOE_ALIGN),
             MOE_ALIGN).start()
        return carry

    lax.fori_loop(0, n_big[tile], big, 0)
    lax.fori_loop(0, n_small[tile], small, 0)


def _drain_dmas(n_big, n_small, copy):
    def big(c, carry):
        copy(0, 0, MOE_BIG * MOE_ALIGN).wait()
        return carry

    def small(c, carry):
        copy(0, 0, MOE_ALIGN).wait()
        return carry

    lax.fori_loop(0, n_big, big, 0)
    lax.fori_loop(0, n_small, small, 0)


def _disp_kernel(*refs, tm, nt):
    lists, (last_ref, h_ref, rt_ref, post_ref, xg_hbm, xs_sc, sems) = refs[:6], refs[6:]
    i = pl.program_id(0)
    slot = i % 2
    cap = xs_sc.shape[1]

    def runs(tile, sl, start):
        copy = lambda s, d, n: pltpu.make_async_copy(
            xs_sc.at[sl, pl.ds(s, n), :], xg_hbm.at[pl.ds(d, n), :], sems.at[sl])
        if start:
            _start_dmas(tile, lists, copy)
        else:
            _drain_dmas(lists[0][tile], lists[1][tile], copy)

    @pl.when(i == 0)
    def _():
        xs_sc[0, 0:MOE_ROWS, :] = jnp.zeros((MOE_ROWS, MOE_XCOLS), F32)
        zero_block = lambda blk: pltpu.make_async_copy(
            xs_sc.at[0, pl.ds(0, MOE_ROWS), :], xg_hbm.at[pl.ds(blk * MOE_ROWS, MOE_ROWS), :], sems.at[0])
        n_blocks = xg_hbm.shape[0] // MOE_ROWS

        def fill(e, act):
            @pl.when(last_ref[e] >= 0)
            def _():
                act(zero_block(last_ref[e]))

        def start_all(e, carry):
            fill(e, lambda cp: cp.start())
            return carry

        def wait_all(e, carry):
            fill(e, lambda cp: cp.wait())
            return carry

        def start_tail(blk, carry):
            zero_block(blk).start()
            return carry

        def wait_tail(blk, carry):
            zero_block(blk).wait()
            return carry

        lax.fori_loop(0, N_EXPERTS, start_all, 0)
        lax.fori_loop(last_ref[N_EXPERTS], n_blocks, start_tail, 0)
        lax.fori_loop(0, N_EXPERTS, wait_all, 0)
        lax.fori_loop(last_ref[N_EXPERTS], n_blocks, wait_tail, 0)

    @pl.when(i >= 2)
    def _():
        runs(i - 2, slot, False)

    post = post_ref[0]
    rows = lax.broadcasted_iota(I32, (cap, tm), 0)
    p1 = jnp.where(rows == post[0:1, :], 1.0, 0.0).astype(BF16)
    p2 = jnp.where(rows == post[1:2, :], 1.0, 0.0).astype(BF16)
    xs_sc[slot, :, 0:D_MODEL] = jnp.dot(p1 + p2, h_ref[...].astype(BF16), preferred_element_type=F32)
    rt = rt_ref[...]
    lane = lax.broadcasted_iota(I32, rt.shape, 1)

    def hi_lo(col):
        hi = col.astype(BF16).astype(F32)
        return jnp.where(lane == 0, hi, jnp.where(lane == 1, col - hi, 0.0)).astype(BF16)

    xs_sc[slot, :, D_MODEL:] = (jnp.dot(p1, hi_lo(rt[:, 2:3]), preferred_element_type=F32)
                                + jnp.dot(p2, hi_lo(rt[:, 3:4]), preferred_element_type=F32))
    runs(i, slot, True)

    @pl.when(i == nt - 1)
    def _():
        if nt > 1:
            runs(i - 1, 1 - slot, False)
        runs(i, slot, False)


def _disp_call(lists, last_blk, h2, route, post, n_rows, tm):
    T = h2.shape[0]
    nt = T // tm
    return pl.pallas_call(
        functools.partial(_disp_kernel, tm=tm, nt=nt),
        grid_spec=pltpu.PrefetchScalarGridSpec(
            num_scalar_prefetch=7, grid=(nt,),
            in_specs=[pl.BlockSpec((tm, D_MODEL), lambda i, *_: (i, 0)),
                      pl.BlockSpec((tm, LANES), lambda i, *_: (i, 0)),
                      pl.BlockSpec((1, 8, tm), lambda i, *_: (i, 0, 0))],
            out_specs=pl.BlockSpec(memory_space=pl.ANY),
            scratch_shapes=[pltpu.VMEM((2, _moe_cap(tm), MOE_XCOLS), F32), pltpu.SemaphoreType.DMA((2,))]),
        out_shape=_sds((n_rows, MOE_XCOLS), F32),
        compiler_params=_cp(("arbitrary",)), name="moe_dispatch",
    )(*lists, last_blk, h2, route, post)


def _expert_kernel(be_ref, nu_ref, x_ref, w1_ref, w3_ref, w2_ref, y_ref, w1b, w3b, w2b):
    i = pl.program_id(0)
    used = i < nu_ref[0]
    fresh = jnp.logical_or(i == 0, be_ref[i] != be_ref[jnp.maximum(i - 1, 0)])

    @pl.when(jnp.logical_and(used, fresh))
    def _():
        w1b[...] = w1_ref[0, 0].astype(BF16)
        w3b[...] = w3_ref[0, 0].astype(BF16)
        w2b[...] = w2_ref[0, 0].astype(BF16)

    @pl.when(used)
    def _():
        x = x_ref[:, 0:D_MODEL].astype(BF16)
        a = jnp.dot(x, w1b[...], preferred_element_type=F32)
        g = jnp.dot(x, w3b[...], preferred_element_type=F32)
        hm = (a * _sigmoid(a) * g).astype(BF16)
        row_w = x_ref[:, D_MODEL:D_MODEL + 1] + x_ref[:, D_MODEL + 1:D_MODEL + 2]
        y_ref[...] = jnp.dot(hm, w2b[...], preferred_element_type=F32) * row_w

    @pl.when(jnp.logical_not(used))
    def _():
        y_ref[...] = jnp.zeros(y_ref.shape, F32)


def _expert_call(blk_e, n_used, xg, w1, w3, w2, layer):
    P = xg.shape[0]
    nb = P // MOE_ROWS
    rowmap = lambda i, be, nu: (jnp.minimum(i, nu[0] - 1), 0)
    wmap = lambda i, be, nu: (layer, be[i], 0, 0)
    return pl.pallas_call(
        _expert_kernel,
        grid_spec=pltpu.PrefetchScalarGridSpec(
            num_scalar_prefetch=2, grid=(nb,),
            in_specs=[pl.BlockSpec((MOE_ROWS, MOE_XCOLS), rowmap),
                      pl.BlockSpec((1, 1, D_MODEL, D_EXPERT), wmap),
                      pl.BlockSpec((1, 1, D_MODEL, D_EXPERT), wmap),
                      pl.BlockSpec((1, 1, D_EXPERT, D_MODEL), wmap)],
            out_specs=pl.BlockSpec((MOE_ROWS, D_MODEL), lambda i, be, nu: (i, 0)),
            scratch_shapes=[pltpu.VMEM((D_MODEL, D_EXPERT), BF16),
                            pltpu.VMEM((D_MODEL, D_EXPERT), BF16),
                            pltpu.VMEM((D_EXPERT, D_MODEL), BF16)]),
        out_shape=_sds((P, D_MODEL), F32),
        compiler_params=_cp(("arbitrary",)), name="moe_experts",
    )(blk_e, n_used, xg, w1, w3, w2)


def _comb_kernel(*refs, tm, final, nt):
    lists, (x_ref, pos_ref, mod_ref, fnw_ref, y_hbm, o_ref, ys_sc, sems) = refs[:6], refs[6:]
    i = pl.program_id(0)
    slot = i % 2
    cap = ys_sc.shape[1]

    def runs(tile, sl, start):
        copy = lambda s, d, n: pltpu.make_async_copy(
            y_hbm.at[pl.ds(d, n), :], ys_sc.at[sl, pl.ds(s, n), :], sems.at[sl])
        if start:
            _start_dmas(tile, lists, copy)
        else:
            _drain_dmas(lists[0][tile], lists[1][tile], copy)

    def begin(tile, sl):
        ys_sc[sl, 2 * tm:cap, :] = jnp.zeros((cap - 2 * tm, D_MODEL), F32)
        runs(tile, sl, True)

    @pl.when(i == 0)
    def _():
        begin(0, 0)

    @pl.when(i + 1 < nt)
    def _():
        begin(i + 1, 1 - slot)

    runs(i, slot, False)
    pos = pos_ref[...]
    cols = lax.broadcasted_iota(I32, (tm, cap), 1)
    pick = (jnp.where(cols == pos[:, 0:1].astype(I32), 1.0, 0.0)
            + jnp.where(cols == pos[:, 1:2].astype(I32), 1.0, 0.0)).astype(BF16)
    y = jnp.dot(pick, ys_sc[slot].astype(BF16), preferred_element_type=F32)
    out = x_ref[...] + mod_ref[0, 2] * y
    if final:
        out = _rms(out) * fnw_ref[...]
    o_ref[...] = out


def _comb_call(lists, xn, pos, mod, fnw, yb, L, tm, final):
    T = xn.shape[0]
    nt = T // tm
    per_b = L // tm
    return pl.pallas_call(
        functools.partial(_comb_kernel, tm=tm, final=final, nt=nt),
        grid_spec=pltpu.PrefetchScalarGridSpec(
            num_scalar_prefetch=6, grid=(nt,),
            in_specs=[pl.BlockSpec((tm, D_MODEL), lambda i, *_: (i, 0)),
                      pl.BlockSpec((tm, LANES), lambda i, *_: (i, 0)),
                      pl.BlockSpec((1, 3, 1, D_MODEL), lambda i, *_: (i // per_b, 0, 0, 0)),
                      pl.BlockSpec((1, D_MODEL), lambda i, *_: (0, 0)),
                      pl.BlockSpec(memory_space=pl.ANY)],
            out_specs=pl.BlockSpec((tm, D_MODEL), lambda i, *_: (i, 0)),
            scratch_shapes=[pltpu.VMEM((2, _moe_cap(tm), D_MODEL), F32), pltpu.SemaphoreType.DMA((2,))]),
        out_shape=_sds((T, D_MODEL), F32),
        compiler_params=_cp(("arbitrary",)), name="moe_combine",
    )(*lists, xn, pos, mod, fnw, yb)


def _moe_cap(tm):
    return max(2 * tm + N_EXPERTS * MOE_ALIGN, MOE_ROWS)


def _dma_lists(groups, src, dst, cap):
    def expand(count, first, stride, width):
        ends = jnp.cumsum(count, axis=1)
        k = jnp.arange(width, dtype=I32)
        run = jnp.minimum(jnp.sum(ends[:, None, :] <= k[None, :, None], axis=2), N_EXPERTS - 1)
        sel = run[:, :, None] == jnp.arange(N_EXPERTS, dtype=I32)[None, None, :]
        pick = lambda a: jnp.sum(jnp.where(sel, a[:, None, :], 0), axis=2)
        off = pick(first) + (k[None, :] - pick(ends - count)) * stride
        return (pick(src) + off).reshape(-1).astype(I32), (pick(dst) + off).reshape(-1).astype(I32)

    big_rows = MOE_BIG * MOE_ALIGN
    n_big, n_small = groups // MOE_BIG, groups % MOE_BIG
    big_src, big_dst = expand(n_big, jnp.zeros_like(groups), big_rows, cap // big_rows)
    small_src, small_dst = expand(n_small, n_big * big_rows, MOE_ALIGN, N_EXPERTS * (MOE_BIG - 1))
    total = lambda a: jnp.sum(a, axis=1).astype(I32)
    return total(n_big), total(n_small), big_src, big_dst, small_src, small_dst


def _moe(xn, h2, route, mod, fnw, w1, w3, w2, layer, L, final):
    T = xn.shape[0]
    tm = min(MOE_TILE, L)
    nt = T // tm
    pos, post, counts = _rank_call(route, tm)
    cnt = counts[:, 0, :N_EXPERTS].astype(I32)
    cnt = (cnt + MOE_ALIGN - 1) // MOE_ALIGN * MOE_ALIGN
    total = jnp.sum(cnt, axis=0)
    padded = (total + MOE_ROWS - 1) // MOE_ROWS * MOE_ROWS
    pend = jnp.cumsum(padded)
    seg_dst = (pend - padded)[None, :] + jnp.cumsum(cnt, axis=0) - cnt
    seg_src = jnp.cumsum(cnt, axis=1) - cnt
    nb = -(-(2 * T + nt * N_EXPERTS * (MOE_ALIGN - 1) + N_EXPERTS * (MOE_ROWS - 1)) // MOE_ROWS)
    blk_e = jnp.minimum(jnp.sum(pend[None, :] <= (jnp.arange(nb, dtype=I32) * MOE_ROWS)[:, None], axis=1),
                        N_EXPERTS - 1).astype(I32)
    n_used = (pend[N_EXPERTS - 1:] // MOE_ROWS).astype(I32)
    last_blk = jnp.concatenate([jnp.where(padded > 0, pend // MOE_ROWS - 1, -1).astype(I32), n_used])
    lists = _dma_lists(cnt // MOE_ALIGN, seg_src, seg_dst, _moe_cap(tm))
    xg = _disp_call(lists, last_blk, h2, route, post, nb * MOE_ROWS, tm)
    yb = _expert_call(blk_e, n_used, xg, w1, w3, w2, layer)
    return _comb_call(lists, xn, pos, mod, fnw, yb, L, tm, final)


def _in1_kernel(x_ref, mod_ref, nw_ref, win_ref, qnw_ref, wuq_ref, kvnw_ref, tq_ref, tk_ref,
                q_ref, ckv_ref, kr_ref):
    h = _norm_mod(x_ref[...], nw_ref[...], mod_ref[0, 0], mod_ref[0, 1])
    zz = _dot_bf(h, win_ref[...])
    cqn = _rms(zz[:, :MLA_Q_RANK]) * qnw_ref[...]
    q = _dot_bf(cqn, wuq_ref[...])
    tq = tq_ref[...]
    for hh in range(MLA_H):
        q_ref[:, hh * LANES:(hh + 1) * LANES] = (q[:, hh * LANES:(hh + 1) * LANES] * tq).astype(BF16)
    c0 = MLA_Q_RANK
    ckv_ref[...] = _rms(zz[:, c0:c0 + MLA_KV_RANK]) * kvnw_ref[...]
    c1 = c0 + MLA_KV_RANK
    tk = tk_ref[...]
    kr_ref[...] = (zz[:, c1:c1 + MLA_ROPE] * tk[:, :MLA_ROPE]
                   + zz[:, c1 + LANES:c1 + LANES + MLA_ROPE] * tk[:, MLA_ROPE:])


def _in1_call(x, mod, nw, w_in_r, q_nw, w_uq_r, kv_nw, tab_q, tab_k, L, tm):
    T = x.shape[0]
    per_b = L // tm
    tok = lambda i: (i, 0)
    const = lambda i: (0, 0)
    pos = lambda i: (i % per_b, 0)
    return pl.pallas_call(
        _in1_kernel, grid=(T // tm,),
        in_specs=[pl.BlockSpec((tm, D_MODEL), tok),
                  pl.BlockSpec((1, 3, 1, D_MODEL), lambda i: (i // per_b, 0, 0, 0)),
                  pl.BlockSpec((1, D_MODEL), const),
                  pl.BlockSpec((D_MODEL, MLA_IN_R), const),
                  pl.BlockSpec((1, MLA_Q_RANK), const),
                  pl.BlockSpec((MLA_Q_RANK, MLA_H * LANES), const),
                  pl.BlockSpec((1, MLA_KV_RANK), const),
                  pl.BlockSpec((tm, LANES), pos),
                  pl.BlockSpec((tm, 2 * MLA_ROPE), pos)],
        out_specs=[pl.BlockSpec((tm, MLA_H * LANES), tok),
                   pl.BlockSpec((tm, MLA_KV_RANK), tok),
                   pl.BlockSpec((tm, MLA_ROPE), tok)],
        out_shape=[_sds((T, MLA_H * LANES), BF16), _sds((T, MLA_KV_RANK), F32),
                   _sds((T, MLA_ROPE), F32)],
        compiler_params=_cp(("arbitrary",)), name="in1",
    )(x, mod, nw, w_in_r, q_nw, w_uq_r, kv_nw, tab_q, tab_k)


def _kv_kernel(ckv_ref, kr_ref, wk_ref, pe_ref, wv_ref, one_ref, k_ref, v_ref):
    c = ckv_ref[...].astype(BF16)
    k_ref[...] = (jnp.dot(c, wk_ref[...], preferred_element_type=F32)
                  + jnp.dot(kr_ref[...].astype(BF16), pe_ref[...], preferred_element_type=F32)).astype(BF16)
    v_ref[...] = (jnp.dot(c, wv_ref[...], preferred_element_type=F32) + one_ref[...]).astype(BF16)


def _kv_call(ckv, kr, wk_r, place, wv_r, ones_row, tm):
    T = ckv.shape[0]
    tok = lambda i: (i, 0)
    const = lambda i: (0, 0)
    wide = MLA_H * LANES
    return pl.pallas_call(
        _kv_kernel, grid=(T // tm,),
        in_specs=[pl.BlockSpec((tm, MLA_KV_RANK), tok),
                  pl.BlockSpec((tm, MLA_ROPE), tok),
                  pl.BlockSpec((MLA_KV_RANK, wide), const),
                  pl.BlockSpec((MLA_ROPE, wide), const),
                  pl.BlockSpec((MLA_KV_RANK, wide), const),
                  pl.BlockSpec((1, wide), const)],
        out_specs=[pl.BlockSpec((tm, wide), tok), pl.BlockSpec((tm, wide), tok)],
        out_shape=[_sds((T, wide), BF16), _sds((T, wide), BF16)],
        compiler_params=_cp(("arbitrary",)), name="mla_kv",
    )(ckv, kr, wk_r, place, wv_r, ones_row)


def _attn_kernel(q_ref, k_ref, v_ref, o_ref, m0, m1, a0, a1, s0, s1, *, tq, tk, Lk, pos0):
    m_scs, acc_scs, s_scs = (m0, m1), (a0, a1), (s0, s1)
    q_lo = pos0 + pl.program_id(2) * tq
    k_hi = jnp.minimum(((q_lo + tq - 1) // CHUNK + 1) * CHUNK, Lk)
    n_blk = (k_hi + tk - 1) // tk
    n_full = jnp.minimum(((q_lo // CHUNK + 1) * CHUNK) // tk, n_blk)
    q_chunk = jnp.right_shift(q_lo + lax.broadcasted_iota(I32, (tq, tk), 0), CHUNK_SHIFT)
    k_iota = lax.broadcasted_iota(I32, (tq, tk), 1)

    def scores(hh, k0):
        hc = slice(hh * LANES, (hh + 1) * LANES)
        return lax.dot_general(q_ref[:, hc], k_ref[pl.ds(k0, tk), hc], NT, preferred_element_type=F32)

    for hh in range(2):
        m_scs[hh][...] = jnp.full((tq, LANES), NEG, F32)
        acc_scs[hh][...] = jnp.zeros((tq, LANES), F32)
        s_scs[hh][...] = scores(hh, 0)

    def step(j, masked, look_ahead):
        k0 = pl.multiple_of(j * tk, tk)
        if look_ahead:
            ahead = [scores(hh, pl.multiple_of((j + 1) * tk, tk)) for hh in range(2)]
        for hh in range(2):
            hc = slice(hh * LANES, (hh + 1) * LANES)
            s = s_scs[hh][...]
            if masked:
                s = jnp.where(q_chunk >= jnp.right_shift(k0 + k_iota, CHUNK_SHIFT), s, NEG)
            m_prev = m_scs[hh][...]
            m_new = jnp.maximum(m_prev, jnp.max(s, axis=-1, keepdims=True))
            if tk % LANES == 0:
                p = jnp.exp2(s - jnp.concatenate([m_new] * (tk // LANES), axis=1))
            else:
                p = jnp.exp2(s - m_new[:, 0:1])
            acc_scs[hh][...] = (jnp.exp2(m_prev - m_new) * acc_scs[hh][...]
                                + jnp.dot(p.astype(BF16), v_ref[pl.ds(k0, tk), hc],
                                          preferred_element_type=F32))
            m_scs[hh][...] = m_new
        if look_ahead:
            for hh in range(2):
                s_scs[hh][...] = ahead[hh]

    def loop(lo, hi, masked, unroll=1):
        def body(jj, carry):
            for u in range(unroll):
                step(lo + jj * unroll + u, masked, True)
            return carry
        lax.fori_loop(0, (hi - lo) // unroll, body, 0)
        return lo + (hi - lo) // unroll * unroll

    n_pre = jnp.minimum(n_full, n_blk - 1)
    done = loop(0, n_pre, False, unroll=2)
    loop(done, n_pre, False)
    loop(n_full, n_blk - 1, True)

    @pl.when(n_full < n_blk)
    def _():
        step(n_blk - 1, True, False)

    @pl.when(n_full == n_blk)
    def _():
        step(n_blk - 1, False, False)

    outs = []
    for hh in range(2):
        acc = acc_scs[hh][...]
        outs.append(acc[:, :MLA_V] / acc[:, MLA_V:MLA_V + 1])
    o_ref[...] = jnp.concatenate(outs, axis=1).astype(BF16)


def _attn_call(q, k, v, B, Lq, Lk, pos0, tq, tk):
    nq = Lq // tq
    return pl.pallas_call(
        functools.partial(_attn_kernel, tq=tq, tk=tk, Lk=Lk, pos0=pos0), grid=(B, MLA_H // 2, nq),
        in_specs=[pl.BlockSpec((tq, 2 * LANES), lambda b, hp, i: (b * nq + i, hp)),
                  pl.BlockSpec((Lk, 2 * LANES), lambda b, hp, i: (b, hp)),
                  pl.BlockSpec((Lk, 2 * LANES), lambda b, hp, i: (b, hp))],
        out_specs=pl.BlockSpec((tq, 2 * MLA_V), lambda b, hp, i: (b * nq + i, hp)),
        out_shape=_sds((B * Lq, MLA_H * MLA_V), BF16),
        scratch_shapes=[pltpu.VMEM((tq, LANES), F32)] * 4 + [pltpu.VMEM((tq, tk), F32)] * 2,
        compiler_params=_cp(("arbitrary", "arbitrary", "arbitrary")), name="mla_attn",
    )(q, k, v)


def _split_cols(w, widths):
    offs = [0]
    for n in widths:
        offs.append(offs[-1] + n)
    return [w[:, offs[i]:offs[i + 1]] for i in range(len(widths))]


def _hi_lo_pair(w):
    hi = w.astype(BF16)
    return jnp.stack([hi, (w - hi.astype(F32)).astype(BF16)], axis=-3)


def _prep_even(ab_w_in, gla_w_alpha, gdn_a_log, gdn_dt_bias):
    gq, gk, gv, glr, gr, qkv, ga, gb, gg = _split_cols(ab_w_in, AB_IN_WIDTHS)
    w_main = jnp.concatenate([gq, gk, gv, gr, qkv, gg], axis=1).astype(BF16)
    small = jnp.concatenate([glr, ga, gb], axis=1)
    w_small = _hi_lo_pair(jnp.pad(small, ((0, 0), (0, LANES - small.shape[1]))))
    wa_pad = jnp.pad(gla_w_alpha, ((0, LANES - GLA_LR), (0, 0)))
    alog_v = jnp.zeros((1, LANES), F32).at[0, LANE_GA:LANE_GA + GDN_H].set(gdn_a_log)
    dtb_v = jnp.zeros((1, LANES), F32).at[0, LANE_GA:LANE_GA + GDN_H].set(gdn_dt_bias)
    return w_main, w_small, wa_pad, alog_v, dtb_v


def _swap_halves(w):
    half = w.shape[-1] // 2
    return jnp.concatenate([w[..., half:], w[..., :half]], axis=-1)


def _prep_odd(mla_w_in, mla_w_uq, mla_w_ukv):
    cq, ckv, kr = _split_cols(mla_w_in, (MLA_Q_RANK, MLA_KV_RANK, MLA_ROPE))
    pad = lambda w: jnp.pad(w, ((0, 0), (0, LANES - w.shape[1])))
    w_in_r = jnp.concatenate([cq, ckv, pad(kr), pad(_swap_halves(kr))], axis=1).astype(BF16)
    uq = mla_w_uq.reshape(MLA_Q_RANK, MLA_H, MLA_NOPE + MLA_ROPE)
    uq_rope = uq[..., MLA_NOPE:]
    w_uq_r = jnp.concatenate([uq, _swap_halves(uq_rope)], axis=-1).reshape(
        MLA_Q_RANK, MLA_H * LANES).astype(BF16)
    ukv = mla_w_ukv.reshape(MLA_KV_RANK, MLA_H, MLA_NOPE + MLA_V)
    wk_r = jnp.pad(ukv[..., :MLA_NOPE], ((0, 0), (0, 0), (0, LANES - MLA_NOPE))).reshape(
        MLA_KV_RANK, MLA_H * LANES).astype(BF16)
    wv_r = jnp.pad(ukv[..., MLA_NOPE:], ((0, 0), (0, 0), (0, LANES - MLA_V))).reshape(
        MLA_KV_RANK, MLA_H * LANES).astype(BF16)
    ones_row = jnp.tile((jnp.arange(LANES) == MLA_V).astype(F32), MLA_H)[None, :]
    eye = jnp.eye(MLA_ROPE, dtype=F32)
    place = jnp.concatenate([jnp.zeros((MLA_ROPE, MLA_NOPE), F32), eye, eye], axis=1)
    place = jnp.tile(place, (1, MLA_H)).astype(BF16)
    return w_in_r, w_uq_r, wk_r, wv_r, place, ones_row


def _rope_tables(pos0, L):
    half = MLA_ROPE // 2
    inv = jnp.exp(-math.log(ROPE_THETA) * jnp.arange(half, dtype=F32) / half)
    ang = (pos0 + jnp.arange(L, dtype=I32)).astype(F32)[:, None] * inv[None, :]
    cos, sin = jnp.cos(ang), jnp.sin(ang)
    tab_k = jnp.concatenate([cos, cos, -sin, sin], axis=1)
    scale = (MLA_NOPE + MLA_ROPE) ** -0.5 * math.log2(math.e)
    tab_q = jnp.concatenate([jnp.ones((L, MLA_NOPE), F32), tab_k], axis=1) * scale
    return tab_q, tab_k


def _mod4(mods, lo, hi):
    return [mods[i, lo:hi].reshape(hi - lo, 3, 1, D_MODEL) for i in range(mods.shape[0])]


def _trunk(x3, mods, pos0, gla_s, gdn_s, conv_s, past_ckv, past_kr, p):
    B, L, _ = x3.shape
    T = B * L
    tm = min(512, L)
    x = x3.reshape(T, D_MODEL)
    row = lambda a: a.reshape(1, -1)

    w_main, w_small, wa_pad, alog_v, dtb_v = p['even']
    zm, zs = _in0_call(x, mods[0], row(p['norm_w'][0, 0]), w_main, w_small, L, tm)
    o_gla, gla_t = _gla_call(zm, zs, wa_pad, row(p['gla_b_alpha'][0]), row(p['gla_norm_w'][0]),
                             jnp.swapaxes(gla_s, -1, -2), B, L)
    hist8 = jnp.pad(conv_s, ((0, 0), (8 - (CONV_W - 1), 0), (0, 0)))
    o_gdn, gdn_new = _gdn_call(zm, zs, p['gdn_conv_w'][0], alog_v, dtb_v, row(p['gdn_norm_w'][0]),
                               hist8, gdn_s, B, L)
    qkv0 = 3 * GLA_H * GLA_DV
    conv_new = zm.reshape(B, L, AB_MAIN)[:, L - (CONV_W - 1):, qkv0:qkv0 + GDN_CONV_DIM].astype(F32)
    w_out = p['ab_w_out_bf']
    half = GLA_H * GLA_DV
    xn, h2, route = _post_call([o_gla, o_gdn], [w_out[:half], w_out[half:]], x, mods[0], mods[1],
                               row(p['norm_w'][0, 1]), p['w_route'][0], p['b_route'][0], L, tm)
    x = _moe(xn, h2, route, mods[1], row(p['final_norm_w']), p['moe_w1'], p['moe_w3'], p['moe_w2'],
             0, L, final=False)

    w_in_r, w_uq_r, wk_r, wv_r, place, ones_row = p['odd']
    tab_q, tab_k = _rope_tables(pos0, L)
    q, ckv, kr = _in1_call(x, mods[2], row(p['norm_w'][1, 0]), w_in_r, row(p['mla_q_norm_w'][0]),
                           w_uq_r, row(p['mla_kv_norm_w'][0]), tab_q, tab_k, L, tm)
    if past_ckv is None:
        ckv_all, kr_all, Lk = ckv, kr, L
    else:
        Lk = past_ckv.shape[1] + L
        ckv_all = jnp.concatenate([past_ckv, ckv.reshape(B, L, -1)], axis=1).reshape(B * Lk, -1)
        kr_all = jnp.concatenate([past_kr, kr.reshape(B, L, -1)], axis=1).reshape(B * Lk, -1)
    tkv = 512 if (B * Lk) % 512 == 0 else Lk
    k_all, v_all = _kv_call(ckv_all, kr_all, wk_r, place, wv_r, ones_row, tkv)
    tq = min(ATTN_TILE, L)
    tk = ATTN_TILE if Lk % ATTN_TILE == 0 else Lk
    att = _attn_call(q, k_all, v_all, B, L, Lk, pos0, tq, tk)
    xn, h2, route = _post_call([att], [p['mla_w_out_bf']], x, mods[2], mods[3],
                               row(p['norm_w'][1, 1]), p['w_route'][1], p['b_route'][1], L, tm)
    y = _moe(xn, h2, route, mods[3], row(p['final_norm_w']), p['moe_w1'], p['moe_w3'], p['moe_w2'],
             1, L, final=True)
    return (y.reshape(B, L, D_MODEL), jnp.swapaxes(gla_t, -1, -2)[None], gdn_new[None], conv_new[None],
            ckv.reshape(1, B, L, MLA_KV_RANK), kr.reshape(1, B, L, MLA_ROPE))


def kernel(x_prompt, x_sample, state_gla, state_gdn, state_gdn_conv, cache_mla_ckv, cache_mla_krope,
           c_prompt, c_sample, ada_w, ada_b, norm_w, final_norm_w, ab_w_in, gla_w_alpha, gla_b_alpha,
           gla_norm_w, gdn_conv_w, gdn_a_log, gdn_dt_bias, gdn_norm_w, ab_w_out, mla_w_in, mla_q_norm_w,
           mla_w_uq, mla_kv_norm_w, mla_w_ukv, mla_w_out, moe_w_group, moe_b_group, moe_w_expert,
           moe_b_expert, moe_w1, moe_w3, moe_w2):
    depth = ada_w.shape[0]
    bp, bs = x_prompt.shape[0], x_sample.shape[0]
    mods = _ada_call(jnp.concatenate([c_prompt, c_sample], axis=0),
                     ada_w.reshape(2 * depth, D_MODEL, 3 * D_MODEL), ada_b.reshape(2 * depth, 1, 3 * D_MODEL))
    w_route = _hi_lo_pair(jnp.pad(jnp.concatenate([moe_w_group, moe_w_expert], axis=-1),
                                  ((0, 0), (0, 0), (0, LANES - N_GROUPS - N_EXPERTS))))
    b_route = jnp.pad(jnp.concatenate([moe_b_group, moe_b_expert], axis=-1),
                      ((0, 0), (0, LANES - N_GROUPS - N_EXPERTS)))[:, None, :]
    p = dict(norm_w=norm_w, final_norm_w=final_norm_w, gla_b_alpha=gla_b_alpha, gla_norm_w=gla_norm_w,
             gdn_conv_w=gdn_conv_w, gdn_norm_w=gdn_norm_w, mla_q_norm_w=mla_q_norm_w,
             mla_kv_norm_w=mla_kv_norm_w, moe_w1=moe_w1, moe_w3=moe_w3, moe_w2=moe_w2,
             w_route=w_route, b_route=b_route,
             even=_prep_even(ab_w_in[0], gla_w_alpha[0], gdn_a_log[0], gdn_dt_bias[0]),
             odd=_prep_odd(mla_w_in[0], mla_w_uq[0], mla_w_ukv[0]),
             ab_w_out_bf=ab_w_out[0].astype(BF16), mla_w_out_bf=mla_w_out[0].astype(BF16))
    y_p, gla_p, gdn_p, conv_p, ckv_p, kr_p = _trunk(
        x_prompt, _mod4(mods, 0, bp), 0,
        jnp.zeros((bp, GLA_H, GLA_DK, GLA_DV), F32), jnp.zeros((bp, GDN_H, GDN_DK, GDN_DV), F32),
        jnp.zeros((bp, CONV_W - 1, GDN_CONV_DIM), F32), None, None, p)
    y_s, gla_s, gdn_s, conv_s, ckv_s, kr_s = _trunk(
        x_sample, _mod4(mods, bp, bp + bs), cache_mla_ckv.shape[2],
        state_gla[0], state_gdn[0], state_gdn_conv[0], cache_mla_ckv[0], cache_mla_krope[0], p)
    return (y_p, y_s, gla_p, gdn_p, conv_p, ckv_p, kr_p, gla_s, gdn_s, conv_s, ckv_s, kr_s)
```

```python
import functools
import math

import jax
import jax.numpy as jnp
from jax import lax
from jax.experimental import pallas as pl
from jax.experimental.pallas import tpu as pltpu

F32 = jnp.float32
BF16 = jnp.bfloat16
I32 = jnp.int32
HI = lax.Precision.HIGHEST

D_MODEL = 1024
CHUNK = 64
GLA_H, GLA_DK, GLA_DV, GLA_LR, GLA_TAU = 4, 64, 128, 16, 16.0
GDN_H, GDN_DK, GDN_DV, CONV_W = 4, 128, 128, 4
GDN_CONV_DIM = GDN_H * (2 * GDN_DK + GDN_DV)
AB_IN_WIDTHS = (GLA_H * GLA_DK, GLA_H * GLA_DK, GLA_H * GLA_DV, GLA_LR, GLA_H * GLA_DV,
                GDN_CONV_DIM, GDN_H, GDN_H, GDN_H * GDN_DV)
AB_MAIN = 3584
MLA_H, MLA_NOPE, MLA_ROPE, MLA_V = 16, 64, 32, 64
MLA_Q_RANK, MLA_KV_RANK = 384, 256
MLA_IN_R = MLA_Q_RANK + MLA_KV_RANK + 256
ROPE_THETA = 10000.0
N_GROUPS, EXPERTS_PER_GROUP, N_EXPERTS, D_EXPERT = 4, 8, 32, 512
GROUP_SHIFT = int(math.log2(EXPERTS_PER_GROUP))
CHUNK_SHIFT = int(math.log2(CHUNK))
EPS = 1e-6

LANES = 128
GLA_SUB = 16
GLA_BATCH = 2
SCAN_CHUNKS = 4
GLA_SAFE_LOG_DECAY = -60.0
MOE_ROWS = 256
MOE_TILE = 512
MOE_ALIGN = 8
MOE_BIG = 4
MOE_XCOLS = D_MODEL + LANES
ATTN_TILE = 512
NEG = -3.0e38
VMEM_LIMIT = 56 * 1024 * 1024

LANE_GA = GLA_LR
LANE_GB = GLA_LR + GDN_H

NT = (((1,), (1,)), ((), ()))
TN = (((0,), (0,)), ((), ()))


def _cp(sem, vmem=None):
    return pltpu.CompilerParams(dimension_semantics=sem, vmem_limit_bytes=vmem or VMEM_LIMIT)


def _sds(shape, dtype):
    return jax.ShapeDtypeStruct(shape, dtype)


def _sigmoid(x):
    return 1.0 / (1.0 + jnp.exp(-x))


def _softplus(x):
    return jnp.maximum(x, 0.0) + jnp.log(1.0 + jnp.exp(-jnp.abs(x)))


def _rms(x):
    return x * lax.rsqrt(jnp.mean(x * x, axis=-1, keepdims=True) + EPS)


def _norm_mod(x, nw, shift, scale):
    return (_rms(x) * nw) * (1.0 + scale) + shift


def _dot_bf(a, b):
    return jnp.dot(a.astype(BF16), b.astype(BF16), preferred_element_type=F32)


def _dot_hi(a, b):
    return jnp.dot(a, b, precision=HI, preferred_element_type=F32)


def _ada_kernel(c_ref, w_ref, b_ref, o_ref):
    c = c_ref[...]
    o_ref[0] = _dot_hi(c * _sigmoid(c), w_ref[0]) + b_ref[0]


def _ada_call(c_all, ada_w, ada_b):
    n = ada_w.shape[0]
    nb = c_all.shape[0]
    return pl.pallas_call(
        _ada_kernel, grid=(n, 3),
        in_specs=[pl.BlockSpec((nb, D_MODEL), lambda i, j: (0, 0)),
                  pl.BlockSpec((1, D_MODEL, D_MODEL), lambda i, j: (i, 0, j)),
                  pl.BlockSpec((1, 1, D_MODEL), lambda i, j: (i, 0, j))],
        out_specs=pl.BlockSpec((1, nb, D_MODEL), lambda i, j: (i, 0, j)),
        out_shape=_sds((n, nb, 3 * D_MODEL), F32),
        compiler_params=_cp(("arbitrary", "arbitrary")), name="ada",
    )(c_all, ada_w, ada_b)


def _in0_kernel(x_ref, mod_ref, nw_ref, wm_ref, ws_ref, zm_ref, zs_ref):
    h = _norm_mod(x_ref[...], nw_ref[...], mod_ref[0, 0], mod_ref[0, 1])
    h_hl = _split_bf(h)
    zm_ref[...] = jnp.dot(h_hl[0], wm_ref[...], preferred_element_type=F32).astype(BF16)
    zs_ref[...] = _dot3(h_hl, (ws_ref[0], ws_ref[1]))


def _in0_call(x, mod, nw, w_main, w_small, L, tm):
    T = x.shape[0]
    per_b = L // tm
    return pl.pallas_call(
        _in0_kernel, grid=(T // tm,),
        in_specs=[pl.BlockSpec((tm, D_MODEL), lambda i: (i, 0)),
                  pl.BlockSpec((1, 3, 1, D_MODEL), lambda i: (i // per_b, 0, 0, 0)),
                  pl.BlockSpec((1, D_MODEL), lambda i: (0, 0)),
                  pl.BlockSpec((D_MODEL, AB_MAIN), lambda i: (0, 0)),
                  pl.BlockSpec((2, D_MODEL, LANES), lambda i: (0, 0, 0))],
        out_specs=[pl.BlockSpec((tm, AB_MAIN), lambda i: (i, 0)),
                   pl.BlockSpec((tm, LANES), lambda i: (i, 0))],
        out_shape=[_sds((T, AB_MAIN), BF16), _sds((T, LANES), F32)],
        compiler_params=_cp(("arbitrary",)), name="in0",
    )(x, mod, nw, w_main, w_small)


def _gated_head_out(o, nw, gate):
    gate = gate.astype(F32)
    return (_rms(o) * nw * (gate * _sigmoid(gate))).astype(BF16)


def _gla_chunk_exact(q, k, v, b, st):
    CL = q.shape[0]
    row = lax.broadcasted_iota(I32, (GLA_SUB, 1), 0)
    bprev = jnp.zeros((1, GLA_DK), F32)
    outs = []
    for blk in range(CL // GLA_SUB):
        sl = slice(blk * GLA_SUB, (blk + 1) * GLA_SUB)
        qi, ki, vi = q[sl], k[sl], v[sl]
        brel = b[sl] - bprev
        bend = brel[GLA_SUB - 1:GLA_SUB]
        oi = lax.dot_general((qi * jnp.exp(brel)).astype(BF16), st.astype(BF16), NT,
                             preferred_element_type=F32)
        for j in range(GLA_SUB):
            d = brel - brel[j:j + 1]
            e = jnp.where(row >= j, jnp.exp(jnp.minimum(d, 0.0)), 0.0)
            a = jnp.sum(qi * ki[j:j + 1] * e, axis=-1, keepdims=True)
            oi = oi + a * vi[j:j + 1]
        outs.append(oi)
        khat = ki * jnp.exp(bend - brel)
        st = st * jnp.exp(bend) + lax.dot_general(vi.astype(BF16), khat.astype(BF16), TN,
                                                  preferred_element_type=F32)
        bprev = b[(blk + 1) * GLA_SUB - 1:(blk + 1) * GLA_SUB]
    return jnp.concatenate(outs, axis=0), st


def _gla_chunk_fast(q, k, v, b, st, causal):
    CL = q.shape[0]
    qh = (q * jnp.exp(b)).astype(BF16)
    kh = (k * jnp.exp(-b)).astype(BF16)
    vb = v.astype(BF16)
    att = jnp.where(causal, lax.dot_general(qh, kh, NT, preferred_element_type=F32), 0.0)
    o = (lax.dot_general(qh, st.astype(BF16), NT, preferred_element_type=F32)
         + jnp.dot(att.astype(BF16), vb, preferred_element_type=F32))
    blast = b[CL - 1:CL]
    kdec = (k * jnp.exp(blast - b)).astype(BF16)
    st = st * jnp.exp(blast) + lax.dot_general(vb, kdec, TN, preferred_element_type=F32)
    return o, st


def _gla_kernel(q_ref, k_ref, v_ref, gr_ref, zs_ref, wa_ref, ba_ref, nw_ref, s0_ref,
                o_ref, st_ref, st_sc, b_sc, *, CL, nch, nsteps, nb):
    step = pl.program_id(1)

    @pl.when(step == 0)
    def _():
        st_sc[...] = s0_ref[...]

    r = lax.broadcasted_iota(I32, (CL, CL), 0)
    cc = lax.broadcasted_iota(I32, (CL, CL), 1)
    causal = r >= cc
    tri = causal.astype(F32)
    b_min = None
    for g in range(nb):
        for ch in range(nch):
            rows = slice(ch * CL, (ch + 1) * CL)
            pre = _dot_hi(zs_ref[g, rows, :], wa_ref[...]) + ba_ref[...]
            la = -_softplus(-pre) * (1.0 / GLA_TAU)
            b_ch = _dot_hi(tri, la)
            b_sc[g, rows, :] = b_ch
            lo = jnp.min(b_ch[CL - 1:CL, :])
            b_min = lo if b_min is None else jnp.minimum(b_min, lo)
    safe = b_min > GLA_SAFE_LOG_DECAY

    def run(chunk_fn):
        nw = nw_ref[...]
        units = [(g, h) for g in range(nb) for h in range(GLA_H)]
        st = {un: st_sc[un[0], un[1]] for un in units}
        for ch in range(nch):
            rows = slice(ch * CL, (ch + 1) * CL)
            for g, h in units:
                kc = slice(h * GLA_DK, (h + 1) * GLA_DK)
                vc = slice(h * GLA_DV, (h + 1) * GLA_DV)
                q = q_ref[g, rows, kc].astype(F32) * GLA_DK ** -0.5
                o, st[g, h] = chunk_fn(q, k_ref[g, rows, kc].astype(F32), v_ref[g, rows, vc].astype(F32),
                                       b_sc[g, rows, kc], st[g, h])
                o_ref[g, rows, vc] = _gated_head_out(o, nw, gr_ref[g, rows, vc])
        for g, h in units:
            st_sc[g, h] = st[g, h]

    @pl.when(safe)
    def _():
        run(functools.partial(_gla_chunk_fast, causal=causal))

    @pl.when(jnp.logical_not(safe))
    def _():
        run(_gla_chunk_exact)

    @pl.when(step == nsteps - 1)
    def _():
        st_ref[...] = st_sc[...]


def _gla_call(zm, zs, wa_pad, b_alpha, nw, s0t, B, L):
    CL = min(CHUNK, L)
    nch = SCAN_CHUNKS if L % (CL * SCAN_CHUNKS) == 0 else 1
    nb = GLA_BATCH if B % GLA_BATCH == 0 else 1
    rows = CL * nch
    nsteps = L // rows
    qk_w = GLA_H * GLA_DK
    v_w = GLA_H * GLA_DV
    zm3 = zm.reshape(B, L, AB_MAIN)
    zs3 = zs.reshape(B, L, LANES)
    tok = lambda col: (lambda b, c: (b, c, col))
    const = lambda b, c: (0, 0)
    state = lambda b, c: (b, 0, 0, 0)
    o, st = pl.pallas_call(
        functools.partial(_gla_kernel, CL=CL, nch=nch, nsteps=nsteps, nb=nb), grid=(B // nb, nsteps),
        in_specs=[pl.BlockSpec((nb, rows, qk_w), tok(0)),
                  pl.BlockSpec((nb, rows, qk_w), tok(1)),
                  pl.BlockSpec((nb, rows, v_w), tok(1)),
                  pl.BlockSpec((nb, rows, v_w), tok(2)),
                  pl.BlockSpec((nb, rows, LANES), tok(0)),
                  pl.BlockSpec((LANES, qk_w), const),
                  pl.BlockSpec((1, qk_w), const),
                  pl.BlockSpec((1, GLA_DV), const),
                  pl.BlockSpec((nb, GLA_H, GLA_DV, GLA_DK), state)],
        out_specs=[pl.BlockSpec((nb, rows, v_w), tok(0)),
                   pl.BlockSpec((nb, GLA_H, GLA_DV, GLA_DK), state)],
        out_shape=[_sds((B, L, v_w), BF16), _sds((B, GLA_H, GLA_DV, GLA_DK), F32)],
        scratch_shapes=[pltpu.VMEM((nb, GLA_H, GLA_DV, GLA_DK), F32), pltpu.VMEM((nb, rows, qk_w), F32)],
        compiler_params=_cp(("arbitrary", "arbitrary")), name="gla",
    )(zm3, zm3, zm3, zm3, zs3, wa_pad, b_alpha, nw, s0t)
    return o.reshape(B * L, v_w), st


def _split_bf(a):
    hi = a.astype(BF16)
    return hi, (a - hi.astype(F32)).astype(BF16)


def _dot3(a_hl, b_hl):
    (ah, al), (bh, bl) = a_hl, b_hl
    d = functools.partial(jnp.dot, preferred_element_type=F32)
    return d(ah, bh) + d(ah, bl) + d(al, bh)


def _gdn_kernel(qkv_ref, zs_ref, gg_ref, cw_ref, alog_ref, dtb_ref, nw_ref, hist_ref, s0_ref,
                o_ref, s_ref, cv_sc, st_sc, *, CL, nch, nsteps):
    step = pl.program_id(1)
    R = CL * nch

    @pl.when(step == 0)
    def _():
        cv_sc[0:8, :] = hist_ref[0]
        st_sc[...] = s0_ref[0]

    x = qkv_ref[...].astype(F32)
    cv_sc[8:8 + R, :] = x
    cw = cw_ref[...]
    conv = (cv_sc[5:5 + R, :] * cw[0:1] + cv_sc[6:6 + R, :] * cw[1:2]
            + cv_sc[7:7 + R, :] * cw[2:3] + x * cw[3:4])
    cv_sc[0:8, :] = x[R - 8:R]
    conv = conv * _sigmoid(conv)

    zs = zs_ref[...]
    g_all = -jnp.exp(alog_ref[...]) * _softplus(zs + dtb_ref[...])
    beta_all = _sigmoid(zs)
    r = lax.broadcasted_iota(I32, (CL, CL), 0)
    cc = lax.broadcasted_iota(I32, (CL, CL), 1)
    tri = (r >= cc).astype(F32)
    eye = (r == cc).astype(F32)
    nw = nw_ref[...]
    qk_w = GDN_H * GDN_DK

    units = [(ch, h) for ch in range(nch) for h in range(GDN_H)]
    stage = {}
    for ch in range(nch):
        rows = slice(ch * CL, (ch + 1) * CL)
        gam = _dot_hi(tri, g_all[rows])
        gam_t = gam.T
        for h in range(GDN_H):
            q = conv[rows, h * GDN_DK:(h + 1) * GDN_DK]
            k = conv[rows, qk_w + h * GDN_DK:qk_w + (h + 1) * GDN_DK]
            v = conv[rows, 2 * qk_w + h * GDN_DV:2 * qk_w + (h + 1) * GDN_DV]
            q = q * lax.rsqrt(jnp.sum(q * q, axis=-1, keepdims=True) + EPS) * GDN_DK ** -0.5
            k = k * lax.rsqrt(jnp.sum(k * k, axis=-1, keepdims=True) + EPS)
            gcol = gam[:, LANE_GA + h:LANE_GA + h + 1]
            grow = gam_t[LANE_GA + h:LANE_GA + h + 1, :]
            bcol = beta_all[rows, LANE_GB + h:LANE_GB + h + 1]
            dec = jnp.where(r >= cc, jnp.exp(jnp.minimum(gcol - grow, 0.0)), 0.0)
            eg = jnp.exp(gcol)
            glast = gcol[CL - 1:CL]
            stage[ch, h] = dict(
                kb=k.astype(BF16), qb=q.astype(BF16), dec=dec, bcol=bcol,
                ub=(v * bcol).astype(BF16), wb=(k * (bcol * eg)).astype(BF16),
                qe=(q * eg).astype(BF16), kdec=(k * jnp.exp(glast - gcol)).astype(BF16),
                elast=jnp.exp(glast))
    for un in units:
        d = stage[un]
        kk = lax.dot_general(d['kb'], d['kb'], NT, preferred_element_type=F32)
        d['pw'] = jnp.where(r > cc, d['bcol'] * kk * d['dec'], 0.0)
        d['t'] = eye - d['pw']
    for _ in range(int(math.log2(CL)) - 1):
        for un in units:
            d = stage[un]
            hl = _split_bf(d['pw'])
            d['pw'] = _dot3(hl, hl)
        for un in units:
            d = stage[un]
            d['t'] = d['t'] + _dot3(_split_bf(d['t']), _split_bf(d['pw']))
    pre = {}
    for un in units:
        d = stage[un]
        tb = d['t'].astype(BF16)
        u = jnp.dot(tb, d['ub'], preferred_element_type=F32)
        w = jnp.dot(tb, d['wb'], preferred_element_type=F32).astype(BF16)
        qk = (lax.dot_general(d['qb'], d['kb'], NT, preferred_element_type=F32) * d['dec']).astype(BF16)
        pre[un] = (u, w, qk, d['qe'], d['kdec'], d['elast'])

    for h in range(GDN_H):
        vc = slice(h * GDN_DV, (h + 1) * GDN_DV)
        st = st_sc[h]
        for ch in range(nch):
            rows = slice(ch * CL, (ch + 1) * CL)
            u, w, qk, qe, kdec, elast = pre[ch, h]
            stb = st.astype(BF16)
            delta = u - jnp.dot(w, stb, preferred_element_type=F32)
            db = delta.astype(BF16)
            o = (jnp.dot(qe, stb, preferred_element_type=F32)
                 + jnp.dot(qk, db, preferred_element_type=F32))
            st = st * elast + lax.dot_general(kdec, db, TN, preferred_element_type=F32)
            o_ref[rows, vc] = _gated_head_out(o, nw, gg_ref[rows, vc])
        st_sc[h] = st

    @pl.when(step == nsteps - 1)
    def _():
        s_ref[0] = st_sc[...]


def _gdn_call(zm, zs, conv_w, alog_v, dtb_v, nw, hist8, s0, B, L):
    CL = min(CHUNK, L)
    nch = SCAN_CHUNKS if L % (CL * SCAN_CHUNKS) == 0 else 1
    rows = CL * nch
    nsteps = L // rows
    T = B * L
    v_w = GDN_H * GDN_DV
    row = lambda b, c: b * nsteps + c
    return pl.pallas_call(
        functools.partial(_gdn_kernel, CL=CL, nch=nch, nsteps=nsteps), grid=(B, nsteps),
        in_specs=[pl.BlockSpec((rows, GDN_CONV_DIM), lambda b, c: (row(b, c), 1)),
                  pl.BlockSpec((rows, LANES), lambda b, c: (row(b, c), 0)),
                  pl.BlockSpec((rows, v_w), lambda b, c: (row(b, c), 6)),
                  pl.BlockSpec((CONV_W, GDN_CONV_DIM), lambda b, c: (0, 0)),
                  pl.BlockSpec((1, LANES), lambda b, c: (0, 0)),
                  pl.BlockSpec((1, LANES), lambda b, c: (0, 0)),
                  pl.BlockSpec((1, GDN_DV), lambda b, c: (0, 0)),
                  pl.BlockSpec((1, 8, GDN_CONV_DIM), lambda b, c: (b, 0, 0)),
                  pl.BlockSpec((1, GDN_H, GDN_DK, GDN_DV), lambda b, c: (b, 0, 0, 0))],
        out_specs=[pl.BlockSpec((rows, v_w), lambda b, c: (row(b, c), 0)),
                   pl.BlockSpec((1, GDN_H, GDN_DK, GDN_DV), lambda b, c: (b, 0, 0, 0))],
        out_shape=[_sds((T, v_w), BF16), _sds((B, GDN_H, GDN_DK, GDN_DV), F32)],
        scratch_shapes=[pltpu.VMEM((8 + rows, GDN_CONV_DIM), F32),
                        pltpu.VMEM((GDN_H, GDN_DK, GDN_DV), F32)],
        compiler_params=_cp(("arbitrary", "arbitrary")), name="gdn",
    )(zm, zs, zm, conv_w, alog_v, dtb_v, nw, hist8, s0)


def _post_kernel(*refs, n_in):
    a_refs = refs[:n_in]
    w_refs = refs[n_in:2 * n_in]
    x_ref, mod1_ref, mod2_ref, nw2_ref, wr_ref, br_ref, xn_ref, h2_ref, rt_ref = refs[2 * n_in:]
    acc = jnp.dot(a_refs[0][...], w_refs[0][...], preferred_element_type=F32)
    for a_ref, w_ref in zip(a_refs[1:], w_refs[1:]):
        acc = acc + jnp.dot(a_ref[...], w_ref[...], preferred_element_type=F32)
    xn = x_ref[...] + mod1_ref[0, 2] * acc
    xn_ref[...] = xn
    h2 = _norm_mod(xn, nw2_ref[...], mod2_ref[0, 0], mod2_ref[0, 1])
    h2_ref[...] = h2
    logits = _dot3(_split_bf(h2), (wr_ref[0], wr_ref[1])) + br_ref[...]
    lane = lax.broadcasted_iota(I32, logits.shape, 1)
    is_g = lane < N_GROUPS
    gl = jnp.where(is_g, logits, NEG)
    gmax = jnp.max(gl, axis=-1, keepdims=True)
    g_sel = jnp.min(jnp.where(gl == gmax, lane, LANES), axis=-1, keepdims=True)
    p_g = 1.0 / jnp.sum(jnp.where(is_g, jnp.exp(logits - gmax), 0.0), axis=-1, keepdims=True)
    in_grp = jnp.logical_and(
        jnp.logical_and(lane >= N_GROUPS, lane < N_GROUPS + N_EXPERTS),
        jnp.right_shift(lane - N_GROUPS, GROUP_SHIFT) == g_sel)
    el = jnp.where(in_grp, logits, NEG)
    v1 = jnp.max(el, axis=-1, keepdims=True)
    i1 = jnp.min(jnp.where(jnp.logical_and(in_grp, el == v1), lane, LANES), axis=-1, keepdims=True)
    rest = jnp.logical_and(in_grp, lane != i1)
    el2 = jnp.where(rest, logits, NEG)
    v2 = jnp.max(el2, axis=-1, keepdims=True)
    i2 = jnp.min(jnp.where(jnp.logical_and(rest, el2 == v2), lane, LANES), axis=-1, keepdims=True)
    ex = jnp.exp(v2 - v1)
    w1 = 1.0 / (1.0 + ex)
    w2 = ex * w1
    e1 = (i1 - N_GROUPS).astype(F32)
    e2 = (i2 - N_GROUPS).astype(F32)
    rt_ref[...] = jnp.where(lane == 0, e1, jnp.where(lane == 1, e2, jnp.where(
        lane == 2, p_g * w1, jnp.where(lane == 3, p_g * w2, 0.0))))


def _post_call(a_list, w_list, x, mod1, mod2, nw2, w_route, b_route, L, tm):
    T = x.shape[0]
    per_b = L // tm
    n_in = len(a_list)
    tok = lambda i: (i, 0)
    const = lambda i: (0, 0)
    modmap = lambda i: (i // per_b, 0, 0, 0)
    in_specs = ([pl.BlockSpec((tm, a.shape[1]), tok) for a in a_list]
                + [pl.BlockSpec(w.shape, const) for w in w_list]
                + [pl.BlockSpec((tm, D_MODEL), tok),
                   pl.BlockSpec((1, 3, 1, D_MODEL), modmap),
                   pl.BlockSpec((1, 3, 1, D_MODEL), modmap),
                   pl.BlockSpec((1, D_MODEL), const),
                   pl.BlockSpec((2, D_MODEL, LANES), lambda i: (0, 0, 0)),
                   pl.BlockSpec((1, LANES), const)])
    return pl.pallas_call(
        functools.partial(_post_kernel, n_in=n_in), grid=(T // tm,),
        in_specs=in_specs,
        out_specs=[pl.BlockSpec((tm, D_MODEL), tok), pl.BlockSpec((tm, D_MODEL), tok),
                   pl.BlockSpec((tm, LANES), tok)],
        out_shape=[_sds((T, D_MODEL), F32), _sds((T, D_MODEL), F32), _sds((T, LANES), F32)],
        compiler_params=_cp(("arbitrary",)), name="post",
    )(*a_list, *w_list, x, mod1, mod2, nw2, w_route, b_route)


def _rank_kernel(rt_ref, pos_ref, post_ref, cnt_ref):
    rt = rt_ref[...]
    tm = rt.shape[0]
    lane = lax.broadcasted_iota(I32, rt.shape, 1)
    o1 = lane == rt[:, 0:1].astype(I32)
    o2 = lane == rt[:, 1:2].astype(I32)
    onehot = jnp.where(o1, 1.0, 0.0) + jnp.where(o2, 1.0, 0.0)
    r = lax.broadcasted_iota(I32, (tm, tm), 0)
    cc = lax.broadcasted_iota(I32, (tm, tm), 1)
    before = jnp.dot(jnp.where(r > cc, 1.0, 0.0).astype(BF16), onehot.astype(BF16),
                     preferred_element_type=F32)
    cnt = jnp.sum(onehot, axis=0, keepdims=True)
    er = lax.broadcasted_iota(I32, (LANES, LANES), 0)
    ec = lax.broadcasted_iota(I32, (LANES, LANES), 1)
    cnt8 = jnp.floor((cnt + (MOE_ALIGN - 1)) * (1.0 / MOE_ALIGN)) * MOE_ALIGN
    start = _dot_hi(jnp.broadcast_to(cnt8, (8, LANES)), jnp.where(er < ec, 1.0, 0.0))[0:1]
    where = before + start
    p1 = jnp.sum(jnp.where(o1, where, 0.0), axis=-1, keepdims=True)
    p2 = jnp.sum(jnp.where(o2, where, 0.0), axis=-1, keepdims=True)
    pos = jnp.where(lane == 0, p1, jnp.where(lane == 1, p2, 0.0))
    pos_ref[...] = pos
    post_ref[0] = pos.T[0:8, :].astype(I32)
    cnt_ref[0] = cnt


def _rank_call(route, tm):
    T = route.shape[0]
    nt = T // tm
    return pl.pallas_call(
        _rank_kernel, grid=(nt,),
        in_specs=[pl.BlockSpec((tm, LANES), lambda i: (i, 0))],
        out_specs=[pl.BlockSpec((tm, LANES), lambda i: (i, 0)),
                   pl.BlockSpec((1, 8, tm), lambda i: (i, 0, 0)),
                   pl.BlockSpec((1, 1, LANES), lambda i: (i, 0, 0))],
        out_shape=[_sds((T, LANES), F32), _sds((nt, 8, tm), I32), _sds((nt, 1, LANES), F32)],
        compiler_params=_cp(("arbitrary",)), name="moe_rank",
    )(route)


def _start_dmas(tile, lists, copy):
    n_big, n_small, big_src, big_dst, small_src, small_dst = lists
    max_big = big_src.shape[0] // n_big.shape[0]
    max_small = small_src.shape[0] // n_small.shape[0]

    def big(k, carry):
        idx = tile * max_big + k
        copy(pl.multiple_of(big_src[idx], MOE_ALIGN), pl.multiple_of(big_dst[idx], MOE_ALIGN),
             MOE_BIG * MOE_ALIGN).start()
        return carry

    def small(k, carry):
        idx = tile * max_small + k
        copy(pl.multiple_of(small_src[idx], MOE_ALIGN), pl.multiple_of(small_dst[idx], MOE_ALIGN),
             MOE_ALIGN).start()
        return carry

    lax.fori_loop(0, n_big[tile], big, 0)
    lax.fori_loop(0, n_small[tile], small, 0)


def _drain_dmas(n_big, n_small, copy):
    def big(c, carry):
        copy(0, 0, MOE_BIG * MOE_ALIGN).wait()
        return carry

    def small(c, carry):
        copy(0, 0, MOE_ALIGN).wait()
        return carry

    lax.fori_loop(0, n_big, big, 0)
    lax.fori_loop(0, n_small, small, 0)


def _disp_kernel(*refs, tm, nt):
    lists, (last_ref, h_ref, rt_ref, post_ref, xg_hbm, xs_sc, sems) = refs[:6], refs[6:]
    i = pl.program_id(0)
    slot = i % 2
    cap = xs_sc.shape[1]

    def runs(tile, sl, start):
        copy = lambda s, d, n: pltpu.make_async_copy(
            xs_sc.at[sl, pl.ds(s, n), :], xg_hbm.at[pl.ds(d, n), :], sems.at[sl])
        if start:
            _start_dmas(tile, lists, copy)
        else:
            _drain_dmas(lists[0][tile], lists[1][tile], copy)

    @pl.when(i == 0)
    def _():
        xs_sc[0, 0:MOE_ROWS, :] = jnp.zeros((MOE_ROWS, MOE_XCOLS), F32)
        zero_block = lambda blk: pltpu.make_async_copy(
            xs_sc.at[0, pl.ds(0, MOE_ROWS), :], xg_hbm.at[pl.ds(blk * MOE_ROWS, MOE_ROWS), :], sems.at[0])
        n_blocks = xg_hbm.shape[0] // MOE_ROWS

        def fill(e, act):
            @pl.when(last_ref[e] >= 0)
            def _():
                act(zero_block(last_ref[e]))

        def start_all(e, carry):
            fill(e, lambda cp: cp.start())
            return carry

        def wait_all(e, carry):
            fill(e, lambda cp: cp.wait())
            return carry

        def start_tail(blk, carry):
            zero_block(blk).start()
            return carry

        def wait_tail(blk, carry):
            zero_block(blk).wait()
            return carry

        lax.fori_loop(0, N_EXPERTS, start_all, 0)
        lax.fori_loop(last_ref[N_EXPERTS], n_blocks, start_tail, 0)
        lax.fori_loop(0, N_EXPERTS, wait_all, 0)
        lax.fori_loop(last_ref[N_EXPERTS], n_blocks, wait_tail, 0)

    @pl.when(i >= 2)
    def _():
        runs(i - 2, slot, False)

    post = post_ref[0]
    rows = lax.broadcasted_iota(I32, (cap, tm), 0)
    p1 = jnp.where(rows == post[0:1, :], 1.0, 0.0).astype(BF16)
    p2 = jnp.where(rows == post[1:2, :], 1.0, 0.0).astype(BF16)
    xs_sc[slot, :, 0:D_MODEL] = jnp.dot(p1 + p2, h_ref[...].astype(BF16), preferred_element_type=F32)
    rt = rt_ref[...]
    lane = lax.broadcasted_iota(I32, rt.shape, 1)

    def hi_lo(col):
        hi = col.astype(BF16).astype(F32)
        return jnp.where(lane == 0, hi, jnp.where(lane == 1, col - hi, 0.0)).astype(BF16)

    xs_sc[slot, :, D_MODEL:] = (jnp.dot(p1, hi_lo(rt[:, 2:3]), preferred_element_type=F32)
                                + jnp.dot(p2, hi_lo(rt[:, 3:4]), preferred_element_type=F32))
    runs(i, slot, True)

    @pl.when(i == nt - 1)
    def _():
        if nt > 1:
            runs(i - 1, 1 - slot, False)
        runs(i, slot, False)


def _disp_call(lists, last_blk, h2, route, post, n_rows, tm):
    T = h2.shape[0]
    nt = T // tm
    return pl.pallas_call(
        functools.partial(_disp_kernel, tm=tm, nt=nt),
        grid_spec=pltpu.PrefetchScalarGridSpec(
            num_scalar_prefetch=7, grid=(nt,),
            in_specs=[pl.BlockSpec((tm, D_MODEL), lambda i, *_: (i, 0)),
                      pl.BlockSpec((tm, LANES), lambda i, *_: (i, 0)),
                      pl.BlockSpec((1, 8, tm), lambda i, *_: (i, 0, 0))],
            out_specs=pl.BlockSpec(memory_space=pl.ANY),
            scratch_shapes=[pltpu.VMEM((2, _moe_cap(tm), MOE_XCOLS), F32), pltpu.SemaphoreType.DMA((2,))]),
        out_shape=_sds((n_rows, MOE_XCOLS), F32),
        compiler_params=_cp(("arbitrary",)), name="moe_dispatch",
    )(*lists, last_blk, h2, route, post)


def _expert_kernel(be_ref, nu_ref, x_ref, w1_ref, w3_ref, w2_ref, y_ref, w1b, w3b, w2b):
    i = pl.program_id(0)
    used = i < nu_ref[0]
    fresh = jnp.logical_or(i == 0, be_ref[i] != be_ref[jnp.maximum(i - 1, 0)])

    @pl.when(jnp.logical_and(used, fresh))
    def _():
        w1b[...] = w1_ref[0, 0].astype(BF16)
        w3b[...] = w3_ref[0, 0].astype(BF16)
        w2b[...] = w2_ref[0, 0].astype(BF16)

    @pl.when(used)
    def _():
        x = x_ref[:, 0:D_MODEL].astype(BF16)
        a = jnp.dot(x, w1b[...], preferred_element_type=F32)
        g = jnp.dot(x, w3b[...], preferred_element_type=F32)
        hm = (a * _sigmoid(a) * g).astype(BF16)
        row_w = x_ref[:, D_MODEL:D_MODEL + 1] + x_ref[:, D_MODEL + 1:D_MODEL + 2]
        y_ref[...] = jnp.dot(hm, w2b[...], preferred_element_type=F32) * row_w

    @pl.when(jnp.logical_not(used))
    def _():
        y_ref[...] = jnp.zeros(y_ref.shape, F32)


def _expert_call(blk_e, n_used, xg, w1, w3, w2, layer):
    P = xg.shape[0]
    nb = P // MOE_ROWS
    rowmap = lambda i, be, nu: (jnp.minimum(i, nu[0] - 1), 0)
    wmap = lambda i, be, nu: (layer, be[i], 0, 0)
    return pl.pallas_call(
        _expert_kernel,
        grid_spec=pltpu.PrefetchScalarGridSpec(
            num_scalar_prefetch=2, grid=(nb,),
            in_specs=[pl.BlockSpec((MOE_ROWS, MOE_XCOLS), rowmap),
                      pl.BlockSpec((1, 1, D_MODEL, D_EXPERT), wmap),
                      pl.BlockSpec((1, 1, D_MODEL, D_EXPERT), wmap),
                      pl.BlockSpec((1, 1, D_EXPERT, D_MODEL), wmap)],
            out_specs=pl.BlockSpec((MOE_ROWS, D_MODEL), lambda i, be, nu: (i, 0)),
            scratch_shapes=[pltpu.VMEM((D_MODEL, D_EXPERT), BF16),
                            pltpu.VMEM((D_MODEL, D_EXPERT), BF16),
                            pltpu.VMEM((D_EXPERT, D_MODEL), BF16)]),
        out_shape=_sds((P, D_MODEL), F32),
        compiler_params=_cp(("arbitrary",)), name="moe_experts",
    )(blk_e, n_used, xg, w1, w3, w2)


def _comb_kernel(*refs, tm, final, nt):
    lists, (x_ref, pos_ref, mod_ref, fnw_ref, y_hbm, o_ref, ys_sc, sems) = refs[:6], refs[6:]
    i = pl.program_id(0)
    slot = i % 2
    cap = ys_sc.shape[1]

    def runs(tile, sl, start):
        copy = lambda s, d, n: pltpu.make_async_copy(
            y_hbm.at[pl.ds(d, n), :], ys_sc.at[sl, pl.ds(s, n), :], sems.at[sl])
        if start:
            _start_dmas(tile, lists, copy)
        else:
            _drain_dmas(lists[0][tile], lists[1][tile], copy)

    def begin(tile, sl):
        ys_sc[sl, 2 * tm:cap, :] = jnp.zeros((cap - 2 * tm, D_MODEL), F32)
        runs(tile, sl, True)

    @pl.when(i == 0)
    def _():
        begin(0, 0)

    @pl.when(i + 1 < nt)
    def _():
        begin(i + 1, 1 - slot)

    runs(i, slot, False)
    pos = pos_ref[...]
    cols = lax.broadcasted_iota(I32, (tm, cap), 1)
    pick = (jnp.where(cols == pos[:, 0:1].astype(I32), 1.0, 0.0)
            + jnp.where(cols == pos[:, 1:2].astype(I32), 1.0, 0.0)).astype(BF16)
    y = jnp.dot(pick, ys_sc[slot].astype(BF16), preferred_element_type=F32)
    out = x_ref[...] + mod_ref[0, 2] * y
    if final:
        out = _rms(out) * fnw_ref[...]
    o_ref[...] = out


def _comb_call(lists, xn, pos, mod, fnw, yb, L, tm, final):
    T = xn.shape[0]
    nt = T // tm
    per_b = L // tm
    return pl.pallas_call(
        functools.partial(_comb_kernel, tm=tm, final=final, nt=nt),
        grid_spec=pltpu.PrefetchScalarGridSpec(
            num_scalar_prefetch=6, grid=(nt,),
            in_specs=[pl.BlockSpec((tm, D_MODEL), lambda i, *_: (i, 0)),
                      pl.BlockSpec((tm, LANES), lambda i, *_: (i, 0)),
                      pl.BlockSpec((1, 3, 1, D_MODEL), lambda i, *_: (i // per_b, 0, 0, 0)),
                      pl.BlockSpec((1, D_MODEL), lambda i, *_: (0, 0)),
                      pl.BlockSpec(memory_space=pl.ANY)],
            out_specs=pl.BlockSpec((tm, D_MODEL), lambda i, *_: (i, 0)),
            scratch_shapes=[pltpu.VMEM((2, _moe_cap(tm), D_MODEL), F32), pltpu.SemaphoreType.DMA((2,))]),
        out_shape=_sds((T, D_MODEL), F32),
        compiler_params=_cp(("arbitrary",)), name="moe_combine",
    )(*lists, xn, pos, mod, fnw, yb)


def _moe_cap(tm):
    return max(2 * tm + N_EXPERTS * MOE_ALIGN, MOE_ROWS)


def _dma_lists(groups, src, dst, cap):
    def expand(count, first, stride, width):
        ends = jnp.cumsum(count, axis=1)
        k = jnp.arange(width, dtype=I32)
        run = jnp.minimum(jnp.sum(ends[:, None, :] <= k[None, :, None], axis=2), N_EXPERTS - 1)
        sel = run[:, :, None] == jnp.arange(N_EXPERTS, dtype=I32)[None, None, :]
        pick = lambda a: jnp.sum(jnp.where(sel, a[:, None, :], 0), axis=2)
        off = pick(first) + (k[None, :] - pick(ends - count)) * stride
        return (pick(src) + off).reshape(-1).astype(I32), (pick(dst) + off).reshape(-1).astype(I32)

    big_rows = MOE_BIG * MOE_ALIGN
    n_big, n_small = groups // MOE_BIG, groups % MOE_BIG
    big_src, big_dst = expand(n_big, jnp.zeros_like(groups), big_rows, cap // big_rows)
    small_src, small_dst = expand(n_small, n_big * big_rows, MOE_ALIGN, N_EXPERTS * (MOE_BIG - 1))
    total = lambda a: jnp.sum(a, axis=1).astype(I32)
    return total(n_big), total(n_small), big_src, big_dst, small_src, small_dst


def _moe(xn, h2, route, mod, fnw, w1, w3, w2, layer, L, final):
    T = xn.shape[0]
    tm = min(MOE_TILE, L)
    nt = T // tm
    pos, post, counts = _rank_call(route, tm)
    cnt = counts[:, 0, :N_EXPERTS].astype(I32)
    cnt = (cnt + MOE_ALIGN - 1) // MOE_ALIGN * MOE_ALIGN
    total = jnp.sum(cnt, axis=0)
    padded = (total + MOE_ROWS - 1) // MOE_ROWS * MOE_ROWS
    pend = jnp.cumsum(padded)
    seg_dst = (pend - padded)[None, :] + jnp.cumsum(cnt, axis=0) - cnt
    seg_src = jnp.cumsum(cnt, axis=1) - cnt
    nb = -(-(2 * T + nt * N_EXPERTS * (MOE_ALIGN - 1) + N_EXPERTS * (MOE_ROWS - 1)) // MOE_ROWS)
    blk_e = jnp.minimum(jnp.sum(pend[None, :] <= (jnp.arange(nb, dtype=I32) * MOE_ROWS)[:, None], axis=1),
                        N_EXPERTS - 1).astype(I32)
    n_used = (pend[N_EXPERTS - 1:] // MOE_ROWS).astype(I32)
    last_blk = jnp.concatenate([jnp.where(padded > 0, pend // MOE_ROWS - 1, -1).astype(I32), n_used])
    lists = _dma_lists(cnt // MOE_ALIGN, seg_src, seg_dst, _moe_cap(tm))
    xg = _disp_call(lists, last_blk, h2, route, post, nb * MOE_ROWS, tm)
    yb = _expert_call(blk_e, n_used, xg, w1, w3, w2, layer)
    return _comb_call(lists, xn, pos, mod, fnw, yb, L, tm, final)


def _in1_kernel(x_ref, mod_ref, nw_ref, win_ref, qnw_ref, wuq_ref, kvnw_ref, tq_ref, tk_ref,
                q_ref, ckv_ref, kr_ref):
    h = _norm_mod(x_ref[...], nw_ref[...], mod_ref[0, 0], mod_ref[0, 1])
    zz = _dot_bf(h, win_ref[...])
    cqn = _rms(zz[:, :MLA_Q_RANK]) * qnw_ref[...]
    q = _dot_bf(cqn, wuq_ref[...])
    tq = tq_ref[...]
    for hh in range(MLA_H):
        q_ref[:, hh * LANES:(hh + 1) * LANES] = (q[:, hh * LANES:(hh + 1) * LANES] * tq).astype(BF16)
    c0 = MLA_Q_RANK
    ckv_ref[...] = _rms(zz[:, c0:c0 + MLA_KV_RANK]) * kvnw_ref[...]
    c1 = c0 + MLA_KV_RANK
    tk = tk_ref[...]
    kr_ref[...] = (zz[:, c1:c1 + MLA_ROPE] * tk[:, :MLA_ROPE]
                   + zz[:, c1 + LANES:c1 + LANES + MLA_ROPE] * tk[:, MLA_ROPE:])


def _in1_call(x, mod, nw, w_in_r, q_nw, w_uq_r, kv_nw, tab_q, tab_k, L, tm):
    T = x.shape[0]
    per_b = L // tm
    tok = lambda i: (i, 0)
    const = lambda i: (0, 0)
    pos = lambda i: (i % per_b, 0)
    return pl.pallas_call(
        _in1_kernel, grid=(T // tm,),
        in_specs=[pl.BlockSpec((tm, D_MODEL), tok),
                  pl.BlockSpec((1, 3, 1, D_MODEL), lambda i: (i // per_b, 0, 0, 0)),
                  pl.BlockSpec((1, D_MODEL), const),
                  pl.BlockSpec((D_MODEL, MLA_IN_R), const),
                  pl.BlockSpec((1, MLA_Q_RANK), const),
                  pl.BlockSpec((MLA_Q_RANK, MLA_H * LANES), const),
                  pl.BlockSpec((1, MLA_KV_RANK), const),
                  pl.BlockSpec((tm, LANES), pos),
                  pl.BlockSpec((tm, 2 * MLA_ROPE), pos)],
        out_specs=[pl.BlockSpec((tm, MLA_H * LANES), tok),
                   pl.BlockSpec((tm, MLA_KV_RANK), tok),
                   pl.BlockSpec((tm, MLA_ROPE), tok)],
        out_shape=[_sds((T, MLA_H * LANES), BF16), _sds((T, MLA_KV_RANK), F32),
                   _sds((T, MLA_ROPE), F32)],
        compiler_params=_cp(("arbitrary",)), name="in1",
    )(x, mod, nw, w_in_r, q_nw, w_uq_r, kv_nw, tab_q, tab_k)


def _kv_kernel(ckv_ref, kr_ref, wk_ref, pe_ref, wv_ref, one_ref, k_ref, v_ref):
    c = ckv_ref[...].astype(BF16)
    k_ref[...] = (jnp.dot(c, wk_ref[...], preferred_element_type=F32)
                  + jnp.dot(kr_ref[...].astype(BF16), pe_ref[...], preferred_element_type=F32)).astype(BF16)
    v_ref[...] = (jnp.dot(c, wv_ref[...], preferred_element_type=F32) + one_ref[...]).astype(BF16)


def _kv_call(ckv, kr, wk_r, place, wv_r, ones_row, tm):
    T = ckv.shape[0]
    tok = lambda i: (i, 0)
    const = lambda i: (0, 0)
    wide = MLA_H * LANES
    return pl.pallas_call(
        _kv_kernel, grid=(T // tm,),
        in_specs=[pl.BlockSpec((tm, MLA_KV_RANK), tok),
                  pl.BlockSpec((tm, MLA_ROPE), tok),
                  pl.BlockSpec((MLA_KV_RANK, wide), const),
                  pl.BlockSpec((MLA_ROPE, wide), const),
                  pl.BlockSpec((MLA_KV_RANK, wide), const),
                  pl.BlockSpec((1, wide), const)],
        out_specs=[pl.BlockSpec((tm, wide), tok), pl.BlockSpec((tm, wide), tok)],
        out_shape=[_sds((T, wide), BF16), _sds((T, wide), BF16)],
        compiler_params=_cp(("arbitrary",)), name="mla_kv",
    )(ckv, kr, wk_r, place, wv_r, ones_row)


def _attn_kernel(q_ref, k_ref, v_ref, o_ref, m0, m1, a0, a1, s0, s1, *, tq, tk, Lk, pos0):
    m_scs, acc_scs, s_scs = (m0, m1), (a0, a1), (s0, s1)
    q_lo = pos0 + pl.program_id(2) * tq
    k_hi = jnp.minimum(((q_lo + tq - 1) // CHUNK + 1) * CHUNK, Lk)
    n_blk = (k_hi + tk - 1) // tk
    n_full = jnp.minimum(((q_lo // CHUNK + 1) * CHUNK) // tk, n_blk)
    q_chunk = jnp.right_shift(q_lo + lax.broadcasted_iota(I32, (tq, tk), 0), CHUNK_SHIFT)
    k_iota = lax.broadcasted_iota(I32, (tq, tk), 1)

    def scores(hh, k0):
        hc = slice(hh * LANES, (hh + 1) * LANES)
        return lax.dot_general(q_ref[:, hc], k_ref[pl.ds(k0, tk), hc], NT, preferred_element_type=F32)

    for hh in range(2):
        m_scs[hh][...] = jnp.full((tq, LANES), NEG, F32)
        acc_scs[hh][...] = jnp.zeros((tq, LANES), F32)
        s_scs[hh][...] = scores(hh, 0)

    def step(j, masked, look_ahead):
        k0 = pl.multiple_of(j * tk, tk)
        if look_ahead:
            ahead = [scores(hh, pl.multiple_of((j + 1) * tk, tk)) for hh in range(2)]
        for hh in range(2):
            hc = slice(hh * LANES, (hh + 1) * LANES)
            s = s_scs[hh][...]
            if masked:
                s = jnp.where(q_chunk >= jnp.right_shift(k0 + k_iota, CHUNK_SHIFT), s, NEG)
            m_prev = m_scs[hh][...]
            m_new = jnp.maximum(m_prev, jnp.max(s, axis=-1, keepdims=True))
            if tk % LANES == 0:
                p = jnp.exp2(s - jnp.concatenate([m_new] * (tk // LANES), axis=1))
            else:
                p = jnp.exp2(s - m_new[:, 0:1])
            acc_scs[hh][...] = (jnp.exp2(m_prev - m_new) * acc_scs[hh][...]
                                + jnp.dot(p.astype(BF16), v_ref[pl.ds(k0, tk), hc],
                                          preferred_element_type=F32))
            m_scs[hh][...] = m_new
        if look_ahead:
            for hh in range(2):
                s_scs[hh][...] = ahead[hh]

    def loop(lo, hi, masked, unroll=1):
        def body(jj, carry):
            for u in range(unroll):
                step(lo + jj * unroll + u, masked, True)
            return carry
        lax.fori_loop(0, (hi - lo) // unroll, body, 0)
        return lo + (hi - lo) // unroll * unroll

    n_pre = jnp.minimum(n_full, n_blk - 1)
    done = loop(0, n_pre, False, unroll=2)
    loop(done, n_pre, False)
    loop(n_full, n_blk - 1, True)

    @pl.when(n_full < n_blk)
    def _():
        step(n_blk - 1, True, False)

    @pl.when(n_full == n_blk)
    def _():
        step(n_blk - 1, False, False)

    outs = []
    for hh in range(2):
        acc = acc_scs[hh][...]
        outs.append(acc[:, :MLA_V] / acc[:, MLA_V:MLA_V + 1])
    o_ref[...] = jnp.concatenate(outs, axis=1).astype(BF16)


def _attn_call(q, k, v, B, Lq, Lk, pos0, tq, tk):
    nq = Lq // tq
    return pl.pallas_call(
        functools.partial(_attn_kernel, tq=tq, tk=tk, Lk=Lk, pos0=pos0), grid=(B, MLA_H // 2, nq),
        in_specs=[pl.BlockSpec((tq, 2 * LANES), lambda b, hp, i: (b * nq + i, hp)),
                  pl.BlockSpec((Lk, 2 * LANES), lambda b, hp, i: (b, hp)),
                  pl.BlockSpec((Lk, 2 * LANES), lambda b, hp, i: (b, hp))],
        out_specs=pl.BlockSpec((tq, 2 * MLA_V), lambda b, hp, i: (b * nq + i, hp)),
        out_shape=_sds((B * Lq, MLA_H * MLA_V), BF16),
        scratch_shapes=[pltpu.VMEM((tq, LANES), F32)] * 4 + [pltpu.VMEM((tq, tk), F32)] * 2,
        compiler_params=_cp(("arbitrary", "arbitrary", "arbitrary")), name="mla_attn",
    )(q, k, v)


def _split_cols(w, widths):
    offs = [0]
    for n in widths:
        offs.append(offs[-1] + n)
    return [w[:, offs[i]:offs[i + 1]] for i in range(len(widths))]


def _hi_lo_pair(w):
    hi = w.astype(BF16)
    return jnp.stack([hi, (w - hi.astype(F32)).astype(BF16)], axis=-3)


def _prep_even(ab_w_in, gla_w_alpha, gdn_a_log, gdn_dt_bias):
    gq, gk, gv, glr, gr, qkv, ga, gb, gg = _split_cols(ab_w_in, AB_IN_WIDTHS)
    w_main = jnp.concatenate([gq, gk, gv, gr, qkv, gg], axis=1).astype(BF16)
    small = jnp.concatenate([glr, ga, gb], axis=1)
    w_small = _hi_lo_pair(jnp.pad(small, ((0, 0), (0, LANES - small.shape[1]))))
    wa_pad = jnp.pad(gla_w_alpha, ((0, LANES - GLA_LR), (0, 0)))
    alog_v = jnp.zeros((1, LANES), F32).at[0, LANE_GA:LANE_GA + GDN_H].set(gdn_a_log)
    dtb_v = jnp.zeros((1, LANES), F32).at[0, LANE_GA:LANE_GA + GDN_H].set(gdn_dt_bias)
    return w_main, w_small, wa_pad, alog_v, dtb_v


def _swap_halves(w):
    half = w.shape[-1] // 2
    return jnp.concatenate([w[..., half:], w[..., :half]], axis=-1)


def _prep_odd(mla_w_in, mla_w_uq, mla_w_ukv):
    cq, ckv, kr = _split_cols(mla_w_in, (MLA_Q_RANK, MLA_KV_RANK, MLA_ROPE))
    pad = lambda w: jnp.pad(w, ((0, 0), (0, LANES - w.shape[1])))
    w_in_r = jnp.concatenate([cq, ckv, pad(kr), pad(_swap_halves(kr))], axis=1).astype(BF16)
    uq = mla_w_uq.reshape(MLA_Q_RANK, MLA_H, MLA_NOPE + MLA_ROPE)
    uq_rope = uq[..., MLA_NOPE:]
    w_uq_r = jnp.concatenate([uq, _swap_halves(uq_rope)], axis=-1).reshape(
        MLA_Q_RANK, MLA_H * LANES).astype(BF16)
    ukv = mla_w_ukv.reshape(MLA_KV_RANK, MLA_H, MLA_NOPE + MLA_V)
    wk_r = jnp.pad(ukv[..., :MLA_NOPE], ((0, 0), (0, 0), (0, LANES - MLA_NOPE))).reshape(
        MLA_KV_RANK, MLA_H * LANES).astype(BF16)
    wv_r = jnp.pad(ukv[..., MLA_NOPE:], ((0, 0), (0, 0), (0, LANES - MLA_V))).reshape(
        MLA_KV_RANK, MLA_H * LANES).astype(BF16)
    ones_row = jnp.tile((jnp.arange(LANES) == MLA_V).astype(F32), MLA_H)[None, :]
    eye = jnp.eye(MLA_ROPE, dtype=F32)
    place = jnp.concatenate([jnp.zeros((MLA_ROPE, MLA_NOPE), F32), eye, eye], axis=1)
    place = jnp.tile(place, (1, MLA_H)).astype(BF16)
    return w_in_r, w_uq_r, wk_r, wv_r, place, ones_row


def _rope_tables(pos0, L):
    half = MLA_ROPE // 2
    inv = jnp.exp(-math.log(ROPE_THETA) * jnp.arange(half, dtype=F32) / half)
    ang = (pos0 + jnp.arange(L, dtype=I32)).astype(F32)[:, None] * inv[None, :]
    cos, sin = jnp.cos(ang), jnp.sin(ang)
    tab_k = jnp.concatenate([cos, cos, -sin, sin], axis=1)
    scale = (MLA_NOPE + MLA_ROPE) ** -0.5 * math.log2(math.e)
    tab_q = jnp.concatenate([jnp.ones((L, MLA_NOPE), F32), tab_k], axis=1) * scale
    return tab_q, tab_k


def _mod4(mods, lo, hi):
    return [mods[i, lo:hi].reshape(hi - lo, 3, 1, D_MODEL) for i in range(mods.shape[0])]


def _trunk(x3, mods, pos0, gla_s, gdn_s, conv_s, past_ckv, past_kr, p):
    B, L, _ = x3.shape
    T = B * L
    tm = min(512, L)
    x = x3.reshape(T, D_MODEL)
    row = lambda a: a.reshape(1, -1)

    w_main, w_small, wa_pad, alog_v, dtb_v = p['even']
    zm, zs = _in0_call(x, mods[0], row(p['norm_w'][0, 0]), w_main, w_small, L, tm)
    o_gla, gla_t = _gla_call(zm, zs, wa_pad, row(p['gla_b_alpha'][0]), row(p['gla_norm_w'][0]),
                             jnp.swapaxes(gla_s, -1, -2), B, L)
    hist8 = jnp.pad(conv_s, ((0, 0), (8 - (CONV_W - 1), 0), (0, 0)))
    o_gdn, gdn_new = _gdn_call(zm, zs, p['gdn_conv_w'][0], alog_v, dtb_v, row(p['gdn_norm_w'][0]),
                               hist8, gdn_s, B, L)
    qkv0 = 3 * GLA_H * GLA_DV
    conv_new = zm.reshape(B, L, AB_MAIN)[:, L - (CONV_W - 1):, qkv0:qkv0 + GDN_CONV_DIM].astype(F32)
    w_out = p['ab_w_out_bf']
    half = GLA_H * GLA_DV
    xn, h2, route = _post_call([o_gla, o_gdn], [w_out[:half], w_out[half:]], x, mods[0], mods[1],
                               row(p['norm_w'][0, 1]), p['w_route'][0], p['b_route'][0], L, tm)
    x = _moe(xn, h2, route, mods[1], row(p['final_norm_w']), p['moe_w1'], p['moe_w3'], p['moe_w2'],
             0, L, final=False)

    w_in_r, w_uq_r, wk_r, wv_r, place, ones_row = p['odd']
    tab_q, tab_k = _rope_tables(pos0, L)
    q, ckv, kr = _in1_call(x, mods[2], row(p['norm_w'][1, 0]), w_in_r, row(p['mla_q_norm_w'][0]),
                           w_uq_r, row(p['mla_kv_norm_w'][0]), tab_q, tab_k, L, tm)
    if past_ckv is None:
        ckv_all, kr_all, Lk = ckv, kr, L
    else:
        Lk = past_ckv.shape[1] + L
        ckv_all = jnp.concatenate([past_ckv, ckv.reshape(B, L, -1)], axis=1).reshape(B * Lk, -1)
        kr_all = jnp.concatenate([past_kr, kr.reshape(B, L, -1)], axis=1).reshape(B * Lk, -1)
    tkv = 512 if (B * Lk) % 512 == 0 else Lk
    k_all, v_all = _kv_call(ckv_all, kr_all, wk_r, place, wv_r, ones_row, tkv)
    tq = min(ATTN_TILE, L)
    tk = ATTN_TILE if Lk % ATTN_TILE == 0 else Lk
    att = _attn_call(q, k_all, v_all, B, L, Lk, pos0, tq, tk)
    xn, h2, route = _post_call([att], [p['mla_w_out_bf']], x, mods[2], mods[3],
                               row(p['norm_w'][1, 1]), p['w_route'][1], p['b_route'][1], L, tm)
    y = _moe(xn, h2, route, mods[3], row(p['final_norm_w']), p['moe_w1'], p['moe_w3'], p['moe_w2'],
             1, L, final=True)
    return (y.reshape(B, L, D_MODEL), jnp.swapaxes(gla_t, -1, -2)[None], gdn_new[None], conv_new[None],
            ckv.reshape(1, B, L, MLA_KV_RANK), kr.reshape(1, B, L, MLA_ROPE))


def kernel(x_prompt, x_sample, state_gla, state_gdn, state_gdn_conv, cache_mla_ckv, cache_mla_krope,
           c_prompt, c_sample, ada_w, ada_b, norm_w, final_norm_w, ab_w_in, gla_w_alpha, gla_b_alpha,
           gla_norm_w, gdn_conv_w, gdn_a_log, gdn_dt_bias, gdn_norm_w, ab_w_out, mla_w_in, mla_q_norm_w,
           mla_w_uq, mla_kv_norm_w, mla_w_ukv, mla_w_out, moe_w_group, moe_b_group, moe_w_expert,
           moe_b_expert, moe_w1, moe_w3, moe_w2):
    depth = ada_w.shape[0]
    bp, bs = x_prompt.shape[0], x_sample.shape[0]
    mods = _ada_call(jnp.concatenate([c_prompt, c_sample], axis=0),
                     ada_w.reshape(2 * depth, D_MODEL, 3 * D_MODEL), ada_b.reshape(2 * depth, 1, 3 * D_MODEL))
    w_route = _hi_lo_pair(jnp.pad(jnp.concatenate([moe_w_group, moe_w_expert], axis=-1),
                                  ((0, 0), (0, 0), (0, LANES - N_GROUPS - N_EXPERTS))))
    b_route = jnp.pad(jnp.concatenate([moe_b_group, moe_b_expert], axis=-1),
                      ((0, 0), (0, LANES - N_GROUPS - N_EXPERTS)))[:, None, :]
    p = dict(norm_w=norm_w, final_norm_w=final_norm_w, gla_b_alpha=gla_b_alpha, gla_norm_w=gla_norm_w,
             gdn_conv_w=gdn_conv_w, gdn_norm_w=gdn_norm_w, mla_q_norm_w=mla_q_norm_w,
             mla_kv_norm_w=mla_kv_norm_w, moe_w1=moe_w1, moe_w3=moe_w3, moe_w2=moe_w2,
             w_route=w_route, b_route=b_route,
             even=_prep_even(ab_w_in[0], gla_w_alpha[0], gdn_a_log[0], gdn_dt_bias[0]),
             odd=_prep_odd(mla_w_in[0], mla_w_uq[0], mla_w_ukv[0]),
             ab_w_out_bf=ab_w_out[0].astype(BF16), mla_w_out_bf=mla_w_out[0].astype(BF16))
    y_p, gla_p, gdn_p, conv_p, ckv_p, kr_p = _trunk(
        x_prompt, _mod4(mods, 0, bp), 0,
        jnp.zeros((bp, GLA_H, GLA_DK, GLA_DV), F32), jnp.zeros((bp, GDN_H, GDN_DK, GDN_DV), F32),
        jnp.zeros((bp, CONV_W - 1, GDN_CONV_DIM), F32), None, None, p)
    y_s, gla_s, gdn_s, conv_s, ckv_s, kr_s = _trunk(
        x_sample, _mod4(mods, bp, bp + bs), cache_mla_ckv.shape[2],
        state_gla[0], state_gdn[0], state_gdn_conv[0], cache_mla_ckv[0], cache_mla_krope[0], p)
    return (y_p, y_s, gla_p, gdn_p, conv_p, ckv_p, kr_p, gla_s, gdn_s, conv_s, ckv_s, kr_s)
```

```python
import functools
import math

import jax
import jax.numpy as jnp
from jax import lax
from jax.experimental import pallas as pl
from jax.experimental.pallas import tpu as pltpu

F32 = jnp.float32
BF16 = jnp.bfloat16
I32 = jnp.int32
HI = lax.Precision.HIGHEST

D_MODEL = 1024
CHUNK = 64
GLA_H, GLA_DK, GLA_DV, GLA_LR, GLA_TAU = 4, 64, 128, 16, 16.0
GDN_H, GDN_DK, GDN_DV, CONV_W = 4, 128, 128, 4
GDN_CONV_DIM = GDN_H * (2 * GDN_DK + GDN_DV)
AB_IN_WIDTHS = (GLA_H * GLA_DK, GLA_H * GLA_DK, GLA_H * GLA_DV, GLA_LR, GLA_H * GLA_DV,
                GDN_CONV_DIM, GDN_H, GDN_H, GDN_H * GDN_DV)
AB_MAIN = 3584
MLA_H, MLA_NOPE, MLA_ROPE, MLA_V = 16, 64, 32, 64
MLA_Q_RANK, MLA_KV_RANK = 384, 256
MLA_IN_R = MLA_Q_RANK + MLA_KV_RANK + 256
ROPE_THETA = 10000.0
N_GROUPS, EXPERTS_PER_GROUP, N_EXPERTS, D_EXPERT = 4, 8, 32, 512
GROUP_SHIFT = int(math.log2(EXPERTS_PER_GROUP))
CHUNK_SHIFT = int(math.log2(CHUNK))
EPS = 1e-6

LANES = 128
GLA_SUB = 16
GLA_BATCH = 2
SCAN_CHUNKS = 4
GLA_SAFE_LOG_DECAY = -60.0
MOE_ROWS = 256
MOE_TILE = 512
MOE_ALIGN = 8
MOE_BIG = 4
MOE_XCOLS = D_MODEL + LANES
ATTN_TILE = 512
NEG = -3.0e38
VMEM_LIMIT = 56 * 1024 * 1024

LANE_GA = GLA_LR
LANE_GB = GLA_LR + GDN_H

NT = (((1,), (1,)), ((), ()))
TN = (((0,), (0,)), ((), ()))


def _cp(sem, vmem=None):
    return pltpu.CompilerParams(dimension_semantics=sem, vmem_limit_bytes=vmem or VMEM_LIMIT)


def _sds(shape, dtype):
    return jax.ShapeDtypeStruct(shape, dtype)


def _sigmoid(x):
    return 1.0 / (1.0 + jnp.exp(-x))


def _softplus(x):
    return jnp.maximum(x, 0.0) + jnp.log(1.0 + jnp.exp(-jnp.abs(x)))


def _rms(x):
    return x * lax.rsqrt(jnp.mean(x * x, axis=-1, keepdims=True) + EPS)


def _norm_mod(x, nw, shift, scale):
    return (_rms(x) * nw) * (1.0 + scale) + shift


def _dot_bf(a, b):
    return jnp.dot(a.astype(BF16), b.astype(BF16), preferred_element_type=F32)


def _dot_hi(a, b):
    return jnp.dot(a, b, precision=HI, preferred_element_type=F32)


def _ada_kernel(c_ref, w_ref, b_ref, o_ref):
    c = c_ref[...]
    o_ref[0] = _dot_hi(c * _sigmoid(c), w_ref[0]) + b_ref[0]


def _ada_call(c_all, ada_w, ada_b):
    n = ada_w.shape[0]
    nb = c_all.shape[0]
    return pl.pallas_call(
        _ada_kernel, grid=(n, 3),
        in_specs=[pl.BlockSpec((nb, D_MODEL), lambda i, j: (0, 0)),
                  pl.BlockSpec((1, D_MODEL, D_MODEL), lambda i, j: (i, 0, j)),
                  pl.BlockSpec((1, 1, D_MODEL), lambda i, j: (i, 0, j))],
        out_specs=pl.BlockSpec((1, nb, D_MODEL), lambda i, j: (i, 0, j)),
        out_shape=_sds((n, nb, 3 * D_MODEL), F32),
        compiler_params=_cp(("arbitrary", "arbitrary")), name="ada",
    )(c_all, ada_w, ada_b)


def _in0_kernel(x_ref, mod_ref, nw_ref, wm_ref, ws_ref, zm_ref, zs_ref):
    h = _norm_mod(x_ref[...], nw_ref[...], mod_ref[0, 0], mod_ref[0, 1])
    h_hl = _split_bf(h)
    zm_ref[...] = jnp.dot(h_hl[0], wm_ref[...], preferred_element_type=F32).astype(BF16)
    zs_ref[...] = _dot3(h_hl, (ws_ref[0], ws_ref[1]))


def _in0_call(x, mod, nw, w_main, w_small, L, tm):
    T = x.shape[0]
    per_b = L // tm
    return pl.pallas_call(
        _in0_kernel, grid=(T // tm,),
        in_specs=[pl.BlockSpec((tm, D_MODEL), lambda i: (i, 0)),
                  pl.BlockSpec((1, 3, 1, D_MODEL), lambda i: (i // per_b, 0, 0, 0)),
                  pl.BlockSpec((1, D_MODEL), lambda i: (0, 0)),
                  pl.BlockSpec((D_MODEL, AB_MAIN), lambda i: (0, 0)),
                  pl.BlockSpec((2, D_MODEL, LANES), lambda i: (0, 0, 0))],
        out_specs=[pl.BlockSpec((tm, AB_MAIN), lambda i: (i, 0)),
                   pl.BlockSpec((tm, LANES), lambda i: (i, 0))],
        out_shape=[_sds((T, AB_MAIN), BF16), _sds((T, LANES), F32)],
        compiler_params=_cp(("arbitrary",)), name="in0",
    )(x, mod, nw, w_main, w_small)


def _gated_head_out(o, nw, gate):
    gate = gate.astype(F32)
    return (_rms(o) * nw * (gate * _sigmoid(gate))).astype(BF16)


def _gla_chunk_exact(q, k, v, b, st):
    CL = q.shape[0]
    row = lax.broadcasted_iota(I32, (GLA_SUB, 1), 0)
    bprev = jnp.zeros((1, GLA_DK), F32)
    outs = []
    for blk in range(CL // GLA_SUB):
        sl = slice(blk * GLA_SUB, (blk + 1) * GLA_SUB)
        qi, ki, vi = q[sl], k[sl], v[sl]
        brel = b[sl] - bprev
        bend = brel[GLA_SUB - 1:GLA_SUB]
        oi = lax.dot_general((qi * jnp.exp(brel)).astype(BF16), st.astype(BF16), NT,
                             preferred_element_type=F32)
        for j in range(GLA_SUB):
            d = brel - brel[j:j + 1]
            e = jnp.where(row >= j, jnp.exp(jnp.minimum(d, 0.0)), 0.0)
            a = jnp.sum(qi * ki[j:j + 1] * e, axis=-1, keepdims=True)
            oi = oi + a * vi[j:j + 1]
        outs.append(oi)
        khat = ki * jnp.exp(bend - brel)
        st = st * jnp.exp(bend) + lax.dot_general(vi.astype(BF16), khat.astype(BF16), TN,
                                                  preferred_element_type=F32)
        bprev = b[(blk + 1) * GLA_SUB - 1:(blk + 1) * GLA_SUB]
    return jnp.concatenate(outs, axis=0), st


def _gla_chunk_fast(q, k, v, b, st, causal):
    CL = q.shape[0]
    qh = (q * jnp.exp(b)).astype(BF16)
    kh = (k * jnp.exp(-b)).astype(BF16)
    vb = v.astype(BF16)
    att = jnp.where(causal, lax.dot_general(qh, kh, NT, preferred_element_type=F32), 0.0)
    o = (lax.dot_general(qh, st.astype(BF16), NT, preferred_element_type=F32)
         + jnp.dot(att.astype(BF16), vb, preferred_element_type=F32))
    blast = b[CL - 1:CL]
    kdec = (k * jnp.exp(blast - b)).astype(BF16)
    st = st * jnp.exp(blast) + lax.dot_general(vb, kdec, TN, preferred_element_type=F32)
    return o, st


def _gla_kernel(q_ref, k_ref, v_ref, gr_ref, zs_ref, wa_ref, ba_ref, nw_ref, s0_ref,
                o_ref, st_ref, st_sc, b_sc, *, CL, nch, nsteps, nb):
    step = pl.program_id(1)

    @pl.when(step == 0)
    def _():
        st_sc[...] = s0_ref[...]

    r = lax.broadcasted_iota(I32, (CL, CL), 0)
    cc = lax.broadcasted_iota(I32, (CL, CL), 1)
    causal = r >= cc
    tri = causal.astype(F32)
    b_min = None
    for g in range(nb):
        for ch in range(nch):
            rows = slice(ch * CL, (ch + 1) * CL)
            pre = _dot_hi(zs_ref[g, rows, :], wa_ref[...]) + ba_ref[...]
            la = -_softplus(-pre) * (1.0 / GLA_TAU)
            b_ch = _dot_hi(tri, la)
            b_sc[g, rows, :] = b_ch
            lo = jnp.min(b_ch[CL - 1:CL, :])
            b_min = lo if b_min is None else jnp.minimum(b_min, lo)
    safe = b_min > GLA_SAFE_LOG_DECAY

    def run(chunk_fn):
        nw = nw_ref[...]
        units = [(g, h) for g in range(nb) for h in range(GLA_H)]
        st = {un: st_sc[un[0], un[1]] for un in units}
        for ch in range(nch):
            rows = slice(ch * CL, (ch + 1) * CL)
            for g, h in units:
                kc = slice(h * GLA_DK, (h + 1) * GLA_DK)
                vc = slice(h * GLA_DV, (h + 1) * GLA_DV)
                q = q_ref[g, rows, kc].astype(F32) * GLA_DK ** -0.5
                o, st[g, h] = chunk_fn(q, k_ref[g, rows, kc].astype(F32), v_ref[g, rows, vc].astype(F32),
                                       b_sc[g, rows, kc], st[g, h])
                o_ref[g, rows, vc] = _gated_head_out(o, nw, gr_ref[g, rows, vc])
        for g, h in units:
            st_sc[g, h] = st[g, h]

    @pl.when(safe)
    def _():
        run(functools.partial(_gla_chunk_fast, causal=causal))

    @pl.when(jnp.logical_not(safe))
    def _():
        run(_gla_chunk_exact)

    @pl.when(step == nsteps - 1)
    def _():
        st_ref[...] = st_sc[...]


def _gla_call(zm, zs, wa_pad, b_alpha, nw, s0t, B, L):
    CL = min(CHUNK, L)
    nch = SCAN_CHUNKS if L % (CL * SCAN_CHUNKS) == 0 else 1
    nb = GLA_BATCH if B % GLA_BATCH == 0 else 1
    rows = CL * nch
    nsteps = L // rows
    qk_w = GLA_H * GLA_DK
    v_w = GLA_H * GLA_DV
    zm3 = zm.reshape(B, L, AB_MAIN)
    zs3 = zs.reshape(B, L, LANES)
    tok = lambda col: (lambda b, c: (b, c, col))
    const = lambda b, c: (0, 0)
    state = lambda b, c: (b, 0, 0, 0)
    o, st = pl.pallas_call(
        functools.partial(_gla_kernel, CL=CL, nch=nch, nsteps=nsteps, nb=nb), grid=(B // nb, nsteps),
        in_specs=[pl.BlockSpec((nb, rows, qk_w), tok(0)),
                  pl.BlockSpec((nb, rows, qk_w), tok(1)),
                  pl.BlockSpec((nb, rows, v_w), tok(1)),
                  pl.BlockSpec((nb, rows, v_w), tok(2)),
                  pl.BlockSpec((nb, rows, LANES), tok(0)),
                  pl.BlockSpec((LANES, qk_w), const),
                  pl.BlockSpec((1, qk_w), const),
                  pl.BlockSpec((1, GLA_DV), const),
                  pl.BlockSpec((nb, GLA_H, GLA_DV, GLA_DK), state)],
        out_specs=[pl.BlockSpec((nb, rows, v_w), tok(0)),
                   pl.BlockSpec((nb, GLA_H, GLA_DV, GLA_DK), state)],
        out_shape=[_sds((B, L, v_w), BF16), _sds((B, GLA_H, GLA_DV, GLA_DK), F32)],
        scratch_shapes=[pltpu.VMEM((nb, GLA_H, GLA_DV, GLA_DK), F32), pltpu.VMEM((nb, rows, qk_w), F32)],
        compiler_params=_cp(("arbitrary", "arbitrary")), name="gla",
    )(zm3, zm3, zm3, zm3, zs3, wa_pad, b_alpha, nw, s0t)
    return o.reshape(B * L, v_w), st


def _split_bf(a):
    hi = a.astype(BF16)
    return hi, (a - hi.astype(F32)).astype(BF16)


def _dot3(a_hl, b_hl):
    (ah, al), (bh, bl) = a_hl, b_hl
    d = functools.partial(jnp.dot, preferred_element_type=F32)
    return d(ah, bh) + d(ah, bl) + d(al, bh)


def _gdn_kernel(qkv_ref, zs_ref, gg_ref, cw_ref, alog_ref, dtb_ref, nw_ref, hist_ref, s0_ref,
                o_ref, s_ref, cv_sc, st_sc, *, CL, nch, nsteps):
    step = pl.program_id(1)
    R = CL * nch

    @pl.when(step == 0)
    def _():
        cv_sc[0:8, :] = hist_ref[0]
        st_sc[...] = s0_ref[0]

    x = qkv_ref[...].astype(F32)
    cv_sc[8:8 + R, :] = x
    cw = cw_ref[...]
    conv = (cv_sc[5:5 + R, :] * cw[0:1] + cv_sc[6:6 + R, :] * cw[1:2]
            + cv_sc[7:7 + R, :] * cw[2:3] + x * cw[3:4])
    cv_sc[0:8, :] = x[R - 8:R]
    conv = conv * _sigmoid(conv)

    zs = zs_ref[...]
    g_all = -jnp.exp(alog_ref[...]) * _softplus(zs + dtb_ref[...])
    beta_all = _sigmoid(zs)
    r = lax.broadcasted_iota(I32, (CL, CL), 0)
    cc = lax.broadcasted_iota(I32, (CL, CL), 1)
    tri = (r >= cc).astype(F32)
    eye = (r == cc).astype(F32)
    nw = nw_ref[...]
    qk_w = GDN_H * GDN_DK

    units = [(ch, h) for ch in range(nch) for h in range(GDN_H)]
    stage = {}
    for ch in range(nch):
        rows = slice(ch * CL, (ch + 1) * CL)
        gam = _dot_hi(tri, g_all[rows])
        gam_t = gam.T
        for h in range(GDN_H):
            q = conv[rows, h * GDN_DK:(h + 1) * GDN_DK]
            k = conv[rows, qk_w + h * GDN_DK:qk_w + (h + 1) * GDN_DK]
            v = conv[rows, 2 * qk_w + h * GDN_DV:2 * qk_w + (h + 1) * GDN_DV]
            q = q * lax.rsqrt(jnp.sum(q * q, axis=-1, keepdims=True) + EPS) * GDN_DK ** -0.5
            k = k * lax.rsqrt(jnp.sum(k * k, axis=-1, keepdims=True) + EPS)
            gcol = gam[:, LANE_GA + h:LANE_GA + h + 1]
            grow = gam_t[LANE_GA + h:LANE_GA + h + 1, :]
            bcol = beta_all[rows, LANE_GB + h:LANE_GB + h + 1]
            dec = jnp.where(r >= cc, jnp.exp(jnp.minimum(gcol - grow, 0.0)), 0.0)
            eg = jnp.exp(gcol)
            glast = gcol[CL - 1:CL]
            stage[ch, h] = dict(
                kb=k.astype(BF16), qb=q.astype(BF16), dec=dec, bcol=bcol,
                ub=(v * bcol).astype(BF16), wb=(k * (bcol * eg)).astype(BF16),
                qe=(q * eg).astype(BF16), kdec=(k * jnp.exp(glast - gcol)).astype(BF16),
                elast=jnp.exp(glast))
    for un in units:
        d = stage[un]
        kk = lax.dot_general(d['kb'], d['kb'], NT, preferred_element_type=F32)
        d['pw'] = jnp.where(r > cc, d['bcol'] * kk * d['dec'], 0.0)
        d['t'] = eye - d['pw']
    for _ in range(int(math.log2(CL)) - 1):
        for un in units:
            d = stage[un]
            hl = _split_bf(d['pw'])
            d['pw'] = _dot3(hl, hl)
        for un in units:
            d = stage[un]
            d['t'] = d['t'] + _dot3(_split_bf(d['t']), _split_bf(d['pw']))
    pre = {}
    for un in units:
        d = stage[un]
        tb = d['t'].astype(BF16)
        u = jnp.dot(tb, d['ub'], preferred_element_type=F32)
        w = jnp.dot(tb, d['wb'], preferred_element_type=F32).astype(BF16)
        qk = (lax.dot_general(d['qb'], d['kb'], NT, preferred_element_type=F32) * d['dec']).astype(BF16)
        pre[un] = (u, w, qk, d['qe'], d['kdec'], d['elast'])

    for h in range(GDN_H):
        vc = slice(h * GDN_DV, (h + 1) * GDN_DV)
        st = st_sc[h]
        for ch in range(nch):
            rows = slice(ch * CL, (ch + 1) * CL)
            u, w, qk, qe, kdec, elast = pre[ch, h]
            stb = st.astype(BF16)
            delta = u - jnp.dot(w, stb, preferred_element_type=F32)
            db = delta.astype(BF16)
            o = (jnp.dot(qe, stb, preferred_element_type=F32)
                 + jnp.dot(qk, db, preferred_element_type=F32))
            st = st * elast + lax.dot_general(kdec, db, TN, preferred_element_type=F32)
            o_ref[rows, vc] = _gated_head_out(o, nw, gg_ref[rows, vc])
        st_sc[h] = st

    @pl.when(step == nsteps - 1)
    def _():
        s_ref[0] = st_sc[...]


def _gdn_call(zm, zs, conv_w, alog_v, dtb_v, nw, hist8, s0, B, L):
    CL = min(CHUNK, L)
    nch = SCAN_CHUNKS if L % (CL * SCAN_CHUNKS) == 0 else 1
    rows = CL * nch
    nsteps = L // rows
    T = B * L
    v_w = GDN_H * GDN_DV
    row = lambda b, c: b * nsteps + c
    return pl.pallas_call(
        functools.partial(_gdn_kernel, CL=CL, nch=nch, nsteps=nsteps), grid=(B, nsteps),
        in_specs=[pl.BlockSpec((rows, GDN_CONV_DIM), lambda b, c: (row(b, c), 1)),
                  pl.BlockSpec((rows, LANES), lambda b, c: (row(b, c), 0)),
                  pl.BlockSpec((rows, v_w), lambda b, c: (row(b, c), 6)),
                  pl.BlockSpec((CONV_W, GDN_CONV_DIM), lambda b, c: (0, 0)),
                  pl.BlockSpec((1, LANES), lambda b, c: (0, 0)),
                  pl.BlockSpec((1, LANES), lambda b, c: (0, 0)),
                  pl.BlockSpec((1, GDN_DV), lambda b, c: (0, 0)),
                  pl.BlockSpec((1, 8, GDN_CONV_DIM), lambda b, c: (b, 0, 0)),
                  pl.BlockSpec((1, GDN_H, GDN_DK, GDN_DV), lambda b, c: (b, 0, 0, 0))],
        out_specs=[pl.BlockSpec((rows, v_w), lambda b, c: (row(b, c), 0)),
                   pl.BlockSpec((1, GDN_H, GDN_DK, GDN_DV), lambda b, c: (b, 0, 0, 0))],
        out_shape=[_sds((T, v_w), BF16), _sds((B, GDN_H, GDN_DK, GDN_DV), F32)],
        scratch_shapes=[pltpu.VMEM((8 + rows, GDN_CONV_DIM), F32),
                        pltpu.VMEM((GDN_H, GDN_DK, GDN_DV), F32)],
        compiler_params=_cp(("arbitrary", "arbitrary")), name="gdn",
    )(zm, zs, zm, conv_w, alog_v, dtb_v, nw, hist8, s0)


def _post_kernel(*refs, n_in):
    a_refs = refs[:n_in]
    w_refs = refs[n_in:2 * n_in]
    x_ref, mod1_ref, mod2_ref, nw2_ref, wr_ref, br_ref, xn_ref, h2_ref, rt_ref = refs[2 * n_in:]
    acc = jnp.dot(a_refs[0][...], w_refs[0][...], preferred_element_type=F32)
    for a_ref, w_ref in zip(a_refs[1:], w_refs[1:]):
        acc = acc + jnp.dot(a_ref[...], w_ref[...], preferred_element_type=F32)
    xn = x_ref[...] + mod1_ref[0, 2] * acc
    xn_ref[...] = xn
    h2 = _norm_mod(xn, nw2_ref[...], mod2_ref[0, 0], mod2_ref[0, 1])
    h2_ref[...] = h2
    logits = _dot3(_split_bf(h2), (wr_ref[0], wr_ref[1])) + br_ref[...]
    lane = lax.broadcasted_iota(I32, logits.shape, 1)
    is_g = lane < N_GROUPS
    gl = jnp.where(is_g, logits, NEG)
    gmax = jnp.max(gl, axis=-1, keepdims=True)
    g_sel = jnp.min(jnp.where(gl == gmax, lane, LANES), axis=-1, keepdims=True)
    p_g = 1.0 / jnp.sum(jnp.where(is_g, jnp.exp(logits - gmax), 0.0), axis=-1, keepdims=True)
    in_grp = jnp.logical_and(
        jnp.logical_and(lane >= N_GROUPS, lane < N_GROUPS + N_EXPERTS),
        jnp.right_shift(lane - N_GROUPS, GROUP_SHIFT) == g_sel)
    el = jnp.where(in_grp, logits, NEG)
    v1 = jnp.max(el, axis=-1, keepdims=True)
    i1 = jnp.min(jnp.where(jnp.logical_and(in_grp, el == v1), lane, LANES), axis=-1, keepdims=True)
    rest = jnp.logical_and(in_grp, lane != i1)
    el2 = jnp.where(rest, logits, NEG)
    v2 = jnp.max(el2, axis=-1, keepdims=True)
    i2 = jnp.min(jnp.where(jnp.logical_and(rest, el2 == v2), lane, LANES), axis=-1, keepdims=True)
    ex = jnp.exp(v2 - v1)
    w1 = 1.0 / (1.0 + ex)
    w2 = ex * w1
    e1 = (i1 - N_GROUPS).astype(F32)
    e2 = (i2 - N_GROUPS).astype(F32)
    rt_ref[...] = jnp.where(lane == 0, e1, jnp.where(lane == 1, e2, jnp.where(
        lane == 2, p_g * w1, jnp.where(lane == 3, p_g * w2, 0.0))))


def _post_call(a_list, w_list, x, mod1, mod2, nw2, w_route, b_route, L, tm):
    T = x.shape[0]
    per_b = L // tm
    n_in = len(a_list)
    tok = lambda i: (i, 0)
    const = lambda i: (0, 0)
    modmap = lambda i: (i // per_b, 0, 0, 0)
    in_specs = ([pl.BlockSpec((tm, a.shape[1]), tok) for a in a_list]
                + [pl.BlockSpec(w.shape, const) for w in w_list]
                + [pl.BlockSpec((tm, D_MODEL), tok),
                   pl.BlockSpec((1, 3, 1, D_MODEL), modmap),
                   pl.BlockSpec((1, 3, 1, D_MODEL), modmap),
                   pl.BlockSpec((1, D_MODEL), const),
                   pl.BlockSpec((2, D_MODEL, LANES), lambda i: (0, 0, 0)),
                   pl.BlockSpec((1, LANES), const)])
    return pl.pallas_call(
        functools.partial(_post_kernel, n_in=n_in), grid=(T // tm,),
        in_specs=in_specs,
        out_specs=[pl.BlockSpec((tm, D_MODEL), tok), pl.BlockSpec((tm, D_MODEL), tok),
                   pl.BlockSpec((tm, LANES), tok)],
        out_shape=[_sds((T, D_MODEL), F32), _sds((T, D_MODEL), F32), _sds((T, LANES), F32)],
        compiler_params=_cp(("arbitrary",)), name="post",
    )(*a_list, *w_list, x, mod1, mod2, nw2, w_route, b_route)


def _rank_kernel(rt_ref, pos_ref, post_ref, cnt_ref):
    rt = rt_ref[...]
    tm = rt.shape[0]
    lane = lax.broadcasted_iota(I32, rt.shape, 1)
    o1 = lane == rt[:, 0:1].astype(I32)
    o2 = lane == rt[:, 1:2].astype(I32)
    onehot = jnp.where(o1, 1.0, 0.0) + jnp.where(o2, 1.0, 0.0)
    r = lax.broadcasted_iota(I32, (tm, tm), 0)
    cc = lax.broadcasted_iota(I32, (tm, tm), 1)
    before = jnp.dot(jnp.where(r > cc, 1.0, 0.0).astype(BF16), onehot.astype(BF16),
                     preferred_element_type=F32)
    cnt = jnp.sum(onehot, axis=0, keepdims=True)
    er = lax.broadcasted_iota(I32, (LANES, LANES), 0)
    ec = lax.broadcasted_iota(I32, (LANES, LANES), 1)
    cnt8 = jnp.floor((cnt + (MOE_ALIGN - 1)) * (1.0 / MOE_ALIGN)) * MOE_ALIGN
    start = _dot_hi(jnp.broadcast_to(cnt8, (8, LANES)), jnp.where(er < ec, 1.0, 0.0))[0:1]
    where = before + start
    p1 = jnp.sum(jnp.where(o1, where, 0.0), axis=-1, keepdims=True)
    p2 = jnp.sum(jnp.where(o2, where, 0.0), axis=-1, keepdims=True)
    pos = jnp.where(lane == 0, p1, jnp.where(lane == 1, p2, 0.0))
    pos_ref[...] = pos
    post_ref[0] = pos.T[0:8, :].astype(I32)
    cnt_ref[0] = cnt


def _rank_call(route, tm):
    T = route.shape[0]
    nt = T // tm
    return pl.pallas_call(
        _rank_kernel, grid=(nt,),
        in_specs=[pl.BlockSpec((tm, LANES), lambda i: (i, 0))],
        out_specs=[pl.BlockSpec((tm, LANES), lambda i: (i, 0)),
                   pl.BlockSpec((1, 8, tm), lambda i: (i, 0, 0)),
                   pl.BlockSpec((1, 1, LANES), lambda i: (i, 0, 0))],
        out_shape=[_sds((T, LANES), F32), _sds((nt, 8, tm), I32), _sds((nt, 1, LANES), F32)],
        compiler_params=_cp(("arbitrary",)), name="moe_rank",
    )(route)


def _start_dmas(tile, lists, copy):
    n_big, n_small, big_src, big_dst, small_src, small_dst = lists
    max_big = big_src.shape[0] // n_big.shape[0]
    max_small = small_src.shape[0] // n_small.shape[0]

    def big(k, carry):
        idx = tile * max_big + k
        copy(pl.multiple_of(big_src[idx], MOE_ALIGN), pl.multiple_of(big_dst[idx], MOE_ALIGN),
             MOE_BIG * MOE_ALIGN).start()
        return carry

    def small(k, carry):
        idx = tile * max_small + k
        copy(pl.multiple_of(small_src[idx], MOE_ALIGN), pl.multiple_of(small_dst[idx], MOE_ALIGN),
             MOE_ALIGN).start()
        return carry

    lax.fori_loop(0, n_big[tile], big, 0)
    lax.fori_loop(0, n_small[tile], small, 0)


def _drain_dmas(n_big, n_small, copy):
    def big(c, carry):
        copy(0, 0, MOE_BIG * MOE_ALIGN).wait()
        return carry

    def small(c, carry):
        copy(0, 0, MOE_ALIGN).wait()
        return carry

    lax.fori_loop(0, n_big, big, 0)
    lax.fori_loop(0, n_small, small, 0)


def _disp_kernel(*refs, tm, nt):
    lists, (last_ref, h_ref, rt_ref, post_ref, xg_hbm, xs_sc, sems) = refs[:6], refs[6:]
    i = pl.program_id(0)
    slot = i % 2
    cap = xs_sc.shape[1]

    def runs(tile, sl, start):
        copy = lambda s, d, n: pltpu.make_async_copy(
            xs_sc.at[sl, pl.ds(s, n), :], xg_hbm.at[pl.ds(d, n), :], sems.at[sl])
        if start:
            _start_dmas(tile, lists, copy)
        else:
            _drain_dmas(lists[0][tile], lists[1][tile], copy)

    @pl.when(i == 0)
    def _():
        xs_sc[0, 0:MOE_ROWS, :] = jnp.zeros((MOE_ROWS, MOE_XCOLS), F32)
        zero_block = lambda blk: pltpu.make_async_copy(
            xs_sc.at[0, pl.ds(0, MOE_ROWS), :], xg_hbm.at[pl.ds(blk * MOE_ROWS, MOE_ROWS), :], sems.at[0])
        n_blocks = xg_hbm.shape[0] // MOE_ROWS

        def fill(e, act):
            @pl.when(last_ref[e] >= 0)
            def _():
                act(zero_block(last_ref[e]))

        def start_all(e, carry):
            fill(e, lambda cp: cp.start())
            return carry

        def wait_all(e, carry):
            fill(e, lambda cp: cp.wait())
            return carry

        def start_tail(blk, carry):
            zero_block(blk).start()
            return carry

        def wait_tail(blk, carry):
            zero_block(blk).wait()
            return carry

        lax.fori_loop(0, N_EXPERTS, start_all, 0)
        lax.fori_loop(last_ref[N_EXPERTS], n_blocks, start_tail, 0)
        lax.fori_loop(0, N_EXPERTS, wait_all, 0)
        lax.fori_loop(last_ref[N_EXPERTS], n_blocks, wait_tail, 0)

    @pl.when(i >= 2)
    def _():
        runs(i - 2, slot, False)

    post = post_ref[0]
    rows = lax.broadcasted_iota(I32, (cap, tm), 0)
    p1 = jnp.where(rows == post[0:1, :], 1.0, 0.0).astype(BF16)
    p2 = jnp.where(rows == post[1:2, :], 1.0, 0.0).astype(BF16)
    xs_sc[slot, :, 0:D_MODEL] = jnp.dot(p1 + p2, h_ref[...].astype(BF16), preferred_element_type=F32)
    rt = rt_ref[...]
    lane = lax.broadcasted_iota(I32, rt.shape, 1)

    def hi_lo(col):
        hi = col.astype(BF16).astype(F32)
        return jnp.where(lane == 0, hi, jnp.where(lane == 1, col - hi, 0.0)).astype(BF16)

    xs_sc[slot, :, D_MODEL:] = (jnp.dot(p1, hi_lo(rt[:, 2:3]), preferred_element_type=F32)
                                + jnp.dot(p2, hi_lo(rt[:, 3:4]), preferred_element_type=F32))
    runs(i, slot, True)

    @pl.when(i == nt - 1)
    def _():
        if nt > 1:
            runs(i - 1, 1 - slot, False)
        runs(i, slot, False)


def _disp_call(lists, last_blk, h2, route, post, n_rows, tm):
    T = h2.shape[0]
    nt = T // tm
    return pl.pallas_call(
        functools.partial(_disp_kernel, tm=tm, nt=nt),
        grid_spec=pltpu.PrefetchScalarGridSpec(
            num_scalar_prefetch=7, grid=(nt,),
            in_specs=[pl.BlockSpec((tm, D_MODEL), lambda i, *_: (i, 0)),
                      pl.BlockSpec((tm, LANES), lambda i, *_: (i, 0)),
                      pl.BlockSpec((1, 8, tm), lambda i, *_: (i, 0, 0))],
            out_specs=pl.BlockSpec(memory_space=pl.ANY),
            scratch_shapes=[pltpu.VMEM((2, _moe_cap(tm), MOE_XCOLS), F32), pltpu.SemaphoreType.DMA((2,))]),
        out_shape=_sds((n_rows, MOE_XCOLS), F32),
        compiler_params=_cp(("arbitrary",)), name="moe_dispatch",
    )(*lists, last_blk, h2, route, post)


def _expert_kernel(be_ref, nu_ref, x_ref, w1_ref, w3_ref, w2_ref, y_ref, w1b, w3b, w2b):
    i = pl.program_id(0)
    used = i < nu_ref[0]
    fresh = jnp.logical_or(i == 0, be_ref[i] != be_ref[jnp.maximum(i - 1, 0)])

    @pl.when(jnp.logical_and(used, fresh))
    def _():
        w1b[...] = w1_ref[0, 0].astype(BF16)
        w3b[...] = w3_ref[0, 0].astype(BF16)
        w2b[...] = w2_ref[0, 0].astype(BF16)

    @pl.when(used)
    def _():
        x = x_ref[:, 0:D_MODEL].astype(BF16)
        a = jnp.dot(x, w1b[...], preferred_element_type=F32)
        g = jnp.dot(x, w3b[...], preferred_element_type=F32)
        hm = (a * _sigmoid(a) * g).astype(BF16)
        row_w = x_ref[:, D_MODEL:D_MODEL + 1] + x_ref[:, D_MODEL + 1:D_MODEL + 2]
        y_ref[...] = jnp.dot(hm, w2b[...], preferred_element_type=F32) * row_w

    @pl.when(jnp.logical_not(used))
    def _():
        y_ref[...] = jnp.zeros(y_ref.shape, F32)


def _expert_call(blk_e, n_used, xg, w1, w3, w2, layer):
    P = xg.shape[0]
    nb = P // MOE_ROWS
    rowmap = lambda i, be, nu: (jnp.minimum(i, nu[0] - 1), 0)
    wmap = lambda i, be, nu: (layer, be[i], 0, 0)
    return pl.pallas_call(
        _expert_kernel,
        grid_spec=pltpu.PrefetchScalarGridSpec(
            num_scalar_prefetch=2, grid=(nb,),
            in_specs=[pl.BlockSpec((MOE_ROWS, MOE_XCOLS), rowmap),
                      pl.BlockSpec((1, 1, D_MODEL, D_EXPERT), wmap),
                      pl.BlockSpec((1, 1, D_MODEL, D_EXPERT), wmap),
                      pl.BlockSpec((1, 1, D_EXPERT, D_MODEL), wmap)],
            out_specs=pl.BlockSpec((MOE_ROWS, D_MODEL), lambda i, be, nu: (i, 0)),
            scratch_shapes=[pltpu.VMEM((D_MODEL, D_EXPERT), BF16),
                            pltpu.VMEM((D_MODEL, D_EXPERT), BF16),
                            pltpu.VMEM((D_EXPERT, D_MODEL), BF16)]),
        out_shape=_sds((P, D_MODEL), F32),
        compiler_params=_cp(("arbitrary",)), name="moe_experts",
    )(blk_e, n_used, xg, w1, w3, w2)


def _comb_kernel(*refs, tm, final, nt):
    lists, (x_ref, pos_ref, mod_ref, fnw_ref, y_hbm, o_ref, ys_sc, sems) = refs[:6], refs[6:]
    i = pl.program_id(0)
    slot = i % 2
    cap = ys_sc.shape[1]

    def runs(tile, sl, start):
        copy = lambda s, d, n: pltpu.make_async_copy(
            y_hbm.at[pl.ds(d, n), :], ys_sc.at[sl, pl.ds(s, n), :], sems.at[sl])
        if start:
            _start_dmas(tile, lists, copy)
        else:
            _drain_dmas(lists[0][tile], lists[1][tile], copy)

    def begin(tile, sl):
        ys_sc[sl, 2 * tm:cap, :] = jnp.zeros((cap - 2 * tm, D_MODEL), F32)
        runs(tile, sl, True)

    @pl.when(i == 0)
    def _():
        begin(0, 0)

    @pl.when(i + 1 < nt)
    def _():
        begin(i + 1, 1 - slot)

    runs(i, slot, False)
    pos = pos_ref[...]
    cols = lax.broadcasted_iota(I32, (tm, cap), 1)
    pick = (jnp.where(cols == pos[:, 0:1].astype(I32), 1.0, 0.0)
            + jnp.where(cols == pos[:, 1:2].astype(I32), 1.0, 0.0)).astype(BF16)
    y = jnp.dot(pick, ys_sc[slot].astype(BF16), preferred_element_type=F32)
    out = x_ref[...] + mod_ref[0, 2] * y
    if final:
        out = _rms(out) * fnw_ref[...]
    o_ref[...] = out


def _comb_call(lists, xn, pos, mod, fnw, yb, L, tm, final):
    T = xn.shape[0]
    nt = T // tm
    per_b = L // tm
    return pl.pallas_call(
        functools.partial(_comb_kernel, tm=tm, final=final, nt=nt),
        grid_spec=pltpu.PrefetchScalarGridSpec(
            num_scalar_prefetch=6, grid=(nt,),
            in_specs=[pl.BlockSpec((tm, D_MODEL), lambda i, *_: (i, 0)),
                      pl.BlockSpec((tm, LANES), lambda i, *_: (i, 0)),
                      pl.BlockSpec((1, 3, 1, D_MODEL), lambda i, *_: (i // per_b, 0, 0, 0)),
                      pl.BlockSpec((1, D_MODEL), lambda i, *_: (0, 0)),
                      pl.BlockSpec(memory_space=pl.ANY)],
            out_specs=pl.BlockSpec((tm, D_MODEL), lambda i, *_: (i, 0)),
            scratch_shapes=[pltpu.VMEM((2, _moe_cap(tm), D_MODEL), F32), pltpu.SemaphoreType.DMA((2,))]),
        out_shape=_sds((T, D_MODEL), F32),
        compiler_params=_cp(("arbitrary",)), name="moe_combine",
    )(*lists, xn, pos, mod, fnw, yb)


def _moe_cap(tm):
    return max(2 * tm + N_EXPERTS * MOE_ALIGN, MOE_ROWS)


def _dma_lists(groups, src, dst, cap):
    def expand(count, first, stride, width):
        ends = jnp.cumsum(count, axis=1)
        k = jnp.arange(width, dtype=I32)
        run = jnp.minimum(jnp.sum(ends[:, None, :] <= k[None, :, None], axis=2), N_EXPERTS - 1)
        sel = run[:, :, None] == jnp.arange(N_EXPERTS, dtype=I32)[None, None, :]
        pick = lambda a: jnp.sum(jnp.where(sel, a[:, None, :], 0), axis=2)
        off = pick(first) + (k[None, :] - pick(ends - count)) * stride
        return (pick(src) + off).reshape(-1).astype(I32), (pick(dst) + off).reshape(-1).astype(I32)

    big_rows = MOE_BIG * MOE_ALIGN
    n_big, n_small = groups // MOE_BIG, groups % MOE_BIG
    big_src, big_dst = expand(n_big, jnp.zeros_like(groups), big_rows, cap // big_rows)
    small_src, small_dst = expand(n_small, n_big * big_rows, MOE_ALIGN, N_EXPERTS * (MOE_BIG - 1))
    total = lambda a: jnp.sum(a, axis=1).astype(I32)
    return total(n_big), total(n_small), big_src, big_dst, small_src, small_dst


def _moe(xn, h2, route, mod, fnw, w1, w3, w2, layer, L, final):
    T = xn.shape[0]
    tm = min(MOE_TILE, L)
    nt = T // tm
    pos, post, counts = _rank_call(route, tm)
    cnt = counts[:, 0, :N_EXPERTS].astype(I32)
    cnt = (cnt + MOE_ALIGN - 1) // MOE_ALIGN * MOE_ALIGN
    total = jnp.sum(cnt, axis=0)
    padded = (total + MOE_ROWS - 1) // MOE_ROWS * MOE_ROWS
    pend = jnp.cumsum(padded)
    seg_dst = (pend - padded)[None, :] + jnp.cumsum(cnt, axis=0) - cnt
    seg_src = jnp.cumsum(cnt, axis=1) - cnt
    nb = -(-(2 * T + nt * N_EXPERTS * (MOE_ALIGN - 1) + N_EXPERTS * (MOE_ROWS - 1)) // MOE_ROWS)
    blk_e = jnp.minimum(jnp.sum(pend[None, :] <= (jnp.arange(nb, dtype=I32) * MOE_ROWS)[:, None], axis=1),
                        N_EXPERTS - 1).astype(I32)
    n_used = (pend[N_EXPERTS - 1:] // MOE_ROWS).astype(I32)
    last_blk = jnp.concatenate([jnp.where(padded > 0, pend // MOE_ROWS - 1, -1).astype(I32), n_used])
    lists = _dma_lists(cnt // MOE_ALIGN, seg_src, seg_dst, _moe_cap(tm))
    xg = _disp_call(lists, last_blk, h2, route, post, nb * MOE_ROWS, tm)
    yb = _expert_call(blk_e, n_used, xg, w1, w3, w2, layer)
    return _comb_call(lists, xn, pos, mod, fnw, yb, L, tm, final)


def _in1_kernel(x_ref, mod_ref, nw_ref, win_ref, qnw_ref, wuq_ref, kvnw_ref, tq_ref, tk_ref,
                q_ref, ckv_ref, kr_ref):
    h = _norm_mod(x_ref[...], nw_ref[...], mod_ref[0, 0], mod_ref[0, 1])
    zz = _dot_bf(h, win_ref[...])
    cqn = _rms(zz[:, :MLA_Q_RANK]) * qnw_ref[...]
    q = _dot_bf(cqn, wuq_ref[...])
    tq = tq_ref[...]
    for hh in range(MLA_H):
        q_ref[:, hh * LANES:(hh + 1) * LANES] = (q[:, hh * LANES:(hh + 1) * LANES] * tq).astype(BF16)
    c0 = MLA_Q_RANK
    ckv_ref[...] = _rms(zz[:, c0:c0 + MLA_KV_RANK]) * kvnw_ref[...]
    c1 = c0 + MLA_KV_RANK
    tk = tk_ref[...]
    kr_ref[...] = (zz[:, c1:c1 + MLA_ROPE] * tk[:, :MLA_ROPE]
                   + zz[:, c1 + LANES:c1 + LANES + MLA_ROPE] * tk[:, MLA_ROPE:])


def _in1_call(x, mod, nw, w_in_r, q_nw, w_uq_r, kv_nw, tab_q, tab_k, L, tm):
    T = x.shape[0]
    per_b = L // tm
    tok = lambda i: (i, 0)
    const = lambda i: (0, 0)
    pos = lambda i: (i % per_b, 0)
    return pl.pallas_call(
        _in1_kernel, grid=(T // tm,),
        in_specs=[pl.BlockSpec((tm, D_MODEL), tok),
                  pl.BlockSpec((1, 3, 1, D_MODEL), lambda i: (i // per_b, 0, 0, 0)),
                  pl.BlockSpec((1, D_MODEL), const),
                  pl.BlockSpec((D_MODEL, MLA_IN_R), const),
                  pl.BlockSpec((1, MLA_Q_RANK), const),
                  pl.BlockSpec((MLA_Q_RANK, MLA_H * LANES), const),
                  pl.BlockSpec((1, MLA_KV_RANK), const),
                  pl.BlockSpec((tm, LANES), pos),
                  pl.BlockSpec((tm, 2 * MLA_ROPE), pos)],
        out_specs=[pl.BlockSpec((tm, MLA_H * LANES), tok),
                   pl.BlockSpec((tm, MLA_KV_RANK), tok),
                   pl.BlockSpec((tm, MLA_ROPE), tok)],
        out_shape=[_sds((T, MLA_H * LANES), BF16), _sds((T, MLA_KV_RANK), F32),
                   _sds((T, MLA_ROPE), F32)],
        compiler_params=_cp(("arbitrary",)), name="in1",
    )(x, mod, nw, w_in_r, q_nw, w_uq_r, kv_nw, tab_q, tab_k)


def _kv_kernel(ckv_ref, kr_ref, wk_ref, pe_ref, wv_ref, one_ref, k_ref, v_ref):
    c = ckv_ref[...].astype(BF16)
    k_ref[...] = (jnp.dot(c, wk_ref[...], preferred_element_type=F32)
                  + jnp.dot(kr_ref[...].astype(BF16), pe_ref[...], preferred_element_type=F32)).astype(BF16)
    v_ref[...] = (jnp.dot(c, wv_ref[...], preferred_element_type=F32) + one_ref[...]).astype(BF16)


def _kv_call(ckv, kr, wk_r, place, wv_r, ones_row, tm):
    T = ckv.shape[0]
    tok = lambda i: (i, 0)
    const = lambda i: (0, 0)
    wide = MLA_H * LANES
    return pl.pallas_call(
        _kv_kernel, grid=(T // tm,),
        in_specs=[pl.BlockSpec((tm, MLA_KV_RANK), tok),
                  pl.BlockSpec((tm, MLA_ROPE), tok),
                  pl.BlockSpec((MLA_KV_RANK, wide), const),
                  pl.BlockSpec((MLA_ROPE, wide), const),
                  pl.BlockSpec((MLA_KV_RANK, wide), const),
                  pl.BlockSpec((1, wide), const)],
        out_specs=[pl.BlockSpec((tm, wide), tok), pl.BlockSpec((tm, wide), tok)],
        out_shape=[_sds((T, wide), BF16), _sds((T, wide), BF16)],
        compiler_params=_cp(("arbitrary",)), name="mla_kv",
    )(ckv, kr, wk_r, place, wv_r, ones_row)


def _attn_kernel(q_ref, k_ref, v_ref, o_ref, m0, m1, a0, a1, s0, s1, *, tq, tk, Lk, pos0):
    m_scs, acc_scs, s_scs = (m0, m1), (a0, a1), (s0, s1)
    q_lo = pos0 + pl.program_id(2) * tq
    k_hi = jnp.minimum(((q_lo + tq - 1) // CHUNK + 1) * CHUNK, Lk)
    n_blk = (k_hi + tk - 1) // tk
    n_full = jnp.minimum(((q_lo // CHUNK + 1) * CHUNK) // tk, n_blk)
    q_chunk = jnp.right_shift(q_lo + lax.broadcasted_iota(I32, (tq, tk), 0), CHUNK_SHIFT)
    k_iota = lax.broadcasted_iota(I32, (tq, tk), 1)

    def scores(hh, k0):
        hc = slice(hh * LANES, (hh + 1) * LANES)
        return lax.dot_general(q_ref[:, hc], k_ref[pl.ds(k0, tk), hc], NT, preferred_element_type=F32)

    for hh in range(2):
        m_scs[hh][...] = jnp.full((tq, LANES), NEG, F32)
        acc_scs[hh][...] = jnp.zeros((tq, LANES), F32)
        s_scs[hh][...] = scores(hh, 0)

    def step(j, masked, look_ahead):
        k0 = pl.multiple_of(j * tk, tk)
        if look_ahead:
            ahead = [scores(hh, pl.multiple_of((j + 1) * tk, tk)) for hh in range(2)]
        for hh in range(2):
            hc = slice(hh * LANES, (hh + 1) * LANES)
            s = s_scs[hh][...]
            if masked:
                s = jnp.where(q_chunk >= jnp.right_shift(k0 + k_iota, CHUNK_SHIFT), s, NEG)
            m_prev = m_scs[hh][...]
            m_new = jnp.maximum(m_prev, jnp.max(s, axis=-1, keepdims=True))
            if tk % LANES == 0:
                p = jnp.exp2(s - jnp.concatenate([m_new] * (tk // LANES), axis=1))
            else:
                p = jnp.exp2(s - m_new[:, 0:1])
            acc_scs[hh][...] = (jnp.exp2(m_prev - m_new) * acc_scs[hh][...]
                                + jnp.dot(p.astype(BF16), v_ref[pl.ds(k0, tk), hc],
                                          preferred_element_type=F32))
            m_scs[hh][...] = m_new
        if look_ahead:
            for hh in range(2):
                s_scs[hh][...] = ahead[hh]

    def loop(lo, hi, masked, unroll=1):
        def body(jj, carry):
            for u in range(unroll):
                step(lo + jj * unroll + u, masked, True)
            return carry
        lax.fori_loop(0, (hi - lo) // unroll, body, 0)
        return lo + (hi - lo) // unroll * unroll

    n_pre = jnp.minimum(n_full, n_blk - 1)
    done = loop(0, n_pre, False, unroll=4)
    done = loop(done, n_pre, False, unroll=2)
    loop(done, n_pre, False)
    loop(n_full, n_blk - 1, True)

    @pl.when(n_full < n_blk)
    def _():
        step(n_blk - 1, True, False)

    @pl.when(n_full == n_blk)
    def _():
        step(n_blk - 1, False, False)

    outs = []
    for hh in range(2):
        acc = acc_scs[hh][...]
        outs.append(acc[:, :MLA_V] / acc[:, MLA_V:MLA_V + 1])
    o_ref[...] = jnp.concatenate(outs, axis=1).astype(BF16)


def _attn_call(q, k, v, B, Lq, Lk, pos0, tq, tk):
    nq = Lq // tq
    return pl.pallas_call(
        functools.partial(_attn_kernel, tq=tq, tk=tk, Lk=Lk, pos0=pos0), grid=(B, MLA_H // 2, nq),
        in_specs=[pl.BlockSpec((tq, 2 * LANES), lambda b, hp, i: (b * nq + i, hp)),
                  pl.BlockSpec((Lk, 2 * LANES), lambda b, hp, i: (b, hp)),
                  pl.BlockSpec((Lk, 2 * LANES), lambda b, hp, i: (b, hp))],
        out_specs=pl.BlockSpec((tq, 2 * MLA_V), lambda b, hp, i: (b * nq + i, hp)),
        out_shape=_sds((B * Lq, MLA_H * MLA_V), BF16),
        scratch_shapes=[pltpu.VMEM((tq, LANES), F32)] * 4 + [pltpu.VMEM((tq, tk), F32)] * 2,
        compiler_params=_cp(("arbitrary", "arbitrary", "arbitrary")), name="mla_attn",
    )(q, k, v)


def _split_cols(w, widths):
    offs = [0]
    for n in widths:
        offs.append(offs[-1] + n)
    return [w[:, offs[i]:offs[i + 1]] for i in range(len(widths))]


def _hi_lo_pair(w):
    hi = w.astype(BF16)
    return jnp.stack([hi, (w - hi.astype(F32)).astype(BF16)], axis=-3)


def _prep_even(ab_w_in, gla_w_alpha, gdn_a_log, gdn_dt_bias):
    gq, gk, gv, glr, gr, qkv, ga, gb, gg = _split_cols(ab_w_in, AB_IN_WIDTHS)
    w_main = jnp.concatenate([gq, gk, gv, gr, qkv, gg], axis=1).astype(BF16)
    small = jnp.concatenate([glr, ga, gb], axis=1)
    w_small = _hi_lo_pair(jnp.pad(small, ((0, 0), (0, LANES - small.shape[1]))))
    wa_pad = jnp.pad(gla_w_alpha, ((0, LANES - GLA_LR), (0, 0)))
    alog_v = jnp.zeros((1, LANES), F32).at[0, LANE_GA:LANE_GA + GDN_H].set(gdn_a_log)
    dtb_v = jnp.zeros((1, LANES), F32).at[0, LANE_GA:LANE_GA + GDN_H].set(gdn_dt_bias)
    return w_main, w_small, wa_pad, alog_v, dtb_v


def _swap_halves(w):
    half = w.shape[-1] // 2
    return jnp.concatenate([w[..., half:], w[..., :half]], axis=-1)


def _prep_odd(mla_w_in, mla_w_uq, mla_w_ukv):
    cq, ckv, kr = _split_cols(mla_w_in, (MLA_Q_RANK, MLA_KV_RANK, MLA_ROPE))
    pad = lambda w: jnp.pad(w, ((0, 0), (0, LANES - w.shape[1])))
    w_in_r = jnp.concatenate([cq, ckv, pad(kr), pad(_swap_halves(kr))], axis=1).astype(BF16)
    uq = mla_w_uq.reshape(MLA_Q_RANK, MLA_H, MLA_NOPE + MLA_ROPE)
    uq_rope = uq[..., MLA_NOPE:]
    w_uq_r = jnp.concatenate([uq, _swap_halves(uq_rope)], axis=-1).reshape(
        MLA_Q_RANK, MLA_H * LANES).astype(BF16)
    ukv = mla_w_ukv.reshape(MLA_KV_RANK, MLA_H, MLA_NOPE + MLA_V)
    wk_r = jnp.pad(ukv[..., :MLA_NOPE], ((0, 0), (0, 0), (0, LANES - MLA_NOPE))).reshape(
        MLA_KV_RANK, MLA_H * LANES).astype(BF16)
    wv_r = jnp.pad(ukv[..., MLA_NOPE:], ((0, 0), (0, 0), (0, LANES - MLA_V))).reshape(
        MLA_KV_RANK, MLA_H * LANES).astype(BF16)
    ones_row = jnp.tile((jnp.arange(LANES) == MLA_V).astype(F32), MLA_H)[None, :]
    eye = jnp.eye(MLA_ROPE, dtype=F32)
    place = jnp.concatenate([jnp.zeros((MLA_ROPE, MLA_NOPE), F32), eye, eye], axis=1)
    place = jnp.tile(place, (1, MLA_H)).astype(BF16)
    return w_in_r, w_uq_r, wk_r, wv_r, place, ones_row


def _rope_tables(pos0, L):
    half = MLA_ROPE // 2
    inv = jnp.exp(-math.log(ROPE_THETA) * jnp.arange(half, dtype=F32) / half)
    ang = (pos0 + jnp.arange(L, dtype=I32)).astype(F32)[:, None] * inv[None, :]
    cos, sin = jnp.cos(ang), jnp.sin(ang)
    tab_k = jnp.concatenate([cos, cos, -sin, sin], axis=1)
    scale = (MLA_NOPE + MLA_ROPE) ** -0.5 * math.log2(math.e)
    tab_q = jnp.concatenate([jnp.ones((L, MLA_NOPE), F32), tab_k], axis=1) * scale
    return tab_q, tab_k


def _mod4(mods, lo, hi):
    return [mods[i, lo:hi].reshape(hi - lo, 3, 1, D_MODEL) for i in range(mods.shape[0])]


def _trunk(x3, mods, pos0, gla_s, gdn_s, conv_s, past_ckv, past_kr, p):
    B, L, _ = x3.shape
    T = B * L
    tm = min(512, L)
    x = x3.reshape(T, D_MODEL)
    row = lambda a: a.reshape(1, -1)

    w_main, w_small, wa_pad, alog_v, dtb_v = p['even']
    zm, zs = _in0_call(x, mods[0], row(p['norm_w'][0, 0]), w_main, w_small, L, tm)
    o_gla, gla_t = _gla_call(zm, zs, wa_pad, row(p['gla_b_alpha'][0]), row(p['gla_norm_w'][0]),
                             jnp.swapaxes(gla_s, -1, -2), B, L)
    hist8 = jnp.pad(conv_s, ((0, 0), (8 - (CONV_W - 1), 0), (0, 0)))
    o_gdn, gdn_new = _gdn_call(zm, zs, p['gdn_conv_w'][0], alog_v, dtb_v, row(p['gdn_norm_w'][0]),
                               hist8, gdn_s, B, L)
    qkv0 = 3 * GLA_H * GLA_DV
    conv_new = zm.reshape(B, L, AB_MAIN)[:, L - (CONV_W - 1):, qkv0:qkv0 + GDN_CONV_DIM].astype(F32)
    w_out = p['ab_w_out_bf']
    half = GLA_H * GLA_DV
    xn, h2, route = _post_call([o_gla, o_gdn], [w_out[:half], w_out[half:]], x, mods[0], mods[1],
                               row(p['norm_w'][0, 1]), p['w_route'][0], p['b_route'][0], L, tm)
    x = _moe(xn, h2, route, mods[1], row(p['final_norm_w']), p['moe_w1'], p['moe_w3'], p['moe_w2'],
             0, L, final=False)

    w_in_r, w_uq_r, wk_r, wv_r, place, ones_row = p['odd']
    tab_q, tab_k = _rope_tables(pos0, L)
    q, ckv, kr = _in1_call(x, mods[2], row(p['norm_w'][1, 0]), w_in_r, row(p['mla_q_norm_w'][0]),
                           w_uq_r, row(p['mla_kv_norm_w'][0]), tab_q, tab_k, L, tm)
    if past_ckv is None:
        ckv_all, kr_all, Lk = ckv, kr, L
    else:
        Lk = past_ckv.shape[1] + L
        ckv_all = jnp.concatenate([past_ckv, ckv.reshape(B, L, -1)], axis=1).reshape(B * Lk, -1)
        kr_all = jnp.concatenate([past_kr, kr.reshape(B, L, -1)], axis=1).reshape(B * Lk, -1)
    tkv = 512 if (B * Lk) % 512 == 0 else Lk
    k_all, v_all = _kv_call(ckv_all, kr_all, wk_r, place, wv_r, ones_row, tkv)
    tq = min(ATTN_TILE, L)
    tk = ATTN_TILE if Lk % ATTN_TILE == 0 else Lk
    att = _attn_call(q, k_all, v_all, B, L, Lk, pos0, tq, tk)
    xn, h2, route = _post_call([att], [p['mla_w_out_bf']], x, mods[2], mods[3],
                               row(p['norm_w'][1, 1]), p['w_route'][1], p['b_route'][1], L, tm)
    y = _moe(xn, h2, route, mods[3], row(p['final_norm_w']), p['moe_w1'], p['moe_w3'], p['moe_w2'],
             1, L, final=True)
    return (y.reshape(B, L, D_MODEL), jnp.swapaxes(gla_t, -1, -2)[None], gdn_new[None], conv_new[None],
            ckv.reshape(1, B, L, MLA_KV_RANK), kr.reshape(1, B, L, MLA_ROPE))


def kernel(x_prompt, x_sample, state_gla, state_gdn, state_gdn_conv, cache_mla_ckv, cache_mla_krope,
           c_prompt, c_sample, ada_w, ada_b, norm_w, final_norm_w, ab_w_in, gla_w_alpha, gla_b_alpha,
           gla_norm_w, gdn_conv_w, gdn_a_log, gdn_dt_bias, gdn_norm_w, ab_w_out, mla_w_in, mla_q_norm_w,
           mla_w_uq, mla_kv_norm_w, mla_w_ukv, mla_w_out, moe_w_group, moe_b_group, moe_w_expert,
           moe_b_expert, moe_w1, moe_w3, moe_w2):
    depth = ada_w.shape[0]
    bp, bs = x_prompt.shape[0], x_sample.shape[0]
    mods = _ada_call(jnp.concatenate([c_prompt, c_sample], axis=0),
                     ada_w.reshape(2 * depth, D_MODEL, 3 * D_MODEL), ada_b.reshape(2 * depth, 1, 3 * D_MODEL))
    w_route = _hi_lo_pair(jnp.pad(jnp.concatenate([moe_w_group, moe_w_expert], axis=-1),
                                  ((0, 0), (0, 0), (0, LANES - N_GROUPS - N_EXPERTS))))
    b_route = jnp.pad(jnp.concatenate([moe_b_group, moe_b_expert], axis=-1),
                      ((0, 0), (0, LANES - N_GROUPS - N_EXPERTS)))[:, None, :]
    p = dict(norm_w=norm_w, final_norm_w=final_norm_w, gla_b_alpha=gla_b_alpha, gla_norm_w=gla_norm_w,
             gdn_conv_w=gdn_conv_w, gdn_norm_w=gdn_norm_w, mla_q_norm_w=mla_q_norm_w,
             mla_kv_norm_w=mla_kv_norm_w, moe_w1=moe_w1, moe_w3=moe_w3, moe_w2=moe_w2,
             w_route=w_route, b_route=b_route,
             even=_prep_even(ab_w_in[0], gla_w_alpha[0], gdn_a_log[0], gdn_dt_bias[0]),
             odd=_prep_odd(mla_w_in[0], mla_w_uq[0], mla_w_ukv[0]),
             ab_w_out_bf=ab_w_out[0].astype(BF16), mla_w_out_bf=mla_w_out[0].astype(BF16))
    y_p, gla_p, gdn_p, conv_p, ckv_p, kr_p = _trunk(
        x_prompt, _mod4(mods, 0, bp), 0,
        jnp.zeros((bp, GLA_H, GLA_DK, GLA_DV), F32), jnp.zeros((bp, GDN_H, GDN_DK, GDN_DV), F32),
        jnp.zeros((bp, CONV_W - 1, GDN_CONV_DIM), F32), None, None, p)
    y_s, gla_s, gdn_s, conv_s, ckv_s, kr_s = _trunk(
        x_sample, _mod4(mods, bp, bp + bs), cache_mla_ckv.shape[2],
        state_gla[0], state_gdn[0], state_gdn_conv[0], cache_mla_ckv[0], cache_mla_krope[0], p)
    return (y_p, y_s, gla_p, gdn_p, conv_p, ckv_p, kr_p, gla_s, gdn_s, conv_s, ckv_s, kr_s)
```
